```python
import math
import jax
import jax.numpy as jnp
from jax import lax
import numpy as np

D_MODEL = 1024
BATCH = 2
SEQ = 8192
DEPTH = 2

N_EVEN = (DEPTH + 1) // 2
N_ODD = DEPTH // 2
Q_BLOCK = 128
EPS = 1e-6
SB_HEADS = 8
SB_HEAD_DIM = 64
SB_WIDTH = SB_HEADS * SB_HEAD_DIM
MLA_HEADS = 8
MLA_Q_RANK = 256
MLA_KV_RANK = 128
MLA_NOPE_DIM = 64
MLA_ROPE_DIM = 32
MLA_QK_DIM = MLA_NOPE_DIM + MLA_ROPE_DIM
MLA_V_DIM = 64
MLA_WIDTH = MLA_HEADS * MLA_V_DIM
ROPE_THETA = 10000.0
OFF_SB_K = SB_WIDTH
OFF_SB_V = 2 * SB_WIDTH
OFF_CQ = 3 * SB_WIDTH
OFF_CKV = OFF_CQ + MLA_Q_RANK
OFF_KR = OFF_CKV + MLA_KV_RANK
IN_COLS = OFF_KR + MLA_ROPE_DIM
MIX_WIDTH = SB_WIDTH + MLA_WIDTH
SSM_WIDTH = D_MODEL // 2
SSM_GROUP = 16
SSM_GROUPS = SSM_WIDTH // SSM_GROUP
SSM_STATE = 64
DT_MIN = 1e-3
DT_MAX = 1e-1
EIG_RE_MAX = -1e-4
D_FF = 7 * D_MODEL // 2
N_EXPERTS = 8
TOP_K = 2
EXPERT_BLOCK = 128

kernel_name = 'hybrid_stickbreak_mla_s5_moe'


def _rms_norm(x, gain):
    xf = x.astype(jnp.float32)
    y = xf * lax.rsqrt(jnp.mean(xf * xf, axis=-1, keepdims=True) + EPS)
    return (y * gain.astype(jnp.float32)).astype(x.dtype)


def _to_heads(t, heads):
    b, s, _ = t.shape
    return t.reshape(b, s, heads, -1).transpose(0, 2, 1, 3).astype(jnp.float32)


def _from_heads(t):
    b, h, s, d = t.shape
    return t.transpose(0, 2, 1, 3).reshape(b, s, h * d)


def _sweep_query_blocks(block_fn, q, k, v):
    b, h, s, dq = q.shape
    nb = s // Q_BLOCK
    qb = q.reshape(b, h, nb, Q_BLOCK, dq).transpose(2, 0, 1, 3, 4)
    starts = jnp.arange(nb, dtype=jnp.int32) * Q_BLOCK
    out = lax.map(lambda args: block_fn(args[0], args[1], k, v), (qb, starts))
    return out.transpose(1, 2, 0, 3, 4).reshape(b, h, s, v.shape[-1])


def _stick_breaking_block(q_blk, t0, k, v):
    s_len = k.shape[2]
    t_idx = t0 + jnp.arange(Q_BLOCK, dtype=jnp.int32)
    s_idx = jnp.arange(s_len, dtype=jnp.int32)
    earlier = s_idx[None, :] < t_idx[:, None]
    z = jnp.einsum('bhqd,bhkd->bhqk', q_blk, k) * (q_blk.shape[-1] ** -0.5)
    log_fail = jnp.where(earlier, jax.nn.log_sigmoid(-z), 0.0)
    log_stick = lax.cumsum(log_fail, axis=3, reverse=True) - log_fail
    w = jnp.where(earlier, jnp.exp(jax.nn.log_sigmoid(z) + log_stick), 0.0)
    return jnp.einsum('bhqk,bhkd->bhqd', w, v)


def _causal_softmax_block(q_blk, t0, k, v):
    s_len = k.shape[2]
    t_idx = t0 + jnp.arange(Q_BLOCK, dtype=jnp.int32)
    s_idx = jnp.arange(s_len, dtype=jnp.int32)
    visible = s_idx[None, :] <= t_idx[:, None]
    z = jnp.einsum('bhqd,bhkd->bhqk', q_blk, k) * (q_blk.shape[-1] ** -0.5)
    p = jax.nn.softmax(jnp.where(visible, z, -jnp.inf), axis=-1)
    return jnp.einsum('bhqk,bhkd->bhqd', p, v)


def _rope_tables(s_len):
    inv_freq = ROPE_THETA ** (-jnp.arange(0, MLA_ROPE_DIM, 2, dtype=jnp.float32) / MLA_ROPE_DIM)
    ang = jnp.arange(s_len, dtype=jnp.float32)[:, None] * inv_freq[None, :]
    return jnp.cos(ang), jnp.sin(ang)


def _rope_tail(t, cos, sin):
    half = MLA_ROPE_DIM // 2
    nope = t[..., :MLA_NOPE_DIM]
    r1 = t[..., MLA_NOPE_DIM:MLA_NOPE_DIM + half]
    r2 = t[..., MLA_NOPE_DIM + half:]
    c = cos[None, :, None, :]
    s = sin[None, :, None, :]
    return jnp.concatenate([nope, r1 * c - r2 * s, r2 * c + r1 * s], axis=-1)


def _attention_layer(x, norm_g, w_in, q_lat_g, w_q_up, kv_lat_g, w_kv_up, q_g, k_g, w_out):
    b, s, _ = x.shape
    h = _rms_norm(x, norm_g) @ w_in
    o_sb = _sweep_query_blocks(
        _stick_breaking_block,
        _to_heads(h[..., :OFF_SB_K], SB_HEADS),
        _to_heads(h[..., OFF_SB_K:OFF_SB_V], SB_HEADS),
        _to_heads(h[..., OFF_SB_V:OFF_CQ], SB_HEADS))
    c_q = _rms_norm(h[..., OFF_CQ:OFF_CKV], q_lat_g)
    c_kv = _rms_norm(h[..., OFF_CKV:OFF_KR], kv_lat_g)
    k_rope = h[..., OFF_KR:]
    q = (c_q @ w_q_up).reshape(b, s, MLA_HEADS, MLA_QK_DIM)
    kv = (c_kv @ w_kv_up).reshape(b, s, MLA_HEADS, MLA_NOPE_DIM + MLA_V_DIM)
    k = jnp.concatenate(
        [kv[..., :MLA_NOPE_DIM],
         jnp.broadcast_to(k_rope[:, :, None, :], (b, s, MLA_HEADS, MLA_ROPE_DIM))], axis=-1)
    v = kv[..., MLA_NOPE_DIM:].astype(jnp.float32)
    q = _rms_norm(q, q_g).astype(jnp.float32)
    k = _rms_norm(k, k_g).astype(jnp.float32)
    cos, sin = _rope_tables(s)
    q = _rope_tail(q, cos, sin)
    k = _rope_tail(k, cos, sin)
    o_mla = _sweep_query_blocks(
        _causal_softmax_block,
        q.transpose(0, 2, 1, 3), k.transpose(0, 2, 1, 3), v.transpose(0, 2, 1, 3))
    merged = jnp.concatenate([_from_heads(o_sb), _from_heads(o_mla)], axis=-1)
    return x + merged.astype(x.dtype) @ w_out


def _dense_swiglu_layer(x, norm_g, w_gate, w_up, w_down):
    xn = _rms_norm(x, norm_g)
    return x + (jax.nn.silu(xn @ w_gate) * (xn @ w_up)) @ w_down


def _s5_discretize(a_re, a_im, log_dt, b_re, b_im):
    a_re = jnp.minimum(a_re.astype(jnp.float32), EIG_RE_MAX)
    a_im = a_im.astype(jnp.float32)
    dt = jnp.exp(log_dt.astype(jnp.float32))[:, None]
    mag = jnp.exp(a_re * dt)
    lam_re = mag * jnp.cos(a_im * dt)
    lam_im = mag * jnp.sin(a_im * dt)
    den = a_re * a_re + a_im * a_im
    num_re = lam_re - 1.0
    coef_re = ((num_re * a_re + lam_im * a_im) / den)[..., None]
    coef_im = ((lam_im * a_re - num_re * a_im) / den)[..., None]
    b_re = b_re.astype(jnp.float32)
    b_im = b_im.astype(jnp.float32)
    bb_re = coef_re * b_re - coef_im * b_im
    bb_im = coef_re * b_im + coef_im * b_re
    return lam_re, lam_im, bb_re, bb_im


def _complex_affine_combine(first, second):
    a1r, a1i, b1r, b1i = first
    a2r, a2i, b2r, b2i = second
    return (a2r * a1r - a2i * a1i,
            a2r * a1i + a2i * a1r,
            a2r * b1r - a2i * b1i + b2r,
            a2r * b1i + a2i * b1r + b2i)


def _s5_layer(x, norm_g, w_in, a_re, a_im, log_dt, b_re, b_im, c_re, c_im, d_skip, w_glu):
    b, s, _ = x.shape
    u = (_rms_norm(x, norm_g) @ w_in).astype(jnp.float32)
    lam_re, lam_im, bb_re, bb_im = _s5_discretize(a_re, a_im, log_dt, b_re, b_im)
    ug = u.reshape(b, s, SSM_GROUPS, SSM_GROUP)
    bu_re = jnp.einsum('bsgc,gnc->bsgn', ug, bb_re)
    bu_im = jnp.einsum('bsgc,gnc->bsgn', ug, bb_im)
    full = bu_re.shape
    _, _, h_re, h_im = lax.associative_scan(
        _complex_affine_combine,
        (jnp.broadcast_to(lam_re, full), jnp.broadcast_to(lam_im, full), bu_re, bu_im),
        axis=1)
    y = (jnp.einsum('bsgn,gcn->bsgc', h_re, c_re.astype(jnp.float32))
         - jnp.einsum('bsgn,gcn->bsgc', h_im, c_im.astype(jnp.float32))).reshape(b, s, SSM_WIDTH)
    y = jax.nn.gelu(y + d_skip.astype(jnp.float32) * u).astype(x.dtype)
    z = y @ w_glu
    return x + z[..., :D_MODEL] * jax.nn.sigmoid(z[..., D_MODEL:])


def _moe_swiglu_layer(x, norm_g, w_router, w_gate, w_up, w_down):
    b, s, d = x.shape
    xt = _rms_norm(x, norm_g).reshape(-1, d)
    n_tok = xt.shape[0]
    logits = xt.astype(jnp.float32) @ w_router.astype(jnp.float32)
    top_vals, top_idx = lax.top_k(logits, TOP_K)
    gates = jax.nn.softmax(top_vals, axis=-1)
    n_assign = n_tok * TOP_K
    flat_e = top_idx.reshape(-1).astype(jnp.int32)
    flat_tok = jnp.repeat(jnp.arange(n_tok, dtype=jnp.int32), TOP_K)
    flat_w = gates.reshape(-1)
    order = jnp.argsort(flat_e)
    e_sorted = flat_e[order]
    counts = jnp.bincount(flat_e, length=N_EXPERTS).astype(jnp.int32)
    padded = (counts + EXPERT_BLOCK - 1) // EXPERT_BLOCK * EXPERT_BLOCK
    pad_end = jnp.cumsum(padded)
    pad_start = pad_end - padded
    unpad_start = jnp.cumsum(counts) - counts
    dest = pad_start[e_sorted] + jnp.arange(n_assign, dtype=jnp.int32) - unpad_start[e_sorted]
    n_slots = (n_assign + EXPERT_BLOCK - 1) // EXPERT_BLOCK * EXPERT_BLOCK + N_EXPERTS * EXPERT_BLOCK
    n_blocks = n_slots // EXPERT_BLOCK
    slot_tok = jnp.zeros((n_slots,), jnp.int32).at[dest].set(flat_tok[order])
    slot_w = jnp.zeros((n_slots,), jnp.float32).at[dest].set(flat_w[order])
    blk_start = jnp.arange(n_blocks, dtype=jnp.int32) * EXPERT_BLOCK
    blk_expert = jnp.clip(jnp.searchsorted(pad_end, blk_start, side='right'), 0, N_EXPERTS - 1)
    xs = xt[slot_tok].reshape(n_blocks, EXPERT_BLOCK, d)

    def expert_block(args):
        xb, e = args
        return (jax.nn.silu(xb @ w_gate[e]) * (xb @ w_up[e])) @ w_down[e]

    ys = lax.map(expert_block, (xs, blk_expert)).reshape(n_slots, d)
    out = jax.ops.segment_sum(ys.astype(jnp.float32) * slot_w[:, None], slot_tok, num_segments=n_tok)
    return x + out.reshape(b, s, d).astype(x.dtype)


def setup_inputs(seed: int = 0) -> dict:
    key = jax.random.key(seed)
    ks = jax.random.split(key, 30)
    f32 = jnp.float32

    def nrm(k, shape, scale):
        return jax.random.normal(k, shape, f32) * scale

    def gain(k, shape):
        return 1.0 + 0.02 * jax.random.normal(k, shape, f32)

    ne, no = N_EVEN, N_ODD
    return {
        'x': jax.random.normal(ks[0], (BATCH, SEQ, D_MODEL), f32),
        'att_norm': gain(ks[1], (ne, D_MODEL)),
        'att_w_in': nrm(ks[2], (ne, D_MODEL, IN_COLS), D_MODEL ** -0.5),
        'att_q_latent_norm': gain(ks[3], (ne, MLA_Q_RANK)),
        'att_w_q_up': nrm(ks[4], (ne, MLA_Q_RANK, MLA_HEADS * MLA_QK_DIM), MLA_Q_RANK ** -0.5),
        'att_kv_latent_norm': gain(ks[5], (ne, MLA_KV_RANK)),
        'att_w_kv_up': nrm(ks[6], (ne, MLA_KV_RANK, MLA_HEADS * (MLA_NOPE_DIM + MLA_V_DIM)), MLA_KV_RANK ** -0.5),
        'att_q_norm': gain(ks[7], (ne, MLA_QK_DIM)),
        'att_k_norm': gain(ks[8], (ne, MLA_QK_DIM)),
        'att_w_out': nrm(ks[9], (ne, MIX_WIDTH, D_MODEL), MIX_WIDTH ** -0.5),
        'dffn_norm': gain(ks[10], (ne, D_MODEL)),
        'dffn_w_gate': nrm(ks[11], (ne, D_MODEL, D_FF), D_MODEL ** -0.5),
        'dffn_w_up': nrm(ks[12], (ne, D_MODEL, D_FF), D_MODEL ** -0.5),
        'dffn_w_down': nrm(ks[13], (ne, D_FF, D_MODEL), D_FF ** -0.5),
        'ssm_norm': gain(ks[14], (no, D_MODEL)),
        'ssm_w_in': nrm(ks[15], (no, D_MODEL, SSM_WIDTH), D_MODEL ** -0.5),
        'ssm_a_re': -0.5 + nrm(ks[16], (no, SSM_GROUPS, SSM_STATE), 0.01),
        'ssm_a_im': math.pi * jnp.arange(SSM_STATE, dtype=f32)[None, None, :]
                    + nrm(ks[17], (no, SSM_GROUPS, SSM_STATE), 0.01),
        'ssm_log_dt': jax.random.uniform(ks[18], (no, SSM_GROUPS), f32,
                                         minval=math.log(DT_MIN), maxval=math.log(DT_MAX)),
        'ssm_b_re': nrm(ks[19], (no, SSM_GROUPS, SSM_STATE, SSM_GROUP), (2 * SSM_GROUP) ** -0.5),
        'ssm_b_im': nrm(ks[20], (no, SSM_GROUPS, SSM_STATE, SSM_GROUP), (2 * SSM_GROUP) ** -0.5),
        'ssm_c_re': nrm(ks[21], (no, SSM_GROUPS, SSM_GROUP, SSM_STATE), 0.5),
        'ssm_c_im': nrm(ks[22], (no, SSM_GROUPS, SSM_GROUP, SSM_STATE), 0.5),
        'ssm_d': nrm(ks[23], (no, SSM_WIDTH), 1.0),
        'ssm_w_glu': nrm(ks[24], (no, SSM_WIDTH, 2 * D_MODEL), SSM_WIDTH ** -0.5),
        'moe_norm': gain(ks[25], (no, D_MODEL)),
        'moe_router': nrm(ks[26], (no, D_MODEL, N_EXPERTS), D_MODEL ** -0.5),
        'moe_w_gate': nrm(ks[27], (no, N_EXPERTS, D_MODEL, D_FF), D_MODEL ** -0.5),
        'moe_w_up': nrm(ks[28], (no, N_EXPERTS, D_MODEL, D_FF), D_MODEL ** -0.5),
        'moe_w_down': nrm(ks[29], (no, N_EXPERTS, D_FF, D_MODEL), D_FF ** -0.5),
    }


def reference(x, att_norm, att_w_in, att_q_latent_norm, att_w_q_up, att_kv_latent_norm, att_w_kv_up,
              att_q_norm, att_k_norm, att_w_out, dffn_norm, dffn_w_gate, dffn_w_up, dffn_w_down,
              ssm_norm, ssm_w_in, ssm_a_re, ssm_a_im, ssm_log_dt, ssm_b_re, ssm_b_im, ssm_c_re, ssm_c_im,
              ssm_d, ssm_w_glu, moe_norm, moe_router, moe_w_gate, moe_w_up, moe_w_down):
    h = x
    for layer in range(DEPTH):
        i = layer // 2
        if layer % 2 == 0:
            h = _attention_layer(h, att_norm[i], att_w_in[i], att_q_latent_norm[i], att_w_q_up[i],
                                 att_kv_latent_norm[i], att_w_kv_up[i], att_q_norm[i], att_k_norm[i],
                                 att_w_out[i])
            h = _dense_swiglu_layer(h, dffn_norm[i], dffn_w_gate[i], dffn_w_up[i], dffn_w_down[i])
        else:
            h = _s5_layer(h, ssm_norm[i], ssm_w_in[i], ssm_a_re[i], ssm_a_im[i], ssm_log_dt[i],
                          ssm_b_re[i], ssm_b_im[i], ssm_c_re[i], ssm_c_im[i], ssm_d[i], ssm_w_glu[i])
            h = _moe_swiglu_layer(h, moe_norm[i], moe_router[i], moe_w_gate[i], moe_w_up[i], moe_w_down[i])
    return h
```

```python
import functools
import math

import jax
import jax.numpy as jnp
from jax import lax
from jax.experimental import pallas as pl
from jax.experimental.pallas import tpu as pltpu

F32 = jnp.float32
BF16 = jnp.bfloat16

EPS = 1e-6
LANES = 128
SB_HEADS = 8
SB_HEAD_DIM = 64
SB_WIDTH = SB_HEADS * SB_HEAD_DIM
MLA_HEADS = 8
MLA_Q_RANK = 256
MLA_KV_RANK = 128
MLA_NOPE_DIM = 64
MLA_ROPE_DIM = 32
MLA_QK_DIM = MLA_NOPE_DIM + MLA_ROPE_DIM
MLA_V_DIM = 64
MLA_WIDTH = MLA_HEADS * MLA_V_DIM
ROPE_THETA = 10000.0
SSM_GROUP = 16
SSM_GROUPS = 32
SSM_STATE = 64
SSM_WIDTH = SSM_GROUP * SSM_GROUPS
SSM_NSTATE = SSM_GROUPS * SSM_STATE
EIG_RE_MAX = -1e-4
N_EXPERTS = 8
VMEM_LIMIT = 56 * 1024 * 1024


def _params(*sem):
    return pltpu.CompilerParams(dimension_semantics=sem, vmem_limit_bytes=VMEM_LIMIT)


def _rms(x, n):
    return x * lax.rsqrt(jnp.sum(x * x, axis=-1, keepdims=True) * (1.0 / n) + EPS)


def _dot(a, b):
    return jnp.dot(a, b, preferred_element_type=F32)


def _dot_nt(a, b):
    return lax.dot_general(a, b, (((1,), (1,)), ((), ())), preferred_element_type=F32)


def _const_spec(shape):
    return pl.BlockSpec(shape, lambda *_: (0,) * len(shape))


IN_SB = 3 * SB_WIDTH
IN_CQ = IN_SB
IN_CKV = IN_CQ + MLA_Q_RANK
IN_KR = IN_CKV + MLA_KV_RANK
IN_KRS = IN_KR + LANES
IN_COLS_PAD = IN_KRS + LANES
MLA_PAD = MLA_HEADS * LANES


def _inproj_kernel(x_ref, g_ref, win_ref, qlg_ref, wq_ref, kvlg_ref, wkv_ref, ctab_ref, stab_ref,
                   qg_ref, qgs_ref, kg_ref, kgs_ref, qkv_ref, qm_ref, km_ref, vm_ref):
    x = x_ref[...]
    xn = _rms(x, x.shape[-1]) * g_ref[...]
    hh = _dot(xn.astype(BF16), win_ref[...])
    qkv_ref[:, :SB_WIDTH] = (hh[:, :SB_WIDTH] * (SB_HEAD_DIM ** -0.5)).astype(BF16)
    qkv_ref[:, SB_WIDTH:] = hh[:, SB_WIDTH:IN_SB].astype(BF16)
    cq = _rms(hh[:, IN_CQ:IN_CKV], MLA_Q_RANK) * qlg_ref[...]
    qf = _dot(cq.astype(BF16), wq_ref[...])
    ckv = _rms(hh[:, IN_CKV:IN_KR], MLA_KV_RANK) * kvlg_ref[...]
    kvf = _dot(ckv.astype(BF16), wkv_ref[...])
    vm_ref[...] = kvf[:, MLA_PAD:].astype(BF16)
    kr = hh[:, IN_KR:IN_KRS]
    krs = hh[:, IN_KRS:IN_COLS_PAD]
    ctab = ctab_ref[...]
    stab = stab_ref[...]
    scale = MLA_QK_DIM ** -0.5
    cq_t = ctab * (qg_ref[...] * scale)
    sq_t = stab * (qgs_ref[...] * scale)
    ck_t = ctab * kg_ref[...]
    sk_t = stab * kgs_ref[...]
    for h in range(MLA_HEADS):
        lo, hi = h * LANES, (h + 1) * LANES
        qh = qf[:, lo:hi]
        qs = qf[:, MLA_PAD + lo:MLA_PAD + hi]
        q_inv = lax.rsqrt(jnp.sum(qh * qh, axis=-1, keepdims=True) * (1.0 / MLA_QK_DIM) + EPS)
        qm_ref[:, lo:hi] = ((qh * cq_t + qs * sq_t) * q_inv).astype(BF16)
        kh = kvf[:, lo:hi] + kr
        k_inv = lax.rsqrt(jnp.sum(kh * kh, axis=-1, keepdims=True) * (1.0 / MLA_QK_DIM) + EPS)
        km_ref[:, lo:hi] = ((kh * ck_t + krs * sk_t) * k_inv).astype(BF16)


def _pad_heads(w, n_heads, width):
    k = w.shape[0]
    w = w.reshape(k, n_heads, width)
    return jnp.pad(w, ((0, 0), (0, 0), (0, LANES - width))).reshape(k, n_heads * LANES)


def _swap_rope(w):
    half = MLA_ROPE_DIM // 2
    return jnp.concatenate([jnp.zeros_like(w[..., :MLA_NOPE_DIM]),
                            w[..., MLA_NOPE_DIM + half:], w[..., MLA_NOPE_DIM:MLA_NOPE_DIM + half]], axis=-1)


def _rope_lane_tables(s_len):
    half = MLA_ROPE_DIM // 2
    inv_freq = ROPE_THETA ** (-jnp.arange(0, MLA_ROPE_DIM, 2, dtype=F32) / MLA_ROPE_DIM)
    ang = jnp.arange(s_len, dtype=F32)[:, None] * inv_freq[None, :]
    cos, sin = jnp.cos(ang), jnp.sin(ang)
    ones = jnp.ones((s_len, MLA_NOPE_DIM), F32)
    zeros_n = jnp.zeros((s_len, MLA_NOPE_DIM), F32)
    zeros_t = jnp.zeros((s_len, LANES - MLA_QK_DIM), F32)
    ctab = jnp.concatenate([ones, cos, cos, zeros_t], axis=-1)
    stab = jnp.concatenate([zeros_n, -sin, sin, zeros_t], axis=-1)
    del half
    return ctab, stab


def _in_projection(x2, s_len, norm_g, w_in, q_lat_g, w_q_up, kv_lat_g, w_kv_up, q_g, k_g, tm):
    t, d = x2.shape
    w_kr = w_in[:, IN_KR - LANES + 0:][:, :0]
    del w_kr
    off_kr = 3 * SB_WIDTH + MLA_Q_RANK + MLA_KV_RANK
    w_rope = w_in[:, off_kr:]
    zeros_n = jnp.zeros((d, MLA_NOPE_DIM), F32)
    zeros_t = jnp.zeros((d, LANES - MLA_QK_DIM), F32)
    half = MLA_ROPE_DIM // 2
    w_in_pad = jnp.concatenate(
        [w_in[:, :off_kr], zeros_n, w_rope, zeros_t,
         zeros_n, w_rope[:, half:], w_rope[:, :half], zeros_t], axis=-1).astype(BF16)
    wq3 = w_q_up.reshape(MLA_Q_RANK, MLA_HEADS, MLA_QK_DIM)
    wq_pad = jnp.concatenate(
        [_pad_heads(w_q_up, MLA_HEADS, MLA_QK_DIM),
         _pad_heads(_swap_rope(wq3).reshape(MLA_Q_RANK, -1), MLA_HEADS, MLA_QK_DIM)], axis=-1).astype(BF16)
    wkv3 = w_kv_up.reshape(MLA_KV_RANK, MLA_HEADS, MLA_NOPE_DIM + MLA_V_DIM)
    wkv_pad = jnp.concatenate(
        [_pad_heads(wkv3[..., :MLA_NOPE_DIM].reshape(MLA_KV_RANK, -1), MLA_HEADS, MLA_NOPE_DIM),
         wkv3[..., MLA_NOPE_DIM:].reshape(MLA_KV_RANK, -1)], axis=-1).astype(BF16)
    ctab, stab = _rope_lane_tables(s_len)
    pad_g = lambda g: jnp.pad(g, (0, LANES - MLA_QK_DIM)).reshape(1, LANES)
    swap_g = lambda g: jnp.pad(jnp.concatenate(
        [jnp.zeros((MLA_NOPE_DIM,), F32), g[MLA_NOPE_DIM + half:], g[MLA_NOPE_DIM:MLA_NOPE_DIM + half]]),
        (0, LANES - MLA_QK_DIM)).reshape(1, LANES)
    n_pos = s_len // tm
    row = lambda i: (i, 0)
    return pl.pallas_call(
        _inproj_kernel,
        grid=(t // tm,),
        in_specs=[
            pl.BlockSpec((tm, d), row),
            _const_spec((1, d)),
            _const_spec((d, IN_COLS_PAD)),
            _const_spec((1, MLA_Q_RANK)),
            _const_spec((MLA_Q_RANK, 2 * MLA_PAD)),
            _const_spec((1, MLA_KV_RANK)),
            _const_spec((MLA_KV_RANK, MLA_PAD + MLA_WIDTH)),
            pl.BlockSpec((tm, LANES), lambda i: (i % n_pos, 0)),
            pl.BlockSpec((tm, LANES), lambda i: (i % n_pos, 0)),
            _const_spec((1, LANES)), _const_spec((1, LANES)),
            _const_spec((1, LANES)), _const_spec((1, LANES)),
        ],
        out_specs=[
            pl.BlockSpec((tm, IN_SB), row),
            pl.BlockSpec((tm, MLA_PAD), row),
            pl.BlockSpec((tm, MLA_PAD), row),
            pl.BlockSpec((tm, MLA_WIDTH), row),
        ],
        out_shape=[
            jax.ShapeDtypeStruct((t, IN_SB), BF16),
            jax.ShapeDtypeStruct((t, MLA_PAD), BF16),
            jax.ShapeDtypeStruct((t, MLA_PAD), BF16),
            jax.ShapeDtypeStruct((t, MLA_WIDTH), BF16),
        ],
        compiler_params=_params("arbitrary"),
        name="in_projection",
    )(x2, norm_g.reshape(1, d), w_in_pad, q_lat_g.reshape(1, -1), wq_pad, kv_lat_g.reshape(1, -1), wkv_pad,
      ctab, stab, pad_g(q_g), swap_g(q_g), pad_g(k_g), swap_g(k_g))


def _sb_kernel(q_ref, k_ref, v_ref, u_ref, o_ref, acc_ref, *, tq):
    qi = pl.program_id(2)
    lane = lax.broadcasted_iota(jnp.int32, (1, LANES), 1)
    row = lax.broadcasted_iota(jnp.int32, (tq, tq), 0)
    col = lax.broadcasted_iota(jnp.int32, (tq, tq), 1)
    earlier = col < row
    q = q_ref[...]

    for j in range(2):
        in_head = (lane >= j * SB_HEAD_DIM) & (lane < (j + 1) * SB_HEAD_DIM)
        qj = jnp.where(in_head, q, jnp.zeros_like(q))

        def chunk(c, carry, diagonal):
            start = pl.multiple_of(c * tq, tq)
            kc = k_ref[pl.ds(start, tq), :]
            vc = v_ref[pl.ds(start, tq), :]
            z = _dot_nt(qj, kc)
            log_fail = -(jnp.maximum(z, 0.0) + jnp.log1p(jnp.exp(-jnp.abs(z))))
            if diagonal:
                log_fail = jnp.where(earlier, log_fail, 0.0)
            hi = log_fail.astype(BF16)
            lo = (log_fail - hi.astype(F32)).astype(BF16)
            log_stick = _dot(hi, u_ref[...]) + _dot(lo, u_ref[...])
            w = jnp.exp(z + log_fail + log_stick + carry)
            if diagonal:
                w = jnp.where(earlier, w, 0.0)
            acc_ref[j] += _dot(w.astype(BF16), vc)
            return carry + log_stick[:, :1] + log_fail[:, :1]

        acc_ref[j] = jnp.zeros((tq, LANES), F32)
        carry = chunk(qi, jnp.zeros((tq, 1), F32), True)
        lax.fori_loop(0, qi, lambda i, c: chunk(qi - 1 - i, c, False), carry)

    o_ref[...] = jnp.where(lane < SB_HEAD_DIM, acc_ref[0], acc_ref[1]).astype(o_ref.dtype)


def _sb_attention(qkv, tq):
    b, s, _ = qkv.shape
    n_pairs = SB_WIDTH // LANES
    tri = (jnp.arange(tq)[:, None] > jnp.arange(tq)[None, :]).astype(BF16)
    return pl.pallas_call(
        functools.partial(_sb_kernel, tq=tq),
        grid=(b, n_pairs, s // tq),
        in_specs=[
            pl.BlockSpec((None, tq, LANES), lambda bi, hp, qi: (bi, qi, hp)),
            pl.BlockSpec((None, s, LANES), lambda bi, hp, qi: (bi, 0, n_pairs + hp)),
            pl.BlockSpec((None, s, LANES), lambda bi, hp, qi: (bi, 0, 2 * n_pairs + hp)),
            _const_spec((tq, tq)),
        ],
        out_specs=pl.BlockSpec((None, tq, LANES), lambda bi, hp, qi: (bi, qi, hp)),
        out_shape=jax.ShapeDtypeStruct((b, s, SB_WIDTH), BF16),
        scratch_shapes=[pltpu.VMEM((2, tq, LANES), F32)],
        compiler_params=_params("arbitrary", "arbitrary", "arbitrary"),
        name="stick_breaking_attention",
    )(qkv, qkv, qkv, tri)


def _mla_kernel(q_ref, k_ref, v_ref, o_ref, acc_ref, *, tq):
    qi = pl.program_id(2)
    lane = lax.broadcasted_iota(jnp.int32, (1, LANES), 1)
    row = lax.broadcasted_iota(jnp.int32, (tq, tq), 0)
    col = lax.broadcasted_iota(jnp.int32, (tq, tq), 1)
    visible = col <= row
    inv_l = []

    for j in range(2):
        qj = q_ref[:, j * LANES:(j + 1) * LANES]

        def chunk(c, carry, diagonal):
            m_prev, l_prev = carry
            start = pl.multiple_of(c * tq, tq)
            kc = k_ref[pl.ds(start, tq), j * LANES:(j + 1) * LANES]
            vc = v_ref[pl.ds(start, tq), :]
            s = _dot_nt(qj, kc)
            if diagonal:
                s = jnp.where(visible, s, -jnp.inf)
            m_new = jnp.maximum(m_prev, jnp.max(s, axis=-1, keepdims=True))
            alpha = jnp.exp(m_prev - m_new)
            p = jnp.exp(s - m_new)
            acc_ref[j] = alpha * acc_ref[j] + _dot(p.astype(BF16), vc)
            return m_new, alpha * l_prev + jnp.sum(p, axis=-1, keepdims=True)

        acc_ref[j] = jnp.zeros((tq, LANES), F32)
        carry = chunk(qi, (jnp.full((tq, 1), -jnp.inf, F32), jnp.zeros((tq, 1), F32)), True)
        _, l_fin = lax.fori_loop(0, qi, lambda i, c: chunk(i, c, False), carry)
        inv_l.append(1.0 / l_fin)

    o_ref[...] = jnp.where(lane < MLA_V_DIM, acc_ref[0] * inv_l[0], acc_ref[1] * inv_l[1]).astype(o_ref.dtype)


def _mla_attention(q, k, v, tq):
    b, s, _ = q.shape
    n_pairs = MLA_WIDTH // LANES
    return pl.pallas_call(
        functools.partial(_mla_kernel, tq=tq),
        grid=(b, n_pairs, s // tq),
        in_specs=[
            pl.BlockSpec((None, tq, 2 * LANES), lambda bi, hp, qi: (bi, qi, hp)),
            pl.BlockSpec((None, s, 2 * LANES), lambda bi, hp, qi: (bi, 0, hp)),
            pl.BlockSpec((None, s, LANES), lambda bi, hp, qi: (bi, 0, hp)),
        ],
        out_specs=pl.BlockSpec((None, tq, LANES), lambda bi, hp, qi: (bi, qi, hp)),
        out_shape=jax.ShapeDtypeStruct((b, s, MLA_WIDTH), BF16),
        scratch_shapes=[pltpu.VMEM((2, tq, LANES), F32)],
        compiler_params=_params("arbitrary", "arbitrary", "arbitrary"),
        name="latent_attention",
    )(q, k, v)


def _outproj_kernel(x_ref, osb_ref, omla_ref, wsb_ref, wmla_ref, g_ref, h_ref, hn_ref):
    h = x_ref[...] + _dot(osb_ref[...], wsb_ref[...]) + _dot(omla_ref[...], wmla_ref[...])
    h_ref[...] = h
    hn_ref[...] = (_rms(h, h.shape[-1]) * g_ref[...]).astype(BF16)


def _out_projection(x2, o_sb, o_mla, w_out, next_norm_g, tm):
    t, d = x2.shape
    row = lambda i: (i, 0)
    w_sb = w_out[:SB_WIDTH].astype(BF16)
    w_mla = w_out[SB_WIDTH:].astype(BF16)
    return pl.pallas_call(
        _outproj_kernel,
        grid=(t // tm,),
        in_specs=[pl.BlockSpec((tm, d), row), pl.BlockSpec((tm, SB_WIDTH), row), pl.BlockSpec((tm, MLA_WIDTH), row),
                  _const_spec((SB_WIDTH, d)), _const_spec((MLA_WIDTH, d)), _const_spec((1, d))],
        out_specs=[pl.BlockSpec((tm, d), row), pl.BlockSpec((tm, d), row)],
        out_shape=[jax.ShapeDtypeStruct((t, d), F32), jax.ShapeDtypeStruct((t, d), BF16)],
        compiler_params=_params("arbitrary"),
        name="out_projection",
    )(x2, o_sb, o_mla, w_sb, w_mla, next_norm_g.reshape(1, d))


def _swiglu(x, wg_ref, wu_ref, wd_ref, f_chunk):
    d_ff = wg_ref.shape[-1]
    acc = None
    for c in range(d_ff // f_chunk):
        cols = slice(c * f_chunk, (c + 1) * f_chunk)
        gate = _dot(x, wg_ref[:, cols])
        up = _dot(x, wu_ref[:, cols])
        act = (gate * jax.nn.sigmoid(gate) * up).astype(BF16)
        part = _dot(act, wd_ref[cols, :])
        acc = part if acc is None else acc + part
    return acc


def _dense_ffn_kernel(h_ref, hn_ref, wg_ref, wu_ref, wd_ref, o_ref, *, f_chunk):
    o_ref[...] = h_ref[...] + _swiglu(hn_ref[...], wg_ref, wu_ref, wd_ref, f_chunk)


def _dense_ffn(h, hn, w_gate, w_up, w_down, tm, f_chunk):
    t, d = h.shape
    d_ff = w_gate.shape[-1]
    row = lambda i: (i, 0)
    resident = lambda shape: pl.BlockSpec(shape, lambda i: (0, 0), pipeline_mode=pl.Buffered(1))
    return pl.pallas_call(
        functools.partial(_dense_ffn_kernel, f_chunk=f_chunk),
        grid=(t // tm,),
        in_specs=[pl.BlockSpec((tm, d), row), pl.BlockSpec((tm, d), row),
                  resident((d, d_ff)), resident((d, d_ff)), resident((d_ff, d))],
        out_specs=pl.BlockSpec((tm, d), row),
        out_shape=jax.ShapeDtypeStruct((t, d), F32),
        compiler_params=_params("arbitrary"),
        name="dense_swiglu",
    )(h, hn, w_gate.astype(BF16), w_up.astype(BF16), w_down.astype(BF16))


def _layer0(x2, b, s, att_norm, att_w_in, q_lat_g, w_q_up, kv_lat_g, w_kv_up, q_g, k_g, w_out,
            dffn_norm, w_gate, w_up, w_down, tm, tq, f_chunk):
    qkv, qm, km, vm = _in_projection(x2, s, att_norm, att_w_in, q_lat_g, w_q_up, kv_lat_g, w_kv_up, q_g, k_g, tm)
    o_sb = _sb_attention(qkv.reshape(b, s, -1), tq).reshape(b * s, -1)
    o_mla = _mla_attention(qm.reshape(b, s, -1), km.reshape(b, s, -1), vm.reshape(b, s, -1), tq).reshape(b * s, -1)
    h1, h1n = _out_projection(x2, o_sb, o_mla, w_out, dffn_norm, tm)
    return _dense_ffn(h1, h1n, w_gate, w_up, w_down, tm, f_chunk)


SSM_SEGS = 8
SSM_SEG_LEN = 64
SSM_TILE = SSM_SEGS * SSM_SEG_LEN
SSM_LANE_CHUNK = 512


def _s5_prep_kernel(are_ref, aim_ref, logdt_ref, bre_ref, bim_ref, cre_ref, cim_ref,
                    wbu_ref, wc_ref, pow_ref):
    n = SSM_NSTATE
    a_re = jnp.minimum(are_ref[...], EIG_RE_MAX)
    a_im = aim_ref[...]
    dt = jnp.exp(logdt_ref[...])
    mag = jnp.exp(a_re * dt)
    lam_re = mag * jnp.cos(a_im * dt)
    lam_im = mag * jnp.sin(a_im * dt)
    den = a_re * a_re + a_im * a_im
    num_re = lam_re - 1.0
    coef_re = (num_re * a_re + lam_im * a_im) / den
    coef_im = (lam_im * a_re - num_re * a_im) / den
    b_re = bre_ref[...]
    b_im = bim_ref[...]
    in_group = (lax.broadcasted_iota(jnp.int32, (SSM_WIDTH, n), 0) // SSM_GROUP
                == lax.broadcasted_iota(jnp.int32, (SSM_WIDTH, n), 1) // SSM_STATE)
    wbu_ref[:, :n] = jnp.where(in_group, coef_re * b_re - coef_im * b_im, 0.0).astype(BF16)
    wbu_ref[:, n:] = jnp.where(in_group, coef_re * b_im + coef_im * b_re, 0.0).astype(BF16)
    in_group_t = (lax.broadcasted_iota(jnp.int32, (n, SSM_WIDTH), 0) // SSM_STATE
                  == lax.broadcasted_iota(jnp.int32, (n, SSM_WIDTH), 1) // SSM_GROUP)
    wc_ref[:n, :] = jnp.where(in_group_t, cre_ref[...], 0.0).astype(BF16)
    wc_ref[n:, :] = jnp.where(in_group_t, -cim_ref[...], 0.0).astype(BF16)
    steps = (lax.broadcasted_iota(jnp.int32, (SSM_SEG_LEN, n), 0) + 1).astype(F32)
    mag_k = jnp.exp(steps * (a_re * dt))
    ang_k = steps * (a_im * dt)
    pow_ref[:, :n] = mag_k * jnp.cos(ang_k)
    pow_ref[:, n:] = mag_k * jnp.sin(ang_k)


def _s5_prepare(a_re, a_im, log_dt, b_re, b_im, c_re, c_im):
    n = SSM_NSTATE
    row = lambda a: a.reshape(1, n)
    b_t = lambda w: jnp.tile(w.transpose(2, 0, 1).reshape(SSM_GROUP, n), (SSM_GROUPS, 1))
    c_t = lambda w: jnp.tile(w.transpose(0, 2, 1).reshape(n, SSM_GROUP), (1, SSM_GROUPS))
    return pl.pallas_call(
        _s5_prep_kernel,
        out_shape=[jax.ShapeDtypeStruct((SSM_WIDTH, 2 * n), BF16),
                   jax.ShapeDtypeStruct((2 * n, SSM_WIDTH), BF16),
                   jax.ShapeDtypeStruct((SSM_SEG_LEN, 2 * n), F32)],
        compiler_params=pltpu.CompilerParams(vmem_limit_bytes=VMEM_LIMIT),
        name="s5_discretize",
    )(row(a_re), row(a_im), row(jnp.repeat(log_dt, SSM_STATE)), b_t(b_re), b_t(b_im), c_t(c_re), c_t(c_im))


def _s5_kernel(h_ref, g_ref, win_ref, wbu_ref, pow_ref, wc_ref, d_ref, wglu_ref, o_ref,
               st_ref, carry_ref, y_ref):
    n = SSM_NSTATE
    d_model = h_ref.shape[-1]

    @pl.when(pl.program_id(1) == 0)
    def _():
        carry_ref[...] = jnp.zeros_like(carry_ref)

    x = h_ref[...]
    xn = _rms(x, d_model) * g_ref[...]
    u = _dot(xn.astype(BF16), win_ref[...])
    u16 = u.astype(BF16)
    nt = n // LANES
    per = SSM_LANE_CHUNK // LANES
    for lc in range(2 * n // SSM_LANE_CHUNK):
        bu = _dot(u16, wbu_ref[:, lc * SSM_LANE_CHUNK:(lc + 1) * SSM_LANE_CHUNK])
        for k in range(per):
            st_ref[lc * per + k] = bu[:, k * LANES:(k + 1) * LANES]

    for lc in range(nt // per):
        tiles = range(lc * per, (lc + 1) * per)
        lam_re = [jnp.broadcast_to(pow_ref[0:1, k * LANES:(k + 1) * LANES], (SSM_SEGS, LANES)) for k in tiles]
        lam_im = [jnp.broadcast_to(pow_ref[0:1, n + k * LANES:n + (k + 1) * LANES], (SSM_SEGS, LANES))
                  for k in tiles]

        def step(i, state):
            rows = pl.ds(i, SSM_SEGS, stride=SSM_SEG_LEN)
            new_state = []
            for idx, k in enumerate(tiles):
                s_re, s_im = state[idx]
                n_re = lam_re[idx] * s_re - lam_im[idx] * s_im + st_ref[k, rows, :]
                n_im = lam_re[idx] * s_im + lam_im[idx] * s_re + st_ref[nt + k, rows, :]
                st_ref[k, rows, :] = n_re
                st_ref[nt + k, rows, :] = n_im
                new_state.append((n_re, n_im))
            return tuple(new_state)

        zero = jnp.zeros((SSM_SEGS, LANES), F32)
        lax.fori_loop(0, SSM_SEG_LEN, step, tuple((zero, zero) for _ in tiles))

    for lc in range(nt // per):
        tiles = range(lc * per, (lc + 1) * per)
        re_cols = slice(lc * SSM_LANE_CHUNK, (lc + 1) * SSM_LANE_CHUNK)
        im_cols = slice(n + lc * SSM_LANE_CHUNK, n + (lc + 1) * SSM_LANE_CHUNK)
        p_re = pow_ref[:, re_cols]
        p_im = pow_ref[:, im_cols]
        seg_pow_re = pow_ref[SSM_SEG_LEN - 1:SSM_SEG_LEN, re_cols]
        seg_pow_im = pow_ref[SSM_SEG_LEN - 1:SSM_SEG_LEN, im_cols]
        c_re = carry_ref[:, re_cols]
        c_im = carry_ref[:, im_cols]
        for j in range(SSM_SEGS):
            rows = pl.ds(j * SSM_SEG_LEN, SSM_SEG_LEN)
            loc_re = jnp.concatenate([st_ref[k, rows, :] for k in tiles], axis=-1)
            loc_im = jnp.concatenate([st_ref[nt + k, rows, :] for k in tiles], axis=-1)
            full_re = loc_re + (p_re * c_re - p_im * c_im)
            full_im = loc_im + (p_re * c_im + p_im * c_re)
            part = (_dot(full_re.astype(BF16), wc_ref[re_cols, :])
                    + _dot(full_im.astype(BF16), wc_ref[im_cols, :]))
            if lc == 0:
                y_ref[rows, :] = part
            else:
                y_ref[rows, :] += part
            end_re = loc_re[SSM_SEG_LEN - 1:, :]
            end_im = loc_im[SSM_SEG_LEN - 1:, :]
            c_re, c_im = (end_re + (seg_pow_re * c_re - seg_pow_im * c_im),
                          end_im + (seg_pow_re * c_im + seg_pow_im * c_re))
        carry_ref[:, re_cols] = c_re
        carry_ref[:, im_cols] = c_im

    y = jax.nn.gelu(y_ref[...] + d_ref[...] * u)
    z = _dot(y.astype(BF16), wglu_ref[...])
    o_ref[...] = x + z[:, :d_model] * jax.nn.sigmoid(z[:, d_model:])


def _s5_layer(h, b, s, norm_g, w_in, a_re, a_im, log_dt, b_re, b_im, c_re, c_im, d_skip, w_glu):
    t, d = h.shape
    n = SSM_NSTATE
    w_bu, w_c, powers = _s5_prepare(a_re, a_im, log_dt, b_re, b_im, c_re, c_im)
    tiles = s // SSM_TILE
    resident = lambda shape: pl.BlockSpec(shape, lambda bi, ti: (0, 0), pipeline_mode=pl.Buffered(1))
    row = lambda bi, ti: (bi * tiles + ti, 0)
    return pl.pallas_call(
        _s5_kernel,
        grid=(b, tiles),
        in_specs=[pl.BlockSpec((SSM_TILE, d), row), resident((1, d)), resident((d, SSM_WIDTH)),
                  resident((SSM_WIDTH, 2 * n)), resident((SSM_SEG_LEN, 2 * n)), resident((2 * n, SSM_WIDTH)),
                  resident((1, SSM_WIDTH)), resident((SSM_WIDTH, 2 * d))],
        out_specs=pl.BlockSpec((SSM_TILE, d), row),
        out_shape=jax.ShapeDtypeStruct((t, d), F32),
        scratch_shapes=[pltpu.VMEM((2 * n // LANES, SSM_TILE, LANES), F32), pltpu.VMEM((1, 2 * n), F32),
                        pltpu.VMEM((SSM_TILE, SSM_WIDTH), F32)],
        compiler_params=_params("arbitrary", "arbitrary"),
        name="s5_mixer",
    )(h, norm_g.reshape(1, d), w_in.astype(BF16), w_bu, powers, w_c, d_skip.reshape(1, -1), w_glu.astype(BF16))


MOE_BLOCK = 256
META_E1, META_E2, META_G1, META_G2, META_R1, META_R2 = range(6)


def _router_kernel(h_ref, g_ref, wr_ref, tri_ref, xn_ref, meta_ref, counts_ref, run_ref):
    @pl.when(pl.program_id(0) == 0)
    def _():
        run_ref[...] = jnp.zeros_like(run_ref)

    x = h_ref[...]
    xn = _rms(x, x.shape[-1]) * g_ref[...]
    xn_ref[...] = xn
    logits = jnp.dot(xn, wr_ref[...], preferred_element_type=F32, precision=lax.Precision.HIGHEST)
    tm = logits.shape[0]
    e = lax.broadcasted_iota(jnp.int32, (tm, N_EXPERTS), 1)
    v1 = jnp.max(logits, axis=-1, keepdims=True)
    i1 = jnp.min(jnp.where(logits == v1, e, N_EXPERTS), axis=-1, keepdims=True)
    rest = jnp.where(e == i1, -jnp.inf, logits)
    v2 = jnp.max(rest, axis=-1, keepdims=True)
    i2 = jnp.min(jnp.where(rest == v2, e, N_EXPERTS), axis=-1, keepdims=True)
    ex = jnp.exp(v2 - v1)
    g1 = 1.0 / (1.0 + ex)
    g2 = ex / (1.0 + ex)
    oh1 = (e == i1).astype(F32)
    oh2 = (e == i2).astype(F32)
    both = oh1 + oh2
    before = _dot(tri_ref[...], both.astype(BF16)) + run_ref[...]
    r1 = jnp.sum(oh1 * before, axis=-1, keepdims=True)
    r2 = jnp.sum(oh2 * before, axis=-1, keepdims=True)
    run_ref[...] += jnp.sum(both, axis=0, keepdims=True)
    counts_ref[...] = run_ref[...]
    meta = jnp.zeros((tm, N_EXPERTS), F32)
    for lane, val in ((META_E1, i1.astype(F32)), (META_E2, i2.astype(F32)), (META_G1, g1), (META_G2, g2),
                      (META_R1, r1), (META_R2, r2)):
        meta = jnp.where(e == lane, val, meta)
    meta_ref[...] = meta


def _router(h, norm_g, w_router, tm):
    t, d = h.shape
    row = lambda i: (i, 0)
    tri = (jnp.arange(tm)[:, None] > jnp.arange(tm)[None, :]).astype(BF16)
    return pl.pallas_call(
        _router_kernel,
        grid=(t // tm,),
        in_specs=[pl.BlockSpec((tm, d), row), _const_spec((1, d)), _const_spec((d, N_EXPERTS)),
                  _const_spec((tm, tm))],
        out_specs=[pl.BlockSpec((tm, d), row), pl.BlockSpec((tm, N_EXPERTS), row), _const_spec((1, N_EXPERTS))],
        out_shape=[jax.ShapeDtypeStruct((t, d), F32), jax.ShapeDtypeStruct((t, N_EXPERTS), F32),
                   jax.ShapeDtypeStruct((1, N_EXPERTS), F32)],
        scratch_shapes=[pltpu.VMEM((1, N_EXPERTS), F32)],
        compiler_params=_params("arbitrary"),
        name="moe_router",
    )(h, norm_g.reshape(1, d), w_router, tri)


def _row_copy(src_ref, src_row, dst_ref, dst_row, sem):
    return pltpu.make_async_copy(src_ref.at[pl.ds(src_row, 1)], dst_ref.at[pl.ds(dst_row, 1)], sem)


def _dispatch_kernel(d1_ref, d2_ref, xn_ref, xs_in_ref, xs_ref, sem, *, tm):
    del xs_in_ref
    base = pl.program_id(0) * tm

    def issue(r, _):
        tok = base + r
        _row_copy(xn_ref, tok, xs_ref, d1_ref[tok], sem).start()
        _row_copy(xn_ref, tok, xs_ref, d2_ref[tok], sem).start()
        return 0

    lax.fori_loop(0, tm, issue, 0)

    def drain(r, _):
        _row_copy(xn_ref, 0, xs_ref, 0, sem).wait()
        _row_copy(xn_ref, 0, xs_ref, 0, sem).wait()
        return 0

    lax.fori_loop(0, tm, drain, 0)


def _dispatch(xn, dest1, dest2, n_slots, tm):
    t, d = xn.shape
    any_spec = pl.BlockSpec(memory_space=pl.ANY)
    return pl.pallas_call(
        functools.partial(_dispatch_kernel, tm=tm),
        grid_spec=pltpu.PrefetchScalarGridSpec(
            num_scalar_prefetch=2, grid=(t // tm,),
            in_specs=[any_spec, any_spec], out_specs=any_spec,
            scratch_shapes=[pltpu.SemaphoreType.DMA(())]),
        out_shape=jax.ShapeDtypeStruct((n_slots, d), F32),
        input_output_aliases={3: 0},
        compiler_params=_params("arbitrary"),
        name="moe_dispatch",
    )(dest1, dest2, xn, jnp.zeros((n_slots, d), F32))


def _expert_ffn_kernel(be_ref, bv_ref, xs_ref, wg_ref, wu_ref, wd_ref, ys_ref, *, f_chunk):
    del be_ref
    valid = bv_ref[pl.program_id(0)] != 0

    @pl.when(valid)
    def _():
        ys_ref[...] = _swiglu(xs_ref[...].astype(BF16), wg_ref, wu_ref, wd_ref, f_chunk)

    @pl.when(jnp.logical_not(valid))
    def _():
        ys_ref[...] = jnp.zeros_like(ys_ref)


def _expert_ffn(xs, blk_expert, blk_valid, w_gate, w_up, w_down, f_chunk):
    n_slots, d = xs.shape
    d_ff = w_gate.shape[-1]
    row = lambda i, be, bv: (i, 0)
    return pl.pallas_call(
        functools.partial(_expert_ffn_kernel, f_chunk=f_chunk),
        grid_spec=pltpu.PrefetchScalarGridSpec(
            num_scalar_prefetch=2, grid=(n_slots // MOE_BLOCK,),
            in_specs=[pl.BlockSpec((MOE_BLOCK, d), row),
                      pl.BlockSpec((None, d, d_ff), lambda i, be, bv: (be[i], 0, 0)),
                      pl.BlockSpec((None, d, d_ff), lambda i, be, bv: (be[i], 0, 0)),
                      pl.BlockSpec((None, d_ff, d), lambda i, be, bv: (be[i], 0, 0))],
            out_specs=pl.BlockSpec((MOE_BLOCK, d), row)),
        out_shape=jax.ShapeDtypeStruct((n_slots, d), F32),
        compiler_params=_params("arbitrary"),
        name="expert_swiglu",
    )(blk_expert, blk_valid, xs, w_gate.astype(BF16), w_up.astype(BF16), w_down.astype(BF16))


def _combine_kernel(d1_ref, d2_ref, h_ref, meta_ref, ys_ref, o_ref, a_ref, b_ref, sem, *, tm):
    base = pl.program_id(0) * tm

    def issue(r, _):
        tok = base + r
        _row_copy(ys_ref, d1_ref[tok], a_ref, r, sem).start()
        _row_copy(ys_ref, d2_ref[tok], b_ref, r, sem).start()
        return 0

    lax.fori_loop(0, tm, issue, 0)

    def drain(r, _):
        _row_copy(ys_ref, 0, a_ref, 0, sem).wait()
        _row_copy(ys_ref, 0, b_ref, 0, sem).wait()
        return 0

    lax.fori_loop(0, tm, drain, 0)
    meta = meta_ref[...]
    g1 = meta[:, META_G1:META_G1 + 1]
    g2 = meta[:, META_G2:META_G2 + 1]
    o_ref[...] = h_ref[...] + (g1 * a_ref[...] + g2 * b_ref[...])


def _combine(h, meta, ys, dest1, dest2, tm):
    t, d = h.shape
    row = lambda i, d1, d2: (i, 0)
    return pl.pallas_call(
        functools.partial(_combine_kernel, tm=tm),
        grid_spec=pltpu.PrefetchScalarGridSpec(
            num_scalar_prefetch=2, grid=(t // tm,),
            in_specs=[pl.BlockSpec((tm, d), row), pl.BlockSpec((tm, N_EXPERTS), row),
                      pl.BlockSpec(memory_space=pl.ANY)],
            out_specs=pl.BlockSpec((tm, d), row),
            scratch_shapes=[pltpu.VMEM((tm, d), F32), pltpu.VMEM((tm, d), F32), pltpu.SemaphoreType.DMA(())]),
        out_shape=jax.ShapeDtypeStruct((t, d), F32),
        compiler_params=_params("arbitrary"),
        name="moe_combine",
    )(dest1, dest2, h, meta, ys)


def _moe_layer(h, norm_g, w_router, w_gate, w_up, w_down, tm, f_chunk):
    t, d = h.shape
    xn, meta, counts = _router(h, norm_g, w_router, tm)
    counts = counts.reshape(N_EXPERTS).astype(jnp.int32)
    padded = (counts + MOE_BLOCK - 1) // MOE_BLOCK * MOE_BLOCK
    pad_end = jnp.cumsum(padded)
    pad_start = pad_end - padded
    e1 = meta[:, META_E1].astype(jnp.int32)
    e2 = meta[:, META_E2].astype(jnp.int32)
    dest1 = pad_start[e1] + meta[:, META_R1].astype(jnp.int32)
    dest2 = pad_start[e2] + meta[:, META_R2].astype(jnp.int32)
    n_slots = 2 * t + N_EXPERTS * MOE_BLOCK
    blk_start = jnp.arange(n_slots // MOE_BLOCK, dtype=jnp.int32) * MOE_BLOCK
    blk_expert = jnp.minimum(jnp.sum(blk_start[:, None] >= pad_end[None, :], axis=1), N_EXPERTS - 1).astype(jnp.int32)
    blk_valid = (blk_start < pad_end[-1]).astype(jnp.int32)
    xs = _dispatch(xn, dest1, dest2, n_slots, tm)
    ys = _expert_ffn(xs, blk_expert, blk_valid, w_gate, w_up, w_down, f_chunk)
    return _combine(h, meta, ys, dest1, dest2, tm)


def kernel(x, att_norm, att_w_in, att_q_latent_norm, att_w_q_up, att_kv_latent_norm, att_w_kv_up, att_q_norm, att_k_norm, att_w_out, dffn_norm, dffn_w_gate, dffn_w_up, dffn_w_down, ssm_norm, ssm_w_in, ssm_a_re, ssm_a_im, ssm_log_dt, ssm_b_re, ssm_b_im, ssm_c_re, ssm_c_im, ssm_d, ssm_w_glu, moe_norm, moe_router, moe_w_gate, moe_w_up, moe_w_down):
    b, s, d = x.shape
    x2 = x.reshape(b * s, d)
    h = _layer0(x2, b, s, att_norm[0], att_w_in[0], att_q_latent_norm[0], att_w_q_up[0], att_kv_latent_norm[0],
                att_w_kv_up[0], att_q_norm[0], att_k_norm[0], att_w_out[0], dffn_norm[0], dffn_w_gate[0],
                dffn_w_up[0], dffn_w_down[0], tm=256, tq=256, f_chunk=512)
    h = _s5_layer(h, b, s, ssm_norm[0], ssm_w_in[0], ssm_a_re[0], ssm_a_im[0], ssm_log_dt[0], ssm_b_re[0],
                  ssm_b_im[0], ssm_c_re[0], ssm_c_im[0], ssm_d[0], ssm_w_glu[0])
    h = _moe_layer(h, moe_norm[0], moe_router[0], moe_w_gate[0], moe_w_up[0], moe_w_down[0], tm=256, f_chunk=512)
    return h.reshape(b, s, d)
```

```python
import functools
import math

import jax
import jax.numpy as jnp
from jax import lax
from jax.experimental import pallas as pl
from jax.experimental.pallas import tpu as pltpu

F32 = jnp.float32
BF16 = jnp.bfloat16

EPS = 1e-6
LANES = 128
SB_HEADS = 8
SB_HEAD_DIM = 64
SB_WIDTH = SB_HEADS * SB_HEAD_DIM
MLA_HEADS = 8
MLA_Q_RANK = 256
MLA_KV_RANK = 128
MLA_NOPE_DIM = 64
MLA_ROPE_DIM = 32
MLA_QK_DIM = MLA_NOPE_DIM + MLA_ROPE_DIM
MLA_V_DIM = 64
MLA_WIDTH = MLA_HEADS * MLA_V_DIM
ROPE_THETA = 10000.0
SSM_GROUP = 16
SSM_GROUPS = 32
SSM_STATE = 64
SSM_WIDTH = SSM_GROUP * SSM_GROUPS
SSM_NSTATE = SSM_GROUPS * SSM_STATE
EIG_RE_MAX = -1e-4
N_EXPERTS = 8
VMEM_LIMIT = 56 * 1024 * 1024


def _params(*sem):
    return pltpu.CompilerParams(dimension_semantics=sem, vmem_limit_bytes=VMEM_LIMIT)


def _rms(x, n):
    return x * lax.rsqrt(jnp.sum(x * x, axis=-1, keepdims=True) * (1.0 / n) + EPS)


def _dot(a, b):
    return jnp.dot(a, b, preferred_element_type=F32)


def _dot_nt(a, b):
    return lax.dot_general(a, b, (((1,), (1,)), ((), ())), preferred_element_type=F32)


def _const_spec(shape):
    return pl.BlockSpec(shape, lambda *_: (0,) * len(shape))


IN_SB = 3 * SB_WIDTH
IN_CQ = IN_SB
IN_CKV = IN_CQ + MLA_Q_RANK
IN_KR = IN_CKV + MLA_KV_RANK
IN_KRS = IN_KR + LANES
IN_COLS_PAD = IN_KRS + LANES
MLA_PAD = MLA_HEADS * LANES


def _inproj_kernel(x_ref, g_ref, win_ref, qlg_ref, wq_ref, kvlg_ref, wkv_ref, ctab_ref, stab_ref,
                   qg_ref, qgs_ref, kg_ref, kgs_ref, qkv_ref, qm_ref, km_ref, vm_ref):
    x = x_ref[...]
    xn = _rms(x, x.shape[-1]) * g_ref[...]
    hh = _dot(xn.astype(BF16), win_ref[...])
    qkv_ref[:, :SB_WIDTH] = (hh[:, :SB_WIDTH] * (SB_HEAD_DIM ** -0.5)).astype(BF16)
    qkv_ref[:, SB_WIDTH:] = hh[:, SB_WIDTH:IN_SB].astype(BF16)
    cq = _rms(hh[:, IN_CQ:IN_CKV], MLA_Q_RANK) * qlg_ref[...]
    qf = _dot(cq.astype(BF16), wq_ref[...])
    ckv = _rms(hh[:, IN_CKV:IN_KR], MLA_KV_RANK) * kvlg_ref[...]
    kvf = _dot(ckv.astype(BF16), wkv_ref[...])
    vm_ref[...] = kvf[:, MLA_PAD:].astype(BF16)
    kr = hh[:, IN_KR:IN_KRS]
    krs = hh[:, IN_KRS:IN_COLS_PAD]
    ctab = ctab_ref[...]
    stab = stab_ref[...]
    scale = MLA_QK_DIM ** -0.5
    cq_t = ctab * (qg_ref[...] * scale)
    sq_t = stab * (qgs_ref[...] * scale)
    ck_t = ctab * kg_ref[...]
    sk_t = stab * kgs_ref[...]
    for h in range(MLA_HEADS):
        lo, hi = h * LANES, (h + 1) * LANES
        qh = qf[:, lo:hi]
        qs = qf[:, MLA_PAD + lo:MLA_PAD + hi]
        q_inv = lax.rsqrt(jnp.sum(qh * qh, axis=-1, keepdims=True) * (1.0 / MLA_QK_DIM) + EPS)
        qm_ref[:, lo:hi] = ((qh * cq_t + qs * sq_t) * q_inv).astype(BF16)
        kh = kvf[:, lo:hi] + kr
        k_inv = lax.rsqrt(jnp.sum(kh * kh, axis=-1, keepdims=True) * (1.0 / MLA_QK_DIM) + EPS)
        km_ref[:, lo:hi] = ((kh * ck_t + krs * sk_t) * k_inv).astype(BF16)


def _pad_heads(w, n_heads, width):
    k = w.shape[0]
    w = w.reshape(k, n_heads, width)
    return jnp.pad(w, ((0, 0), (0, 0), (0, LANES - width))).reshape(k, n_heads * LANES)


def _swap_rope(w):
    half = MLA_ROPE_DIM // 2
    return jnp.concatenate([jnp.zeros_like(w[..., :MLA_NOPE_DIM]),
                            w[..., MLA_NOPE_DIM + half:], w[..., MLA_NOPE_DIM:MLA_NOPE_DIM + half]], axis=-1)


def _rope_lane_tables(s_len):
    half = MLA_ROPE_DIM // 2
    inv_freq = ROPE_THETA ** (-jnp.arange(0, MLA_ROPE_DIM, 2, dtype=F32) / MLA_ROPE_DIM)
    ang = jnp.arange(s_len, dtype=F32)[:, None] * inv_freq[None, :]
    cos, sin = jnp.cos(ang), jnp.sin(ang)
    ones = jnp.ones((s_len, MLA_NOPE_DIM), F32)
    zeros_n = jnp.zeros((s_len, MLA_NOPE_DIM), F32)
    zeros_t = jnp.zeros((s_len, LANES - MLA_QK_DIM), F32)
    ctab = jnp.concatenate([ones, cos, cos, zeros_t], axis=-1)
    stab = jnp.concatenate([zeros_n, -sin, sin, zeros_t], axis=-1)
    del half
    return ctab, stab


def _in_projection(x2, s_len, norm_g, w_in, q_lat_g, w_q_up, kv_lat_g, w_kv_up, q_g, k_g, tm):
    t, d = x2.shape
    w_kr = w_in[:, IN_KR - LANES + 0:][:, :0]
    del w_kr
    off_kr = 3 * SB_WIDTH + MLA_Q_RANK + MLA_KV_RANK
    w_rope = w_in[:, off_kr:]
    zeros_n = jnp.zeros((d, MLA_NOPE_DIM), F32)
    zeros_t = jnp.zeros((d, LANES - MLA_QK_DIM), F32)
    half = MLA_ROPE_DIM // 2
    w_in_pad = jnp.concatenate(
        [w_in[:, :off_kr], zeros_n, w_rope, zeros_t,
         zeros_n, w_rope[:, half:], w_rope[:, :half], zeros_t], axis=-1).astype(BF16)
    wq3 = w_q_up.reshape(MLA_Q_RANK, MLA_HEADS, MLA_QK_DIM)
    wq_pad = jnp.concatenate(
        [_pad_heads(w_q_up, MLA_HEADS, MLA_QK_DIM),
         _pad_heads(_swap_rope(wq3).reshape(MLA_Q_RANK, -1), MLA_HEADS, MLA_QK_DIM)], axis=-1).astype(BF16)
    wkv3 = w_kv_up.reshape(MLA_KV_RANK, MLA_HEADS, MLA_NOPE_DIM + MLA_V_DIM)
    wkv_pad = jnp.concatenate(
        [_pad_heads(wkv3[..., :MLA_NOPE_DIM].reshape(MLA_KV_RANK, -1), MLA_HEADS, MLA_NOPE_DIM),
         wkv3[..., MLA_NOPE_DIM:].reshape(MLA_KV_RANK, -1)], axis=-1).astype(BF16)
    ctab, stab = _rope_lane_tables(s_len)
    pad_g = lambda g: jnp.pad(g, (0, LANES - MLA_QK_DIM)).reshape(1, LANES)
    swap_g = lambda g: jnp.pad(jnp.concatenate(
        [jnp.zeros((MLA_NOPE_DIM,), F32), g[MLA_NOPE_DIM + half:], g[MLA_NOPE_DIM:MLA_NOPE_DIM + half]]),
        (0, LANES - MLA_QK_DIM)).reshape(1, LANES)
    n_pos = s_len // tm
    row = lambda i: (i, 0)
    return pl.pallas_call(
        _inproj_kernel,
        grid=(t // tm,),
        in_specs=[
            pl.BlockSpec((tm, d), row),
            _const_spec((1, d)),
            _const_spec((d, IN_COLS_PAD)),
            _const_spec((1, MLA_Q_RANK)),
            _const_spec((MLA_Q_RANK, 2 * MLA_PAD)),
            _const_spec((1, MLA_KV_RANK)),
            _const_spec((MLA_KV_RANK, MLA_PAD + MLA_WIDTH)),
            pl.BlockSpec((tm, LANES), lambda i: (i % n_pos, 0)),
            pl.BlockSpec((tm, LANES), lambda i: (i % n_pos, 0)),
            _const_spec((1, LANES)), _const_spec((1, LANES)),
            _const_spec((1, LANES)), _const_spec((1, LANES)),
        ],
        out_specs=[
            pl.BlockSpec((tm, IN_SB), row),
            pl.BlockSpec((tm, MLA_PAD), row),
            pl.BlockSpec((tm, MLA_PAD), row),
            pl.BlockSpec((tm, MLA_WIDTH), row),
        ],
        out_shape=[
            jax.ShapeDtypeStruct((t, IN_SB), BF16),
            jax.ShapeDtypeStruct((t, MLA_PAD), BF16),
            jax.ShapeDtypeStruct((t, MLA_PAD), BF16),
            jax.ShapeDtypeStruct((t, MLA_WIDTH), BF16),
        ],
        compiler_params=_params("arbitrary"),
        name="in_projection",
    )(x2, norm_g.reshape(1, d), w_in_pad, q_lat_g.reshape(1, -1), wq_pad, kv_lat_g.reshape(1, -1), wkv_pad,
      ctab, stab, pad_g(q_g), swap_g(q_g), pad_g(k_g), swap_g(k_g))


def _sb_kernel(q_ref, k_ref, v_ref, u_ref, o_ref, acc_ref, carry_ref, *, tq, tk):
    qi = pl.program_id(2)
    lane = lax.broadcasted_iota(jnp.int32, (1, LANES), 1)
    row = lax.broadcasted_iota(jnp.int32, (tq, tk), 0)
    col = lax.broadcasted_iota(jnp.int32, (tq, tk), 1)
    q = q_ref[...]
    q_heads = [jnp.where((lane >= j * SB_HEAD_DIM) & (lane < (j + 1) * SB_HEAD_DIM), q, jnp.zeros_like(q))
               for j in range(2)]
    acc_ref[...] = jnp.zeros_like(acc_ref)
    carry_ref[...] = jnp.zeros_like(carry_ref)

    def sweep(start, diagonal):
        start = pl.multiple_of(start, tk)
        kc = k_ref[pl.ds(start, tk), :]
        vc = v_ref[pl.ds(start, tk), :]
        if diagonal:
            earlier = (start + col) < (qi * tq + row)
        for j in range(2):
            z = _dot_nt(q_heads[j], kc)
            log_fail = -(jnp.maximum(z, 0.0) + jnp.log1p(jnp.exp(-jnp.abs(z))))
            if diagonal:
                log_fail = jnp.where(earlier, log_fail, 0.0)
            hi = log_fail.astype(BF16)
            lo = (log_fail - hi.astype(F32)).astype(BF16)
            log_stick = _dot(hi, u_ref[...]) + _dot(lo, u_ref[...])
            carry = carry_ref[j]
            w = jnp.exp(z + log_fail + log_stick + jnp.tile(carry, (1, tk // LANES)))
            if diagonal:
                w = jnp.where(earlier, w, 0.0)
            acc_ref[j] += _dot(w.astype(BF16), vc)
            carry_ref[j] = carry + (log_stick[:, :1] + log_fail[:, :1])

    subs = tuple(reversed(range(tq // tk)))
    for sub in subs:
        sweep(qi * tq + sub * tk, True)

    def earlier_block(i, _):
        for sub in subs:
            sweep((qi - 1 - i) * tq + sub * tk, False)
        return 0

    lax.fori_loop(0, qi, earlier_block, 0)
    o_ref[...] = jnp.where(lane < SB_HEAD_DIM, acc_ref[0], acc_ref[1]).astype(o_ref.dtype)


def _sb_attention(qkv, tq, tk):
    b, s, _ = qkv.shape
    n_pairs = SB_WIDTH // LANES
    tri = (jnp.arange(tk)[:, None] > jnp.arange(tk)[None, :]).astype(BF16)
    return pl.pallas_call(
        functools.partial(_sb_kernel, tq=tq, tk=tk),
        grid=(b, n_pairs, s // tq),
        in_specs=[
            pl.BlockSpec((None, tq, LANES), lambda bi, hp, qi: (bi, qi, hp)),
            pl.BlockSpec((None, s, LANES), lambda bi, hp, qi: (bi, 0, n_pairs + hp)),
            pl.BlockSpec((None, s, LANES), lambda bi, hp, qi: (bi, 0, 2 * n_pairs + hp)),
            _const_spec((tk, tk)),
        ],
        out_specs=pl.BlockSpec((None, tq, LANES), lambda bi, hp, qi: (bi, qi, hp)),
        out_shape=jax.ShapeDtypeStruct((b, s, SB_WIDTH), BF16),
        scratch_shapes=[pltpu.VMEM((2, tq, LANES), F32), pltpu.VMEM((2, tq, LANES), F32)],
        compiler_params=_params("arbitrary", "arbitrary", "arbitrary"),
        name="stick_breaking_attention",
    )(qkv, qkv, qkv, tri)


def _mla_kernel(q_ref, k_ref, v_ref, o_ref, acc_ref, m_ref, l_ref, *, tq):
    qi = pl.program_id(2)
    lane = lax.broadcasted_iota(jnp.int32, (1, LANES), 1)
    acc_ref[...] = jnp.zeros_like(acc_ref)
    l_ref[...] = jnp.zeros_like(l_ref)
    m_ref[...] = jnp.full(m_ref.shape, -jnp.inf, F32)

    def sweep(c, diagonal):
        start = pl.multiple_of(c * tq, tq)
        vc = v_ref[pl.ds(start, tq), :]
        for j in range(2):
            s = _dot_nt(q_ref[:, j * LANES:(j + 1) * LANES], k_ref[pl.ds(start, tq), j * LANES:(j + 1) * LANES])
            if diagonal:
                row = lax.broadcasted_iota(jnp.int32, (tq, tq), 0)
                col = lax.broadcasted_iota(jnp.int32, (tq, tq), 1)
                s = jnp.where(col <= row, s, -jnp.inf)
            m_prev = m_ref[j]
            m_new = jnp.maximum(m_prev, jnp.max(s, axis=-1, keepdims=True))
            alpha = jnp.exp(m_prev - m_new)
            p = jnp.exp(s - jnp.tile(m_new, (1, tq // LANES)))
            l_ref[j] = alpha * l_ref[j] + jnp.sum(p, axis=-1, keepdims=True)
            acc_ref[j] = alpha * acc_ref[j] + _dot(p.astype(BF16), vc)
            m_ref[j] = m_new

    def earlier_block(c, _):
        sweep(c, False)
        return 0

    lax.fori_loop(0, qi, earlier_block, 0)
    sweep(qi, True)
    o_ref[...] = jnp.where(lane < MLA_V_DIM, acc_ref[0] / l_ref[0], acc_ref[1] / l_ref[1]).astype(o_ref.dtype)


def _mla_attention(q, k, v, tq):
    b, s, _ = q.shape
    n_pairs = MLA_WIDTH // LANES
    return pl.pallas_call(
        functools.partial(_mla_kernel, tq=tq),
        grid=(b, n_pairs, s // tq),
        in_specs=[
            pl.BlockSpec((None, tq, 2 * LANES), lambda bi, hp, qi: (bi, qi, hp)),
            pl.BlockSpec((None, s, 2 * LANES), lambda bi, hp, qi: (bi, 0, hp)),
            pl.BlockSpec((None, s, LANES), lambda bi, hp, qi: (bi, 0, hp)),
        ],
        out_specs=pl.BlockSpec((None, tq, LANES), lambda bi, hp, qi: (bi, qi, hp)),
        out_shape=jax.ShapeDtypeStruct((b, s, MLA_WIDTH), BF16),
        scratch_shapes=[pltpu.VMEM((2, tq, LANES), F32)] * 3,
        compiler_params=_params("arbitrary", "arbitrary", "arbitrary"),
        name="latent_attention",
    )(q, k, v)


def _outproj_kernel(x_ref, osb_ref, omla_ref, wsb_ref, wmla_ref, g_ref, h_ref, hn_ref):
    h = x_ref[...] + _dot(osb_ref[...], wsb_ref[...]) + _dot(omla_ref[...], wmla_ref[...])
    h_ref[...] = h
    hn_ref[...] = (_rms(h, h.shape[-1]) * g_ref[...]).astype(BF16)


def _out_projection(x2, o_sb, o_mla, w_out, next_norm_g, tm):
    t, d = x2.shape
    row = lambda i: (i, 0)
    w_sb = w_out[:SB_WIDTH].astype(BF16)
    w_mla = w_out[SB_WIDTH:].astype(BF16)
    return pl.pallas_call(
        _outproj_kernel,
        grid=(t // tm,),
        in_specs=[pl.BlockSpec((tm, d), row), pl.BlockSpec((tm, SB_WIDTH), row), pl.BlockSpec((tm, MLA_WIDTH), row),
                  _const_spec((SB_WIDTH, d)), _const_spec((MLA_WIDTH, d)), _const_spec((1, d))],
        out_specs=[pl.BlockSpec((tm, d), row), pl.BlockSpec((tm, d), row)],
        out_shape=[jax.ShapeDtypeStruct((t, d), F32), jax.ShapeDtypeStruct((t, d), BF16)],
        compiler_params=_params("arbitrary"),
        name="out_projection",
    )(x2, o_sb, o_mla, w_sb, w_mla, next_norm_g.reshape(1, d))


def _swiglu(x, wg_ref, wu_ref, wd_ref, f_chunk):
    d_ff = wg_ref.shape[-1]
    acc = None
    for c in range(d_ff // f_chunk):
        cols = slice(c * f_chunk, (c + 1) * f_chunk)
        gate = _dot(x, wg_ref[:, cols])
        up = _dot(x, wu_ref[:, cols])
        act = (gate * jax.nn.sigmoid(gate) * up).astype(BF16)
        part = _dot(act, wd_ref[cols, :])
        acc = part if acc is None else acc + part
    return acc


def _dense_ffn_kernel(h_ref, hn_ref, wg_ref, wu_ref, wd_ref, o_ref, *, f_chunk):
    o_ref[...] = h_ref[...] + _swiglu(hn_ref[...], wg_ref, wu_ref, wd_ref, f_chunk)


def _dense_ffn(h, hn, w_gate, w_up, w_down, tm, f_chunk):
    t, d = h.shape
    d_ff = w_gate.shape[-1]
    row = lambda i: (i, 0)
    resident = lambda shape: pl.BlockSpec(shape, lambda i: (0, 0), pipeline_mode=pl.Buffered(1))
    return pl.pallas_call(
        functools.partial(_dense_ffn_kernel, f_chunk=f_chunk),
        grid=(t // tm,),
        in_specs=[pl.BlockSpec((tm, d), row), pl.BlockSpec((tm, d), row),
                  resident((d, d_ff)), resident((d, d_ff)), resident((d_ff, d))],
        out_specs=pl.BlockSpec((tm, d), row),
        out_shape=jax.ShapeDtypeStruct((t, d), F32),
        compiler_params=_params("arbitrary"),
        name="dense_swiglu",
    )(h, hn, w_gate.astype(BF16), w_up.astype(BF16), w_down.astype(BF16))


def _layer0(x2, b, s, att_norm, att_w_in, q_lat_g, w_q_up, kv_lat_g, w_kv_up, q_g, k_g, w_out,
            dffn_norm, w_gate, w_up, w_down, tm, tq, tk, f_chunk):
    qkv, qm, km, vm = _in_projection(x2, s, att_norm, att_w_in, q_lat_g, w_q_up, kv_lat_g, w_kv_up, q_g, k_g, tm)
    o_sb = _sb_attention(qkv.reshape(b, s, -1), tq, tk).reshape(b * s, -1)
    o_mla = _mla_attention(qm.reshape(b, s, -1), km.reshape(b, s, -1), vm.reshape(b, s, -1), tq).reshape(b * s, -1)
    h1, h1n = _out_projection(x2, o_sb, o_mla, w_out, dffn_norm, tm)
    return _dense_ffn(h1, h1n, w_gate, w_up, w_down, tm, f_chunk)


SSM_SEGS = 8
SSM_SEG_LEN = 64
SSM_TILE = SSM_SEGS * SSM_SEG_LEN
SSM_LANE_CHUNK = 512


def _s5_prep_kernel(are_ref, aim_ref, logdt_ref, bre_ref, bim_ref, cre_ref, cim_ref,
                    wbu_ref, wc_ref, pow_ref):
    n = SSM_NSTATE
    a_re = jnp.minimum(are_ref[...], EIG_RE_MAX)
    a_im = aim_ref[...]
    dt = jnp.exp(logdt_ref[...])
    mag = jnp.exp(a_re * dt)
    lam_re = mag * jnp.cos(a_im * dt)
    lam_im = mag * jnp.sin(a_im * dt)
    den = a_re * a_re + a_im * a_im
    num_re = lam_re - 1.0
    coef_re = (num_re * a_re + lam_im * a_im) / den
    coef_im = (lam_im * a_re - num_re * a_im) / den
    b_re = bre_ref[...]
    b_im = bim_ref[...]
    in_group = (lax.broadcasted_iota(jnp.int32, (SSM_WIDTH, n), 0) // SSM_GROUP
                == lax.broadcasted_iota(jnp.int32, (SSM_WIDTH, n), 1) // SSM_STATE)
    wbu_ref[:, :n] = jnp.where(in_group, coef_re * b_re - coef_im * b_im, 0.0).astype(BF16)
    wbu_ref[:, n:] = jnp.where(in_group, coef_re * b_im + coef_im * b_re, 0.0).astype(BF16)
    in_group_t = (lax.broadcasted_iota(jnp.int32, (n, SSM_WIDTH), 0) // SSM_STATE
                  == lax.broadcasted_iota(jnp.int32, (n, SSM_WIDTH), 1) // SSM_GROUP)
    wc_ref[:n, :] = jnp.where(in_group_t, cre_ref[...], 0.0).astype(BF16)
    wc_ref[n:, :] = jnp.where(in_group_t, -cim_ref[...], 0.0).astype(BF16)
    steps = (lax.broadcasted_iota(jnp.int32, (SSM_SEG_LEN, n), 0) + 1).astype(F32)
    mag_k = jnp.exp(steps * (a_re * dt))
    ang_k = steps * (a_im * dt)
    pow_ref[:, :n] = mag_k * jnp.cos(ang_k)
    pow_ref[:, n:] = mag_k * jnp.sin(ang_k)


def _s5_prepare(a_re, a_im, log_dt, b_re, b_im, c_re, c_im):
    n = SSM_NSTATE
    row = lambda a: a.reshape(1, n)
    b_t = lambda w: jnp.tile(w.transpose(2, 0, 1).reshape(SSM_GROUP, n), (SSM_GROUPS, 1))
    c_t = lambda w: jnp.tile(w.transpose(0, 2, 1).reshape(n, SSM_GROUP), (1, SSM_GROUPS))
    return pl.pallas_call(
        _s5_prep_kernel,
        out_shape=[jax.ShapeDtypeStruct((SSM_WIDTH, 2 * n), BF16),
                   jax.ShapeDtypeStruct((2 * n, SSM_WIDTH), BF16),
                   jax.ShapeDtypeStruct((SSM_SEG_LEN, 2 * n), F32)],
        compiler_params=pltpu.CompilerParams(vmem_limit_bytes=VMEM_LIMIT),
        name="s5_discretize",
    )(row(a_re), row(a_im), row(jnp.repeat(log_dt, SSM_STATE)), b_t(b_re), b_t(b_im), c_t(c_re), c_t(c_im))


def _s5_kernel(h_ref, g_ref, win_ref, wbu_ref, pow_ref, wc_ref, d_ref, wglu_ref, o_ref,
               st_ref, carry_ref, y_ref):
    n = SSM_NSTATE
    d_model = h_ref.shape[-1]

    @pl.when(pl.program_id(1) == 0)
    def _():
        carry_ref[...] = jnp.zeros_like(carry_ref)

    x = h_ref[...]
    xn = _rms(x, d_model) * g_ref[...]
    u = _dot(xn.astype(BF16), win_ref[...])
    u16 = u.astype(BF16)
    nt = n // LANES
    per = SSM_LANE_CHUNK // LANES
    for lc in range(2 * n // SSM_LANE_CHUNK):
        bu = _dot(u16, wbu_ref[:, lc * SSM_LANE_CHUNK:(lc + 1) * SSM_LANE_CHUNK])
        for k in range(per):
            st_ref[lc * per + k] = bu[:, k * LANES:(k + 1) * LANES]

    for lc in range(nt // per):
        tiles = range(lc * per, (lc + 1) * per)
        lam_re = [jnp.broadcast_to(pow_ref[0:1, k * LANES:(k + 1) * LANES], (SSM_SEGS, LANES)) for k in tiles]
        lam_im = [jnp.broadcast_to(pow_ref[0:1, n + k * LANES:n + (k + 1) * LANES], (SSM_SEGS, LANES))
                  for k in tiles]

        def step(i, state):
            rows = pl.ds(i, SSM_SEGS, stride=SSM_SEG_LEN)
            new_state = []
            for idx, k in enumerate(tiles):
                s_re, s_im = state[idx]
                n_re = lam_re[idx] * s_re - lam_im[idx] * s_im + st_ref[k, rows, :]
                n_im = lam_re[idx] * s_im + lam_im[idx] * s_re + st_ref[nt + k, rows, :]
                st_ref[k, rows, :] = n_re
                st_ref[nt + k, rows, :] = n_im
                new_state.append((n_re, n_im))
            return tuple(new_state)

        zero = jnp.zeros((SSM_SEGS, LANES), F32)
        lax.fori_loop(0, SSM_SEG_LEN, step, tuple((zero, zero) for _ in tiles))

    for lc in range(nt // per):
        tiles = range(lc * per, (lc + 1) * per)
        re_cols = slice(lc * SSM_LANE_CHUNK, (lc + 1) * SSM_LANE_CHUNK)
        im_cols = slice(n + lc * SSM_LANE_CHUNK, n + (lc + 1) * SSM_LANE_CHUNK)
        p_re = pow_ref[:, re_cols]
        p_im = pow_ref[:, im_cols]
        seg_pow_re = pow_ref[SSM_SEG_LEN - 1:SSM_SEG_LEN, re_cols]
        seg_pow_im = pow_ref[SSM_SEG_LEN - 1:SSM_SEG_LEN, im_cols]
        c_re = carry_ref[:, re_cols]
        c_im = carry_ref[:, im_cols]
        for j in range(SSM_SEGS):
            rows = pl.ds(j * SSM_SEG_LEN, SSM_SEG_LEN)
            loc_re = jnp.concatenate([st_ref[k, rows, :] for k in tiles], axis=-1)
            loc_im = jnp.concatenate([st_ref[nt + k, rows, :] for k in tiles], axis=-1)
            full_re = loc_re + (p_re * c_re - p_im * c_im)
            full_im = loc_im + (p_re * c_im + p_im * c_re)
            part = (_dot(full_re.astype(BF16), wc_ref[re_cols, :])
                    + _dot(full_im.astype(BF16), wc_ref[im_cols, :]))
            if lc == 0:
                y_ref[rows, :] = part
            else:
                y_ref[rows, :] += part
            end_re = loc_re[SSM_SEG_LEN - 1:, :]
            end_im = loc_im[SSM_SEG_LEN - 1:, :]
            c_re, c_im = (end_re + (seg_pow_re * c_re - seg_pow_im * c_im),
                          end_im + (seg_pow_re * c_im + seg_pow_im * c_re))
        carry_ref[:, re_cols] = c_re
        carry_ref[:, im_cols] = c_im

    y = jax.nn.gelu(y_ref[...] + d_ref[...] * u)
    z = _dot(y.astype(BF16), wglu_ref[...])
    o_ref[...] = x + z[:, :d_model] * jax.nn.sigmoid(z[:, d_model:])


def _s5_layer(h, b, s, norm_g, w_in, a_re, a_im, log_dt, b_re, b_im, c_re, c_im, d_skip, w_glu):
    t, d = h.shape
    n = SSM_NSTATE
    w_bu, w_c, powers = _s5_prepare(a_re, a_im, log_dt, b_re, b_im, c_re, c_im)
    tiles = s // SSM_TILE
    resident = lambda shape: pl.BlockSpec(shape, lambda bi, ti: (0, 0), pipeline_mode=pl.Buffered(1))
    row = lambda bi, ti: (bi * tiles + ti, 0)
    return pl.pallas_call(
        _s5_kernel,
        grid=(b, tiles),
        in_specs=[pl.BlockSpec((SSM_TILE, d), row), resident((1, d)), resident((d, SSM_WIDTH)),
                  resident((SSM_WIDTH, 2 * n)), resident((SSM_SEG_LEN, 2 * n)), resident((2 * n, SSM_WIDTH)),
                  resident((1, SSM_WIDTH)), resident((SSM_WIDTH, 2 * d))],
        out_specs=pl.BlockSpec((SSM_TILE, d), row),
        out_shape=jax.ShapeDtypeStruct((t, d), F32),
        scratch_shapes=[pltpu.VMEM((2 * n // LANES, SSM_TILE, LANES), F32), pltpu.VMEM((1, 2 * n), F32),
                        pltpu.VMEM((SSM_TILE, SSM_WIDTH), F32)],
        compiler_params=_params("arbitrary", "arbitrary"),
        name="s5_mixer",
    )(h, norm_g.reshape(1, d), w_in.astype(BF16), w_bu, powers, w_c, d_skip.reshape(1, -1), w_glu.astype(BF16))


MOE_BLOCK = 256
META_E1, META_E2, META_G1, META_G2, META_R1, META_R2 = range(6)


def _router_kernel(h_ref, g_ref, wr_ref, tri_ref, xn_ref, meta_ref, counts_ref, run_ref):
    @pl.when(pl.program_id(0) == 0)
    def _():
        run_ref[...] = jnp.zeros_like(run_ref)

    x = h_ref[...]
    xn = _rms(x, x.shape[-1]) * g_ref[...]
    xn_ref[...] = xn
    logits = jnp.dot(xn, wr_ref[...], preferred_element_type=F32, precision=lax.Precision.HIGHEST)
    tm = logits.shape[0]
    e = lax.broadcasted_iota(jnp.int32, (tm, N_EXPERTS), 1)
    v1 = jnp.max(logits, axis=-1, keepdims=True)
    i1 = jnp.min(jnp.where(logits == v1, e, N_EXPERTS), axis=-1, keepdims=True)
    rest = jnp.where(e == i1, -jnp.inf, logits)
    v2 = jnp.max(rest, axis=-1, keepdims=True)
    i2 = jnp.min(jnp.where(rest == v2, e, N_EXPERTS), axis=-1, keepdims=True)
    ex = jnp.exp(v2 - v1)
    g1 = 1.0 / (1.0 + ex)
    g2 = ex / (1.0 + ex)
    oh1 = (e == i1).astype(F32)
    oh2 = (e == i2).astype(F32)
    both = oh1 + oh2
    before = _dot(tri_ref[...], both.astype(BF16)) + run_ref[...]
    r1 = jnp.sum(oh1 * before, axis=-1, keepdims=True)
    r2 = jnp.sum(oh2 * before, axis=-1, keepdims=True)
    run_ref[...] += jnp.sum(both, axis=0, keepdims=True)
    counts_ref[...] = run_ref[...]
    meta = jnp.zeros((tm, N_EXPERTS), F32)
    for lane, val in ((META_E1, i1.astype(F32)), (META_E2, i2.astype(F32)), (META_G1, g1), (META_G2, g2),
                      (META_R1, r1), (META_R2, r2)):
        meta = jnp.where(e == lane, val, meta)
    meta_ref[...] = meta


def _router(h, norm_g, w_router, tm):
    t, d = h.shape
    row = lambda i: (i, 0)
    tri = (jnp.arange(tm)[:, None] > jnp.arange(tm)[None, :]).astype(BF16)
    return pl.pallas_call(
        _router_kernel,
        grid=(t // tm,),
        in_specs=[pl.BlockSpec((tm, d), row), _const_spec((1, d)), _const_spec((d, N_EXPERTS)),
                  _const_spec((tm, tm))],
        out_specs=[pl.BlockSpec((tm, d), row), pl.BlockSpec((tm, N_EXPERTS), row), _const_spec((1, N_EXPERTS))],
        out_shape=[jax.ShapeDtypeStruct((t, d), F32), jax.ShapeDtypeStruct((t, N_EXPERTS), F32),
                   jax.ShapeDtypeStruct((1, N_EXPERTS), F32)],
        scratch_shapes=[pltpu.VMEM((1, N_EXPERTS), F32)],
        compiler_params=_params("arbitrary"),
        name="moe_router",
    )(h, norm_g.reshape(1, d), w_router, tri)


def _row_copy(src_ref, src_row, dst_ref, dst_row, sem):
    return pltpu.make_async_copy(src_ref.at[pl.ds(src_row, 1)], dst_ref.at[pl.ds(dst_row, 1)], sem)


def _dispatch_kernel(d1_ref, d2_ref, xn_ref, xs_in_ref, xs_ref, sem, *, tm):
    del xs_in_ref
    base = pl.program_id(0) * tm

    def issue(r, _):
        tok = base + r
        _row_copy(xn_ref, r, xs_ref, d1_ref[tok], sem).start()
        _row_copy(xn_ref, r, xs_ref, d2_ref[tok], sem).start()
        return 0

    lax.fori_loop(0, tm, issue, 0)

    def drain(r, _):
        _row_copy(xn_ref, 0, xs_ref, 0, sem).wait()
        _row_copy(xn_ref, 0, xs_ref, 0, sem).wait()
        return 0

    lax.fori_loop(0, tm, drain, 0)


def _dispatch(xn, dest1, dest2, n_slots, tm):
    t, d = xn.shape
    any_spec = pl.BlockSpec(memory_space=pl.ANY)
    return pl.pallas_call(
        functools.partial(_dispatch_kernel, tm=tm),
        grid_spec=pltpu.PrefetchScalarGridSpec(
            num_scalar_prefetch=2, grid=(t // tm,),
            in_specs=[pl.BlockSpec((tm, d), lambda i, d1, d2: (i, 0)), any_spec], out_specs=any_spec,
            scratch_shapes=[pltpu.SemaphoreType.DMA(())]),
        out_shape=jax.ShapeDtypeStruct((n_slots, d), F32),
        input_output_aliases={3: 0},
        compiler_params=_params("arbitrary"),
        name="moe_dispatch",
    )(dest1, dest2, xn, jnp.zeros((n_slots, d), F32))


def _expert_ffn_kernel(be_ref, bv_ref, xs_ref, wg_ref, wu_ref, wd_ref, ys_ref, *, f_chunk):
    del be_ref
    valid = bv_ref[pl.program_id(0)] != 0

    @pl.when(valid)
    def _():
        ys_ref[...] = _swiglu(xs_ref[...].astype(BF16), wg_ref, wu_ref, wd_ref, f_chunk)

    @pl.when(jnp.logical_not(valid))
    def _():
        ys_ref[...] = jnp.zeros_like(ys_ref)


def _expert_ffn(xs, blk_expert, blk_valid, w_gate, w_up, w_down, f_chunk):
    n_slots, d = xs.shape
    d_ff = w_gate.shape[-1]
    row = lambda i, be, bv: (i, 0)
    return pl.pallas_call(
        functools.partial(_expert_ffn_kernel, f_chunk=f_chunk),
        grid_spec=pltpu.PrefetchScalarGridSpec(
            num_scalar_prefetch=2, grid=(n_slots // MOE_BLOCK,),
            in_specs=[pl.BlockSpec((MOE_BLOCK, d), row),
                      pl.BlockSpec((None, d, d_ff), lambda i, be, bv: (be[i], 0, 0)),
                      pl.BlockSpec((None, d, d_ff), lambda i, be, bv: (be[i], 0, 0)),
                      pl.BlockSpec((None, d_ff, d), lambda i, be, bv: (be[i], 0, 0))],
            out_specs=pl.BlockSpec((MOE_BLOCK, d), row)),
        out_shape=jax.ShapeDtypeStruct((n_slots, d), F32),
        compiler_params=_params("arbitrary"),
        name="expert_swiglu",
    )(blk_expert, blk_valid, xs, w_gate.astype(BF16), w_up.astype(BF16), w_down.astype(BF16))


def _combine_kernel(d1_ref, d2_ref, h_ref, meta_ref, ys_ref, o_ref, a_ref, b_ref, sem, *, tm):
    base = pl.program_id(0) * tm

    def issue(r, _):
        tok = base + r
        _row_copy(ys_ref, d1_ref[tok], a_ref, r, sem).start()
        _row_copy(ys_ref, d2_ref[tok], b_ref, r, sem).start()
        return 0

    lax.fori_loop(0, tm, issue, 0)

    def drain(r, _):
        _row_copy(ys_ref, 0, a_ref, 0, sem).wait()
        _row_copy(ys_ref, 0, b_ref, 0, sem).wait()
        return 0

    lax.fori_loop(0, tm, drain, 0)
    meta = meta_ref[...]
    g1 = meta[:, META_G1:META_G1 + 1]
    g2 = meta[:, META_G2:META_G2 + 1]
    o_ref[...] = h_ref[...] + (g1 * a_ref[...] + g2 * b_ref[...])


def _combine(h, meta, ys, dest1, dest2, tm):
    t, d = h.shape
    row = lambda i, d1, d2: (i, 0)
    return pl.pallas_call(
        functools.partial(_combine_kernel, tm=tm),
        grid_spec=pltpu.PrefetchScalarGridSpec(
            num_scalar_prefetch=2, grid=(t // tm,),
            in_specs=[pl.BlockSpec((tm, d), row), pl.BlockSpec((tm, N_EXPERTS), row),
                      pl.BlockSpec(memory_space=pl.ANY)],
            out_specs=pl.BlockSpec((tm, d), row),
            scratch_shapes=[pltpu.VMEM((tm, d), F32), pltpu.VMEM((tm, d), F32), pltpu.SemaphoreType.DMA(())]),
        out_shape=jax.ShapeDtypeStruct((t, d), F32),
        compiler_params=_params("arbitrary"),
        name="moe_combine",
    )(dest1, dest2, h, meta, ys)


def _moe_layer(h, norm_g, w_router, w_gate, w_up, w_down, tm, f_chunk):
    t, d = h.shape
    xn, meta, counts = _router(h, norm_g, w_router, tm)
    counts = counts.reshape(N_EXPERTS).astype(jnp.int32)
    padded = (counts + MOE_BLOCK - 1) // MOE_BLOCK * MOE_BLOCK
    pad_end = jnp.cumsum(padded)
    pad_start = pad_end - padded
    e1 = meta[:, META_E1].astype(jnp.int32)
    e2 = meta[:, META_E2].astype(jnp.int32)
    dest1 = pad_start[e1] + meta[:, META_R1].astype(jnp.int32)
    dest2 = pad_start[e2] + meta[:, META_R2].astype(jnp.int32)
    n_slots = 2 * t + N_EXPERTS * MOE_BLOCK
    blk_start = jnp.arange(n_slots // MOE_BLOCK, dtype=jnp.int32) * MOE_BLOCK
    blk_expert = jnp.minimum(jnp.sum(blk_start[:, None] >= pad_end[None, :], axis=1), N_EXPERTS - 1).astype(jnp.int32)
    blk_valid = (blk_start < pad_end[-1]).astype(jnp.int32)
    xs = _dispatch(xn, dest1, dest2, n_slots, tm)
    ys = _expert_ffn(xs, blk_expert, blk_valid, w_gate, w_up, w_down, f_chunk)
    return _combine(h, meta, ys, dest1, dest2, tm)


def kernel(x, att_norm, att_w_in, att_q_latent_norm, att_w_q_up, att_kv_latent_norm, att_w_kv_up, att_q_norm, att_k_norm, att_w_out, dffn_norm, dffn_w_gate, dffn_w_up, dffn_w_down, ssm_norm, ssm_w_in, ssm_a_re, ssm_a_im, ssm_log_dt, ssm_b_re, ssm_b_im, ssm_c_re, ssm_c_im, ssm_d, ssm_w_glu, moe_norm, moe_router, moe_w_gate, moe_w_up, moe_w_down):
    b, s, d = x.shape
    x2 = x.reshape(b * s, d)
    h = _layer0(x2, b, s, att_norm[0], att_w_in[0], att_q_latent_norm[0], att_w_q_up[0], att_kv_latent_norm[0],
                att_w_kv_up[0], att_q_norm[0], att_k_norm[0], att_w_out[0], dffn_norm[0], dffn_w_gate[0],
                dffn_w_up[0], dffn_w_down[0], tm=256, tq=512, tk=256, f_chunk=512)
    h = _s5_layer(h, b, s, ssm_norm[0], ssm_w_in[0], ssm_a_re[0], ssm_a_im[0], ssm_log_dt[0], ssm_b_re[0],
                  ssm_b_im[0], ssm_c_re[0], ssm_c_im[0], ssm_d[0], ssm_w_glu[0])
    h = _moe_layer(h, moe_norm[0], moe_router[0], moe_w_gate[0], moe_w_up[0], moe_w_down[0], tm=256, f_chunk=512)
    return h.reshape(b, s, d)
```

```python
import functools
import math

import jax
import jax.numpy as jnp
from jax import lax
from jax.experimental import pallas as pl
from jax.experimental.pallas import tpu as pltpu

F32 = jnp.float32
BF16 = jnp.bfloat16

EPS = 1e-6
LANES = 128
SB_HEADS = 8
SB_HEAD_DIM = 64
SB_WIDTH = SB_HEADS * SB_HEAD_DIM
MLA_HEADS = 8
MLA_Q_RANK = 256
MLA_KV_RANK = 128
MLA_NOPE_DIM = 64
MLA_ROPE_DIM = 32
MLA_QK_DIM = MLA_NOPE_DIM + MLA_ROPE_DIM
MLA_V_DIM = 64
MLA_WIDTH = MLA_HEADS * MLA_V_DIM
ROPE_THETA = 10000.0
SSM_GROUP = 16
SSM_GROUPS = 32
SSM_STATE = 64
SSM_WIDTH = SSM_GROUP * SSM_GROUPS
SSM_NSTATE = SSM_GROUPS * SSM_STATE
EIG_RE_MAX = -1e-4
N_EXPERTS = 8
VMEM_LIMIT = 56 * 1024 * 1024


def _params(*sem):
    return pltpu.CompilerParams(dimension_semantics=sem, vmem_limit_bytes=VMEM_LIMIT)


def _rms(x, n):
    return x * lax.rsqrt(jnp.sum(x * x, axis=-1, keepdims=True) * (1.0 / n) + EPS)


def _dot(a, b):
    return jnp.dot(a, b, preferred_element_type=F32)


def _dot_nt(a, b):
    return lax.dot_general(a, b, (((1,), (1,)), ((), ())), preferred_element_type=F32)


def _const_spec(shape):
    return pl.BlockSpec(shape, lambda *_: (0,) * len(shape))


IN_SB = 3 * SB_WIDTH
IN_CQ = IN_SB
IN_CKV = IN_CQ + MLA_Q_RANK
IN_KR = IN_CKV + MLA_KV_RANK
IN_KRS = IN_KR + LANES
IN_COLS_PAD = IN_KRS + LANES
MLA_PAD = MLA_HEADS * LANES


def _inproj_kernel(x_ref, g_ref, win_ref, qlg_ref, wq_ref, kvlg_ref, wkv_ref, ctab_ref, stab_ref,
                   qg_ref, qgs_ref, kg_ref, kgs_ref, qkv_ref, qm_ref, km_ref, vm_ref):
    x = x_ref[...]
    xn = _rms(x, x.shape[-1]) * g_ref[...]
    hh = _dot(xn.astype(BF16), win_ref[...])
    qkv_ref[:, :SB_WIDTH] = (hh[:, :SB_WIDTH] * (SB_HEAD_DIM ** -0.5)).astype(BF16)
    qkv_ref[:, SB_WIDTH:] = hh[:, SB_WIDTH:IN_SB].astype(BF16)
    cq = _rms(hh[:, IN_CQ:IN_CKV], MLA_Q_RANK) * qlg_ref[...]
    qf = _dot(cq.astype(BF16), wq_ref[...])
    ckv = _rms(hh[:, IN_CKV:IN_KR], MLA_KV_RANK) * kvlg_ref[...]
    kvf = _dot(ckv.astype(BF16), wkv_ref[...])
    vm_ref[...] = kvf[:, MLA_PAD:].astype(BF16)
    kr = hh[:, IN_KR:IN_KRS]
    krs = hh[:, IN_KRS:IN_COLS_PAD]
    ctab = ctab_ref[...]
    stab = stab_ref[...]
    scale = MLA_QK_DIM ** -0.5
    cq_t = ctab * (qg_ref[...] * scale)
    sq_t = stab * (qgs_ref[...] * scale)
    ck_t = ctab * kg_ref[...]
    sk_t = stab * kgs_ref[...]
    for h in range(MLA_HEADS):
        lo, hi = h * LANES, (h + 1) * LANES
        qh = qf[:, lo:hi]
        qs = qf[:, MLA_PAD + lo:MLA_PAD + hi]
        q_inv = lax.rsqrt(jnp.sum(qh * qh, axis=-1, keepdims=True) * (1.0 / MLA_QK_DIM) + EPS)
        qm_ref[:, lo:hi] = ((qh * cq_t + qs * sq_t) * q_inv).astype(BF16)
        kh = kvf[:, lo:hi] + kr
        k_inv = lax.rsqrt(jnp.sum(kh * kh, axis=-1, keepdims=True) * (1.0 / MLA_QK_DIM) + EPS)
        km_ref[:, lo:hi] = ((kh * ck_t + krs * sk_t) * k_inv).astype(BF16)


def _pad_heads(w, n_heads, width):
    k = w.shape[0]
    w = w.reshape(k, n_heads, width)
    return jnp.pad(w, ((0, 0), (0, 0), (0, LANES - width))).reshape(k, n_heads * LANES)


def _swap_rope(w):
    half = MLA_ROPE_DIM // 2
    return jnp.concatenate([jnp.zeros_like(w[..., :MLA_NOPE_DIM]),
                            w[..., MLA_NOPE_DIM + half:], w[..., MLA_NOPE_DIM:MLA_NOPE_DIM + half]], axis=-1)


def _rope_lane_tables(s_len):
    half = MLA_ROPE_DIM // 2
    inv_freq = ROPE_THETA ** (-jnp.arange(0, MLA_ROPE_DIM, 2, dtype=F32) / MLA_ROPE_DIM)
    ang = jnp.arange(s_len, dtype=F32)[:, None] * inv_freq[None, :]
    cos, sin = jnp.cos(ang), jnp.sin(ang)
    ones = jnp.ones((s_len, MLA_NOPE_DIM), F32)
    zeros_n = jnp.zeros((s_len, MLA_NOPE_DIM), F32)
    zeros_t = jnp.zeros((s_len, LANES - MLA_QK_DIM), F32)
    ctab = jnp.concatenate([ones, cos, cos, zeros_t], axis=-1)
    stab = jnp.concatenate([zeros_n, -sin, sin, zeros_t], axis=-1)
    del half
    return ctab, stab


def _in_projection(x2, s_len, norm_g, w_in, q_lat_g, w_q_up, kv_lat_g, w_kv_up, q_g, k_g, tm):
    t, d = x2.shape
    w_kr = w_in[:, IN_KR - LANES + 0:][:, :0]
    del w_kr
    off_kr = 3 * SB_WIDTH + MLA_Q_RANK + MLA_KV_RANK
    w_rope = w_in[:, off_kr:]
    zeros_n = jnp.zeros((d, MLA_NOPE_DIM), F32)
    zeros_t = jnp.zeros((d, LANES - MLA_QK_DIM), F32)
    half = MLA_ROPE_DIM // 2
    w_in_pad = jnp.concatenate(
        [w_in[:, :off_kr], zeros_n, w_rope, zeros_t,
         zeros_n, w_rope[:, half:], w_rope[:, :half], zeros_t], axis=-1).astype(BF16)
    wq3 = w_q_up.reshape(MLA_Q_RANK, MLA_HEADS, MLA_QK_DIM)
    wq_pad = jnp.concatenate(
        [_pad_heads(w_q_up, MLA_HEADS, MLA_QK_DIM),
         _pad_heads(_swap_rope(wq3).reshape(MLA_Q_RANK, -1), MLA_HEADS, MLA_QK_DIM)], axis=-1).astype(BF16)
    wkv3 = w_kv_up.reshape(MLA_KV_RANK, MLA_HEADS, MLA_NOPE_DIM + MLA_V_DIM)
    wkv_pad = jnp.concatenate(
        [_pad_heads(wkv3[..., :MLA_NOPE_DIM].reshape(MLA_KV_RANK, -1), MLA_HEADS, MLA_NOPE_DIM),
         wkv3[..., MLA_NOPE_DIM:].reshape(MLA_KV_RANK, -1)], axis=-1).astype(BF16)
    ctab, stab = _rope_lane_tables(s_len)
    pad_g = lambda g: jnp.pad(g, (0, LANES - MLA_QK_DIM)).reshape(1, LANES)
    swap_g = lambda g: jnp.pad(jnp.concatenate(
        [jnp.zeros((MLA_NOPE_DIM,), F32), g[MLA_NOPE_DIM + half:], g[MLA_NOPE_DIM:MLA_NOPE_DIM + half]]),
        (0, LANES - MLA_QK_DIM)).reshape(1, LANES)
    n_pos = s_len // tm
    row = lambda i: (i, 0)
    return pl.pallas_call(
        _inproj_kernel,
        grid=(t // tm,),
        in_specs=[
            pl.BlockSpec((tm, d), row),
            _const_spec((1, d)),
            _const_spec((d, IN_COLS_PAD)),
            _const_spec((1, MLA_Q_RANK)),
            _const_spec((MLA_Q_RANK, 2 * MLA_PAD)),
            _const_spec((1, MLA_KV_RANK)),
            _const_spec((MLA_KV_RANK, MLA_PAD + MLA_WIDTH)),
            pl.BlockSpec((tm, LANES), lambda i: (i % n_pos, 0)),
            pl.BlockSpec((tm, LANES), lambda i: (i % n_pos, 0)),
            _const_spec((1, LANES)), _const_spec((1, LANES)),
            _const_spec((1, LANES)), _const_spec((1, LANES)),
        ],
        out_specs=[
            pl.BlockSpec((tm, IN_SB), row),
            pl.BlockSpec((tm, MLA_PAD), row),
            pl.BlockSpec((tm, MLA_PAD), row),
            pl.BlockSpec((tm, MLA_WIDTH), row),
        ],
        out_shape=[
            jax.ShapeDtypeStruct((t, IN_SB), BF16),
            jax.ShapeDtypeStruct((t, MLA_PAD), BF16),
            jax.ShapeDtypeStruct((t, MLA_PAD), BF16),
            jax.ShapeDtypeStruct((t, MLA_WIDTH), BF16),
        ],
        compiler_params=_params("arbitrary"),
        name="in_projection",
    )(x2, norm_g.reshape(1, d), w_in_pad, q_lat_g.reshape(1, -1), wq_pad, kv_lat_g.reshape(1, -1), wkv_pad,
      ctab, stab, pad_g(q_g), swap_g(q_g), pad_g(k_g), swap_g(k_g))


SB_EXP_UNDERFLOW = 110.0


def _sb_kernel(q_ref, k_ref, v_ref, u_ref, o_ref, acc_ref, carry_ref, *, tq, tk):
    qi = pl.program_id(2)
    lane = lax.broadcasted_iota(jnp.int32, (1, LANES), 1)
    row = lax.broadcasted_iota(jnp.int32, (tq, tk), 0)
    col = lax.broadcasted_iota(jnp.int32, (tq, tk), 1)
    q = q_ref[...]
    q_heads = [jnp.where((lane >= j * SB_HEAD_DIM) & (lane < (j + 1) * SB_HEAD_DIM), q, jnp.zeros_like(q))
               for j in range(2)]
    acc_ref[...] = jnp.zeros_like(acc_ref)
    carry_ref[...] = jnp.zeros_like(carry_ref)

    def sweep(start, diagonal):
        start = pl.multiple_of(start, tk)
        kc = k_ref[pl.ds(start, tk), :]
        vc = v_ref[pl.ds(start, tk), :]
        if diagonal:
            earlier = (start + col) < (qi * tq + row)
        for j in range(2):
            z = _dot_nt(q_heads[j], kc)
            log_fail = -(jnp.maximum(z, 0.0) + jnp.log1p(jnp.exp(-jnp.abs(z))))
            if diagonal:
                log_fail = jnp.where(earlier, log_fail, 0.0)
            hi = log_fail.astype(BF16)
            lo = (log_fail - hi.astype(F32)).astype(BF16)
            log_stick = _dot(hi, u_ref[...]) + _dot(lo, u_ref[...])
            carry = carry_ref[j]
            w = jnp.exp(z + log_fail + log_stick + jnp.tile(carry, (1, tk // LANES)))
            if diagonal:
                w = jnp.where(earlier, w, 0.0)
            acc_ref[j] += _dot(w.astype(BF16), vc)
            carry_ref[j] = carry + (log_stick[:, :1] + log_fail[:, :1])

    for sub in reversed(range(tq // tk)):
        sweep(qi * tq + sub * tk, True)

    def sticks_alive():
        return jnp.max(carry_ref[...]) > -SB_EXP_UNDERFLOW

    def earlier_chunk(state):
        c, _ = state
        sweep(c * tk, False)
        return c - 1, sticks_alive()

    lax.while_loop(lambda state: (state[0] >= 0) & state[1], earlier_chunk,
                   (qi * (tq // tk) - 1, sticks_alive()))
    o_ref[...] = jnp.where(lane < SB_HEAD_DIM, acc_ref[0], acc_ref[1]).astype(o_ref.dtype)


def _sb_attention(qkv, tq, tk):
    b, s, _ = qkv.shape
    n_pairs = SB_WIDTH // LANES
    tri = (jnp.arange(tk)[:, None] > jnp.arange(tk)[None, :]).astype(BF16)
    return pl.pallas_call(
        functools.partial(_sb_kernel, tq=tq, tk=tk),
        grid=(b, n_pairs, s // tq),
        in_specs=[
            pl.BlockSpec((None, tq, LANES), lambda bi, hp, qi: (bi, qi, hp)),
            pl.BlockSpec((None, s, LANES), lambda bi, hp, qi: (bi, 0, n_pairs + hp)),
            pl.BlockSpec((None, s, LANES), lambda bi, hp, qi: (bi, 0, 2 * n_pairs + hp)),
            _const_spec((tk, tk)),
        ],
        out_specs=pl.BlockSpec((None, tq, LANES), lambda bi, hp, qi: (bi, qi, hp)),
        out_shape=jax.ShapeDtypeStruct((b, s, SB_WIDTH), BF16),
        scratch_shapes=[pltpu.VMEM((2, tq, LANES), F32), pltpu.VMEM((2, tq, LANES), F32)],
        compiler_params=_params("arbitrary", "arbitrary", "arbitrary"),
        name="stick_breaking_attention",
    )(qkv, qkv, qkv, tri)


def _mla_kernel(q_ref, k_ref, v_ref, o_ref, acc_ref, m_ref, l_ref, *, tq):
    qi = pl.program_id(2)
    lane = lax.broadcasted_iota(jnp.int32, (1, LANES), 1)
    acc_ref[...] = jnp.zeros_like(acc_ref)
    l_ref[...] = jnp.zeros_like(l_ref)
    m_ref[...] = jnp.full(m_ref.shape, -jnp.inf, F32)

    def sweep(c, diagonal):
        start = pl.multiple_of(c * tq, tq)
        vc = v_ref[pl.ds(start, tq), :]
        for j in range(2):
            s = _dot_nt(q_ref[:, j * LANES:(j + 1) * LANES], k_ref[pl.ds(start, tq), j * LANES:(j + 1) * LANES])
            if diagonal:
                row = lax.broadcasted_iota(jnp.int32, (tq, tq), 0)
                col = lax.broadcasted_iota(jnp.int32, (tq, tq), 1)
                s = jnp.where(col <= row, s, -jnp.inf)
            m_prev = m_ref[j]
            m_new = jnp.maximum(m_prev, jnp.max(s, axis=-1, keepdims=True))
            alpha = jnp.exp(m_prev - m_new)
            p = jnp.exp(s - jnp.tile(m_new, (1, tq // LANES)))
            l_ref[j] = alpha * l_ref[j] + jnp.sum(p, axis=-1, keepdims=True)
            acc_ref[j] = alpha * acc_ref[j] + _dot(p.astype(BF16), vc)
            m_ref[j] = m_new

    def earlier_block(c, _):
        sweep(c, False)
        return 0

    lax.fori_loop(0, qi, earlier_block, 0)
    sweep(qi, True)
    o_ref[...] = jnp.where(lane < MLA_V_DIM, acc_ref[0] / l_ref[0], acc_ref[1] / l_ref[1]).astype(o_ref.dtype)


def _mla_attention(q, k, v, tq):
    b, s, _ = q.shape
    n_pairs = MLA_WIDTH // LANES
    return pl.pallas_call(
        functools.partial(_mla_kernel, tq=tq),
        grid=(b, n_pairs, s // tq),
        in_specs=[
            pl.BlockSpec((None, tq, 2 * LANES), lambda bi, hp, qi: (bi, qi, hp)),
            pl.BlockSpec((None, s, 2 * LANES), lambda bi, hp, qi: (bi, 0, hp)),
            pl.BlockSpec((None, s, LANES), lambda bi, hp, qi: (bi, 0, hp)),
        ],
        out_specs=pl.BlockSpec((None, tq, LANES), lambda bi, hp, qi: (bi, qi, hp)),
        out_shape=jax.ShapeDtypeStruct((b, s, MLA_WIDTH), BF16),
        scratch_shapes=[pltpu.VMEM((2, tq, LANES), F32)] * 3,
        compiler_params=_params("arbitrary", "arbitrary", "arbitrary"),
        name="latent_attention",
    )(q, k, v)


def _outproj_kernel(x_ref, osb_ref, omla_ref, wsb_ref, wmla_ref, g_ref, h_ref, hn_ref):
    h = x_ref[...] + _dot(osb_ref[...], wsb_ref[...]) + _dot(omla_ref[...], wmla_ref[...])
    h_ref[...] = h
    hn_ref[...] = (_rms(h, h.shape[-1]) * g_ref[...]).astype(BF16)


def _out_projection(x2, o_sb, o_mla, w_out, next_norm_g, tm):
    t, d = x2.shape
    row = lambda i: (i, 0)
    w_sb = w_out[:SB_WIDTH].astype(BF16)
    w_mla = w_out[SB_WIDTH:].astype(BF16)
    return pl.pallas_call(
        _outproj_kernel,
        grid=(t // tm,),
        in_specs=[pl.BlockSpec((tm, d), row), pl.BlockSpec((tm, SB_WIDTH), row), pl.BlockSpec((tm, MLA_WIDTH), row),
                  _const_spec((SB_WIDTH, d)), _const_spec((MLA_WIDTH, d)), _const_spec((1, d))],
        out_specs=[pl.BlockSpec((tm, d), row), pl.BlockSpec((tm, d), row)],
        out_shape=[jax.ShapeDtypeStruct((t, d), F32), jax.ShapeDtypeStruct((t, d), BF16)],
        compiler_params=_params("arbitrary"),
        name="out_projection",
    )(x2, o_sb, o_mla, w_sb, w_mla, next_norm_g.reshape(1, d))


def _swiglu(x, wg_ref, wu_ref, wd_ref, f_chunk):
    d_ff = wg_ref.shape[-1]
    acc = None
    for c in range(d_ff // f_chunk):
        cols = slice(c * f_chunk, (c + 1) * f_chunk)
        gate = _dot(x, wg_ref[:, cols])
        up = _dot(x, wu_ref[:, cols])
        act = (gate * jax.nn.sigmoid(gate) * up).astype(BF16)
        part = _dot(act, wd_ref[cols, :])
        acc = part if acc is None else acc + part
    return acc


def _dense_ffn_kernel(h_ref, hn_ref, wg_ref, wu_ref, wd_ref, o_ref, *, f_chunk):
    o_ref[...] = h_ref[...] + _swiglu(hn_ref[...], wg_ref, wu_ref, wd_ref, f_chunk)


def _dense_ffn(h, hn, w_gate, w_up, w_down, tm, f_chunk):
    t, d = h.shape
    d_ff = w_gate.shape[-1]
    row = lambda i: (i, 0)
    resident = lambda shape: pl.BlockSpec(shape, lambda i: (0, 0), pipeline_mode=pl.Buffered(1))
    return pl.pallas_call(
        functools.partial(_dense_ffn_kernel, f_chunk=f_chunk),
        grid=(t // tm,),
        in_specs=[pl.BlockSpec((tm, d), row), pl.BlockSpec((tm, d), row),
                  resident((d, d_ff)), resident((d, d_ff)), resident((d_ff, d))],
        out_specs=pl.BlockSpec((tm, d), row),
        out_shape=jax.ShapeDtypeStruct((t, d), F32),
        compiler_params=_params("arbitrary"),
        name="dense_swiglu",
    )(h, hn, w_gate.astype(BF16), w_up.astype(BF16), w_down.astype(BF16))


def _layer0(x2, b, s, att_norm, att_w_in, q_lat_g, w_q_up, kv_lat_g, w_kv_up, q_g, k_g, w_out,
            dffn_norm, w_gate, w_up, w_down, tm, tq, tk, f_chunk):
    qkv, qm, km, vm = _in_projection(x2, s, att_norm, att_w_in, q_lat_g, w_q_up, kv_lat_g, w_kv_up, q_g, k_g, tm)
    o_sb = _sb_attention(qkv.reshape(b, s, -1), tq, tk).reshape(b * s, -1)
    o_mla = _mla_attention(qm.reshape(b, s, -1), km.reshape(b, s, -1), vm.reshape(b, s, -1), tq).reshape(b * s, -1)
    h1, h1n = _out_projection(x2, o_sb, o_mla, w_out, dffn_norm, tm)
    return _dense_ffn(h1, h1n, w_gate, w_up, w_down, tm, f_chunk)


SSM_SEGS = 8
SSM_SEG_LEN = 64
SSM_TILE = SSM_SEGS * SSM_SEG_LEN
SSM_CH_CHUNK = LANES
SSM_ST_CHUNK = SSM_CH_CHUNK // SSM_GROUP * SSM_STATE
SSM_SCAN_TILES = 8


def _s5_prep_kernel(are_ref, aim_ref, logdt_ref, bre_ref, bim_ref, cre_ref, cim_ref,
                    wbu_ref, wc_ref, pow_ref):
    n = SSM_NSTATE
    a_re = jnp.minimum(are_ref[...], EIG_RE_MAX)
    a_im = aim_ref[...]
    dt = jnp.exp(logdt_ref[...])
    mag = jnp.exp(a_re * dt)
    lam_re = mag * jnp.cos(a_im * dt)
    lam_im = mag * jnp.sin(a_im * dt)
    den = a_re * a_re + a_im * a_im
    num_re = lam_re - 1.0
    coef_re = (num_re * a_re + lam_im * a_im) / den
    coef_im = (lam_im * a_re - num_re * a_im) / den
    b_re = bre_ref[...]
    b_im = bim_ref[...]
    in_group = (lax.broadcasted_iota(jnp.int32, (SSM_CH_CHUNK, n), 0) // SSM_GROUP
                == lax.broadcasted_iota(jnp.int32, (SSM_CH_CHUNK, n), 1) % SSM_ST_CHUNK // SSM_STATE)
    wbu_ref[:, :n] = jnp.where(in_group, coef_re * b_re - coef_im * b_im, 0.0).astype(BF16)
    wbu_ref[:, n:] = jnp.where(in_group, coef_re * b_im + coef_im * b_re, 0.0).astype(BF16)
    in_group_t = (lax.broadcasted_iota(jnp.int32, (n, SSM_CH_CHUNK), 0) % SSM_ST_CHUNK // SSM_STATE
                  == lax.broadcasted_iota(jnp.int32, (n, SSM_CH_CHUNK), 1) // SSM_GROUP)
    wc_ref[:n, :] = jnp.where(in_group_t, cre_ref[...], 0.0).astype(BF16)
    wc_ref[n:, :] = jnp.where(in_group_t, -cim_ref[...], 0.0).astype(BF16)
    steps = jnp.where(lax.broadcasted_iota(jnp.int32, (SSM_SEGS, n), 0) == 0, 1.0, float(SSM_SEG_LEN))
    mag_k = jnp.exp(steps * (a_re * dt))
    ang_k = steps * (a_im * dt)
    pow_ref[:, :n] = mag_k * jnp.cos(ang_k)
    pow_ref[:, n:] = mag_k * jnp.sin(ang_k)


def _s5_prepare(a_re, a_im, log_dt, b_re, b_im, c_re, c_im):
    n = SSM_NSTATE
    row = lambda a: a.reshape(1, n)
    reps = SSM_CH_CHUNK // SSM_GROUP
    b_t = lambda w: jnp.tile(w.transpose(2, 0, 1).reshape(SSM_GROUP, n), (reps, 1))
    c_t = lambda w: jnp.tile(w.transpose(0, 2, 1).reshape(n, SSM_GROUP), (1, reps))
    return pl.pallas_call(
        _s5_prep_kernel,
        out_shape=[jax.ShapeDtypeStruct((SSM_CH_CHUNK, 2 * n), BF16),
                   jax.ShapeDtypeStruct((2 * n, SSM_CH_CHUNK), BF16),
                   jax.ShapeDtypeStruct((SSM_SEGS, 2 * n), F32)],
        compiler_params=pltpu.CompilerParams(vmem_limit_bytes=VMEM_LIMIT),
        name="s5_discretize",
    )(row(a_re), row(a_im), row(jnp.repeat(log_dt, SSM_STATE)), b_t(b_re), b_t(b_im), c_t(c_re), c_t(c_im))


def _s5_kernel(h_ref, g_ref, win_ref, wbu_ref, pow_ref, wc_ref, d_ref, wglu_ref, o_ref,
               ut_ref, up_ref, st_ref, carry_ref, yp_ref, yt_ref):
    n = SSM_NSTATE
    nt = n // LANES
    n_ch = SSM_WIDTH // LANES
    per = SSM_ST_CHUNK // LANES
    d_model = h_ref.shape[-1]

    @pl.when(pl.program_id(1) == 0)
    def _():
        carry_ref[...] = jnp.zeros_like(carry_ref)

    x = h_ref[...]
    xn = _rms(x, d_model) * g_ref[...]
    u = _dot(xn.astype(BF16), win_ref[...])
    for k in range(n_ch):
        ut_ref[k] = u[:, k * LANES:(k + 1) * LANES]

    def to_step_major(i, _):
        dst = pl.ds(pl.multiple_of(i * SSM_SEGS, SSM_SEGS), SSM_SEGS)
        for k in range(n_ch):
            up_ref[k, dst, :] = ut_ref[k, pl.ds(i, SSM_SEGS, stride=SSM_SEG_LEN), :]
        return 0

    lax.fori_loop(0, SSM_SEG_LEN, to_step_major, 0)

    for c in range(n_ch):
        u16 = up_ref[c].astype(BF16)
        for part in range(2):
            cols = slice(part * n + c * SSM_ST_CHUNK, part * n + (c + 1) * SSM_ST_CHUNK)
            bu = _dot(u16, wbu_ref[:, cols])
            for k in range(per):
                st_ref[part * nt + c * per + k] = bu[:, k * LANES:(k + 1) * LANES]

    for grp in range(nt // SSM_SCAN_TILES):
        tiles = range(grp * SSM_SCAN_TILES, (grp + 1) * SSM_SCAN_TILES)
        re_l = [slice(k * LANES, (k + 1) * LANES) for k in tiles]
        im_l = [slice(n + k * LANES, n + (k + 1) * LANES) for k in tiles]
        lam_re = [jnp.broadcast_to(pow_ref[0:1, c], (SSM_SEGS, LANES)) for c in re_l]
        lam_im = [jnp.broadcast_to(pow_ref[0:1, c], (SSM_SEGS, LANES)) for c in im_l]

        def scan(state, store):
            def step(i, state):
                rows = pl.ds(pl.multiple_of(i * SSM_SEGS, SSM_SEGS), SSM_SEGS)
                new_state = []
                for idx, k in enumerate(tiles):
                    s_re, s_im = state[idx]
                    n_re = lam_re[idx] * s_re - lam_im[idx] * s_im + st_ref[k, rows, :]
                    n_im = lam_re[idx] * s_im + lam_im[idx] * s_re + st_ref[nt + k, rows, :]
                    if store:
                        st_ref[k, rows, :] = n_re
                        st_ref[nt + k, rows, :] = n_im
                    new_state.append((n_re, n_im))
                return tuple(new_state)

            return lax.fori_loop(0, SSM_SEG_LEN, step, state)

        zero = jnp.zeros((SSM_SEGS, LANES), F32)
        ends = scan(tuple((zero, zero) for _ in tiles), False)
        starts = []
        for idx in range(len(tiles)):
            e_re, e_im = ends[idx]
            p_re = pow_ref[1:2, re_l[idx]]
            p_im = pow_ref[1:2, im_l[idx]]
            c_re = carry_ref[:, re_l[idx]]
            c_im = carry_ref[:, im_l[idx]]
            rows_re, rows_im = [], []
            for j in range(SSM_SEGS):
                rows_re.append(c_re)
                rows_im.append(c_im)
                c_re, c_im = (e_re[j:j + 1] + (p_re * c_re - p_im * c_im),
                              e_im[j:j + 1] + (p_re * c_im + p_im * c_re))
            carry_ref[:, re_l[idx]] = c_re
            carry_ref[:, im_l[idx]] = c_im
            starts.append((jnp.concatenate(rows_re, axis=0), jnp.concatenate(rows_im, axis=0)))
        scan(tuple(starts), True)

    for c in range(n_ch):
        h_re = jnp.concatenate([st_ref[c * per + k] for k in range(per)], axis=-1).astype(BF16)
        h_im = jnp.concatenate([st_ref[nt + c * per + k] for k in range(per)], axis=-1).astype(BF16)
        rows_re = slice(c * SSM_ST_CHUNK, (c + 1) * SSM_ST_CHUNK)
        rows_im = slice(n + c * SSM_ST_CHUNK, n + (c + 1) * SSM_ST_CHUNK)
        yp_ref[c] = _dot(h_re, wc_ref[rows_re, :]) + _dot(h_im, wc_ref[rows_im, :])

    def to_token_major(i, _):
        src = pl.ds(pl.multiple_of(i * SSM_SEGS, SSM_SEGS), SSM_SEGS)
        for k in range(n_ch):
            yt_ref[k, pl.ds(i, SSM_SEGS, stride=SSM_SEG_LEN), :] = yp_ref[k, src, :]
        return 0

    lax.fori_loop(0, SSM_SEG_LEN, to_token_major, 0)
    y = jnp.concatenate([yt_ref[k] for k in range(n_ch)], axis=-1)
    y = jax.nn.gelu(y + d_ref[...] * u)
    z = _dot(y.astype(BF16), wglu_ref[...])
    o_ref[...] = x + z[:, :d_model] * jax.nn.sigmoid(z[:, d_model:])


def _s5_layer(h, b, s, norm_g, w_in, a_re, a_im, log_dt, b_re, b_im, c_re, c_im, d_skip, w_glu):
    t, d = h.shape
    n = SSM_NSTATE
    w_bu, w_c, powers = _s5_prepare(a_re, a_im, log_dt, b_re, b_im, c_re, c_im)
    tiles = s // SSM_TILE
    resident = lambda shape: pl.BlockSpec(shape, lambda bi, ti: (0, 0), pipeline_mode=pl.Buffered(1))
    row = lambda bi, ti: (bi * tiles + ti, 0)
    ch_tiles = pltpu.VMEM((SSM_WIDTH // LANES, SSM_TILE, LANES), F32)
    return pl.pallas_call(
        _s5_kernel,
        grid=(b, tiles),
        in_specs=[pl.BlockSpec((SSM_TILE, d), row), resident((1, d)), resident((d, SSM_WIDTH)),
                  resident((SSM_CH_CHUNK, 2 * n)), resident((SSM_SEGS, 2 * n)), resident((2 * n, SSM_CH_CHUNK)),
                  resident((1, SSM_WIDTH)), resident((SSM_WIDTH, 2 * d))],
        out_specs=pl.BlockSpec((SSM_TILE, d), row),
        out_shape=jax.ShapeDtypeStruct((t, d), F32),
        scratch_shapes=[ch_tiles, ch_tiles, pltpu.VMEM((2 * n // LANES, SSM_TILE, LANES), F32),
                        pltpu.VMEM((1, 2 * n), F32), ch_tiles, ch_tiles],
        compiler_params=_params("arbitrary", "arbitrary"),
        name="s5_mixer",
    )(h, norm_g.reshape(1, d), w_in.astype(BF16), w_bu, powers, w_c, d_skip.reshape(1, -1), w_glu.astype(BF16))


MOE_BLOCK = 256
META_E1, META_E2, META_G1, META_G2, META_R1, META_R2 = range(6)


def _router_kernel(h_ref, g_ref, wr_ref, tri_ref, xn_ref, meta_ref, counts_ref, run_ref):
    @pl.when(pl.program_id(0) == 0)
    def _():
        run_ref[...] = jnp.zeros_like(run_ref)

    x = h_ref[...]
    xn = _rms(x, x.shape[-1]) * g_ref[...]
    xn_ref[...] = xn
    logits = jnp.dot(xn, wr_ref[...], preferred_element_type=F32, precision=lax.Precision.HIGHEST)
    tm = logits.shape[0]
    e = lax.broadcasted_iota(jnp.int32, (tm, N_EXPERTS), 1)
    v1 = jnp.max(logits, axis=-1, keepdims=True)
    i1 = jnp.min(jnp.where(logits == v1, e, N_EXPERTS), axis=-1, keepdims=True)
    rest = jnp.where(e == i1, -jnp.inf, logits)
    v2 = jnp.max(rest, axis=-1, keepdims=True)
    i2 = jnp.min(jnp.where(rest == v2, e, N_EXPERTS), axis=-1, keepdims=True)
    ex = jnp.exp(v2 - v1)
    g1 = 1.0 / (1.0 + ex)
    g2 = ex / (1.0 + ex)
    oh1 = (e == i1).astype(F32)
    oh2 = (e == i2).astype(F32)
    both = oh1 + oh2
    before = _dot(tri_ref[...], both.astype(BF16)) + run_ref[...]
    r1 = jnp.sum(oh1 * before, axis=-1, keepdims=True)
    r2 = jnp.sum(oh2 * before, axis=-1, keepdims=True)
    run_ref[...] += jnp.sum(both, axis=0, keepdims=True)
    counts_ref[...] = run_ref[...]
    meta = jnp.zeros((tm, N_EXPERTS), F32)
    for lane, val in ((META_E1, i1.astype(F32)), (META_E2, i2.astype(F32)), (META_G1, g1), (META_G2, g2),
                      (META_R1, r1), (META_R2, r2)):
        meta = jnp.where(e == lane, val, meta)
    meta_ref[...] = meta


def _router(h, norm_g, w_router, tm):
    t, d = h.shape
    row = lambda i: (i, 0)
    tri = (jnp.arange(tm)[:, None] > jnp.arange(tm)[None, :]).astype(BF16)
    return pl.pallas_call(
        _router_kernel,
        grid=(t // tm,),
        in_specs=[pl.BlockSpec((tm, d), row), _const_spec((1, d)), _const_spec((d, N_EXPERTS)),
                  _const_spec((tm, tm))],
        out_specs=[pl.BlockSpec((tm, d), row), pl.BlockSpec((tm, N_EXPERTS), row), _const_spec((1, N_EXPERTS))],
        out_shape=[jax.ShapeDtypeStruct((t, d), F32), jax.ShapeDtypeStruct((t, N_EXPERTS), F32),
                   jax.ShapeDtypeStruct((1, N_EXPERTS), F32)],
        scratch_shapes=[pltpu.VMEM((1, N_EXPERTS), F32)],
        compiler_params=_params("arbitrary"),
        name="moe_router",
    )(h, norm_g.reshape(1, d), w_router, tri)


def _row_copy(src_ref, src_row, dst_ref, dst_row, sem):
    return pltpu.make_async_copy(src_ref.at[pl.ds(src_row, 1)], dst_ref.at[pl.ds(dst_row, 1)], sem)


def _dispatch_kernel(d1_ref, d2_ref, xn_ref, xs_in_ref, xs_ref, sem, *, tm):
    del xs_in_ref
    base = pl.program_id(0) * tm

    def issue(r, _):
        tok = base + r
        _row_copy(xn_ref, r, xs_ref, d1_ref[tok], sem).start()
        _row_copy(xn_ref, r, xs_ref, d2_ref[tok], sem).start()
        return 0

    lax.fori_loop(0, tm, issue, 0)

    def drain(r, _):
        _row_copy(xn_ref, 0, xs_ref, 0, sem).wait()
        _row_copy(xn_ref, 0, xs_ref, 0, sem).wait()
        return 0

    lax.fori_loop(0, tm, drain, 0)


def _dispatch(xn, dest1, dest2, n_slots, tm):
    t, d = xn.shape
    any_spec = pl.BlockSpec(memory_space=pl.ANY)
    return pl.pallas_call(
        functools.partial(_dispatch_kernel, tm=tm),
        grid_spec=pltpu.PrefetchScalarGridSpec(
            num_scalar_prefetch=2, grid=(t // tm,),
            in_specs=[pl.BlockSpec((tm, d), lambda i, d1, d2: (i, 0)), any_spec], out_specs=any_spec,
            scratch_shapes=[pltpu.SemaphoreType.DMA(())]),
        out_shape=jax.ShapeDtypeStruct((n_slots, d), F32),
        input_output_aliases={3: 0},
        compiler_params=_params("arbitrary"),
        name="moe_dispatch",
    )(dest1, dest2, xn, jnp.zeros((n_slots, d), F32))


def _expert_ffn_kernel(be_ref, bv_ref, xs_ref, wg_ref, wu_ref, wd_ref, ys_ref, *, f_chunk):
    del be_ref
    valid = bv_ref[pl.program_id(0)] != 0

    @pl.when(valid)
    def _():
        ys_ref[...] = _swiglu(xs_ref[...].astype(BF16), wg_ref, wu_ref, wd_ref, f_chunk)

    @pl.when(jnp.logical_not(valid))
    def _():
        ys_ref[...] = jnp.zeros_like(ys_ref)


def _expert_ffn(xs, blk_expert, blk_valid, w_gate, w_up, w_down, f_chunk):
    n_slots, d = xs.shape
    d_ff = w_gate.shape[-1]
    row = lambda i, be, bv: (i, 0)
    return pl.pallas_call(
        functools.partial(_expert_ffn_kernel, f_chunk=f_chunk),
        grid_spec=pltpu.PrefetchScalarGridSpec(
            num_scalar_prefetch=2, grid=(n_slots // MOE_BLOCK,),
            in_specs=[pl.BlockSpec((MOE_BLOCK, d), row),
                      pl.BlockSpec((None, d, d_ff), lambda i, be, bv: (be[i], 0, 0)),
                      pl.BlockSpec((None, d, d_ff), lambda i, be, bv: (be[i], 0, 0)),
                      pl.BlockSpec((None, d_ff, d), lambda i, be, bv: (be[i], 0, 0))],
            out_specs=pl.BlockSpec((MOE_BLOCK, d), row)),
        out_shape=jax.ShapeDtypeStruct((n_slots, d), F32),
        compiler_params=_params("arbitrary"),
        name="expert_swiglu",
    )(blk_expert, blk_valid, xs, w_gate.astype(BF16), w_up.astype(BF16), w_down.astype(BF16))


def _combine_kernel(d1_ref, d2_ref, h_ref, meta_ref, ys_ref, o_ref, a_ref, b_ref, sem, *, tm):
    base = pl.program_id(0) * tm

    def issue(r, _):
        tok = base + r
        _row_copy(ys_ref, d1_ref[tok], a_ref, r, sem).start()
        _row_copy(ys_ref, d2_ref[tok], b_ref, r, sem).start()
        return 0

    lax.fori_loop(0, tm, issue, 0)

    def drain(r, _):
        _row_copy(ys_ref, 0, a_ref, 0, sem).wait()
        _row_copy(ys_ref, 0, b_ref, 0, sem).wait()
        return 0

    lax.fori_loop(0, tm, drain, 0)
    meta = meta_ref[...]
    g1 = meta[:, META_G1:META_G1 + 1]
    g2 = meta[:, META_G2:META_G2 + 1]
    o_ref[...] = h_ref[...] + (g1 * a_ref[...] + g2 * b_ref[...])


def _combine(h, meta, ys, dest1, dest2, tm):
    t, d = h.shape
    row = lambda i, d1, d2: (i, 0)
    return pl.pallas_call(
        functools.partial(_combine_kernel, tm=tm),
        grid_spec=pltpu.PrefetchScalarGridSpec(
            num_scalar_prefetch=2, grid=(t // tm,),
            in_specs=[pl.BlockSpec((tm, d), row), pl.BlockSpec((tm, N_EXPERTS), row),
                      pl.BlockSpec(memory_space=pl.ANY)],
            out_specs=pl.BlockSpec((tm, d), row),
            scratch_shapes=[pltpu.VMEM((tm, d), F32), pltpu.VMEM((tm, d), F32), pltpu.SemaphoreType.DMA(())]),
        out_shape=jax.ShapeDtypeStruct((t, d), F32),
        compiler_params=_params("arbitrary"),
        name="moe_combine",
    )(dest1, dest2, h, meta, ys)


def _moe_layer(h, norm_g, w_router, w_gate, w_up, w_down, tm, f_chunk):
    t, d = h.shape
    xn, meta, counts = _router(h, norm_g, w_router, tm)
    counts = counts.reshape(N_EXPERTS).astype(jnp.int32)
    padded = (counts + MOE_BLOCK - 1) // MOE_BLOCK * MOE_BLOCK
    pad_end = jnp.cumsum(padded)
    pad_start = pad_end - padded
    e1 = meta[:, META_E1].astype(jnp.int32)
    e2 = meta[:, META_E2].astype(jnp.int32)
    dest1 = pad_start[e1] + meta[:, META_R1].astype(jnp.int32)
    dest2 = pad_start[e2] + meta[:, META_R2].astype(jnp.int32)
    n_slots = 2 * t + N_EXPERTS * MOE_BLOCK
    blk_start = jnp.arange(n_slots // MOE_BLOCK, dtype=jnp.int32) * MOE_BLOCK
    blk_expert = jnp.minimum(jnp.sum(blk_start[:, None] >= pad_end[None, :], axis=1), N_EXPERTS - 1).astype(jnp.int32)
    blk_valid = (blk_start < pad_end[-1]).astype(jnp.int32)
    xs = _dispatch(xn, dest1, dest2, n_slots, tm)
    ys = _expert_ffn(xs, blk_expert, blk_valid, w_gate, w_up, w_down, f_chunk)
    return _combine(h, meta, ys, dest1, dest2, tm)


def kernel(x, att_norm, att_w_in, att_q_latent_norm, att_w_q_up, att_kv_latent_norm, att_w_kv_up, att_q_norm, att_k_norm, att_w_out, dffn_norm, dffn_w_gate, dffn_w_up, dffn_w_down, ssm_norm, ssm_w_in, ssm_a_re, ssm_a_im, ssm_log_dt, ssm_b_re, ssm_b_im, ssm_c_re, ssm_c_im, ssm_d, ssm_w_glu, moe_norm, moe_router, moe_w_gate, moe_w_up, moe_w_down):
    b, s, d = x.shape
    x2 = x.reshape(b * s, d)
    h = _layer0(x2, b, s, att_norm[0], att_w_in[0], att_q_latent_norm[0], att_w_q_up[0], att_kv_latent_norm[0],
                att_w_kv_up[0], att_q_norm[0], att_k_norm[0], att_w_out[0], dffn_norm[0], dffn_w_gate[0],
                dffn_w_up[0], dffn_w_down[0], tm=256, tq=512, tk=256, f_chunk=512)
    h = _s5_layer(h, b, s, ssm_norm[0], ssm_w_in[0], ssm_a_re[0], ssm_a_im[0], ssm_log_dt[0], ssm_b_re[0],
                  ssm_b_im[0], ssm_c_re[0], ssm_c_im[0], ssm_d[0], ssm_w_glu[0])
    h = _moe_layer(h, moe_norm[0], moe_router[0], moe_w_gate[0], moe_w_up[0], moe_w_down[0], tm=256, f_chunk=512)
    return h.reshape(b, s, d)
```

```python
import functools
import math

import jax
import jax.numpy as jnp
from jax import lax
from jax.experimental import pallas as pl
from jax.experimental.pallas import tpu as pltpu

F32 = jnp.float32
BF16 = jnp.bfloat16

EPS = 1e-6
LANES = 128
SB_HEADS = 8
SB_HEAD_DIM = 64
SB_WIDTH = SB_HEADS * SB_HEAD_DIM
MLA_HEADS = 8
MLA_Q_RANK = 256
MLA_KV_RANK = 128
MLA_NOPE_DIM = 64
MLA_ROPE_DIM = 32
MLA_QK_DIM = MLA_NOPE_DIM + MLA_ROPE_DIM
MLA_V_DIM = 64
MLA_WIDTH = MLA_HEADS * MLA_V_DIM
ROPE_THETA = 10000.0
SSM_GROUP = 16
SSM_GROUPS = 32
SSM_STATE = 64
SSM_WIDTH = SSM_GROUP * SSM_GROUPS
SSM_NSTATE = SSM_GROUPS * SSM_STATE
EIG_RE_MAX = -1e-4
N_EXPERTS = 8
VMEM_LIMIT = 56 * 1024 * 1024


def _params(*sem):
    return pltpu.CompilerParams(dimension_semantics=sem, vmem_limit_bytes=VMEM_LIMIT)


def _rms(x, n):
    return x * lax.rsqrt(jnp.sum(x * x, axis=-1, keepdims=True) * (1.0 / n) + EPS)


def _dot(a, b):
    return jnp.dot(a, b, preferred_element_type=F32)


def _dot_nt(a, b):
    return lax.dot_general(a, b, (((1,), (1,)), ((), ())), preferred_element_type=F32)


def _const_spec(shape):
    return pl.BlockSpec(shape, lambda *_: (0,) * len(shape))


IN_SB = 3 * SB_WIDTH
IN_CQ = IN_SB
IN_CKV = IN_CQ + MLA_Q_RANK
IN_KR = IN_CKV + MLA_KV_RANK
IN_KRS = IN_KR + LANES
IN_COLS_PAD = IN_KRS + LANES
MLA_PAD = MLA_HEADS * LANES


def _inproj_kernel(x_ref, g_ref, win_ref, qlg_ref, wq_ref, kvlg_ref, wkv_ref, ctab_ref, stab_ref,
                   qg_ref, qgs_ref, kg_ref, kgs_ref, qkv_ref, qm_ref, km_ref, vm_ref):
    x = x_ref[...]
    xn = _rms(x, x.shape[-1]) * g_ref[...]
    hh = _dot(xn.astype(BF16), win_ref[...])
    qkv_ref[:, :SB_WIDTH] = (hh[:, :SB_WIDTH] * (SB_HEAD_DIM ** -0.5)).astype(BF16)
    qkv_ref[:, SB_WIDTH:] = hh[:, SB_WIDTH:IN_SB].astype(BF16)
    cq = _rms(hh[:, IN_CQ:IN_CKV], MLA_Q_RANK) * qlg_ref[...]
    qf = _dot(cq.astype(BF16), wq_ref[...])
    ckv = _rms(hh[:, IN_CKV:IN_KR], MLA_KV_RANK) * kvlg_ref[...]
    kvf = _dot(ckv.astype(BF16), wkv_ref[...])
    vm_ref[...] = kvf[:, MLA_PAD:].astype(BF16)
    kr = hh[:, IN_KR:IN_KRS]
    krs = hh[:, IN_KRS:IN_COLS_PAD]
    ctab = ctab_ref[...]
    stab = stab_ref[...]
    scale = MLA_QK_DIM ** -0.5
    cq_t = ctab * (qg_ref[...] * scale)
    sq_t = stab * (qgs_ref[...] * scale)
    ck_t = ctab * kg_ref[...]
    sk_t = stab * kgs_ref[...]
    for h in range(MLA_HEADS):
        lo, hi = h * LANES, (h + 1) * LANES
        qh = qf[:, lo:hi]
        qs = qf[:, MLA_PAD + lo:MLA_PAD + hi]
        q_inv = lax.rsqrt(jnp.sum(qh * qh, axis=-1, keepdims=True) * (1.0 / MLA_QK_DIM) + EPS)
        qm_ref[:, lo:hi] = ((qh * cq_t + qs * sq_t) * q_inv).astype(BF16)
        kh = kvf[:, lo:hi] + kr
        k_inv = lax.rsqrt(jnp.sum(kh * kh, axis=-1, keepdims=True) * (1.0 / MLA_QK_DIM) + EPS)
        km_ref[:, lo:hi] = ((kh * ck_t + krs * sk_t) * k_inv).astype(BF16)


def _pad_heads(w, n_heads, width):
    k = w.shape[0]
    w = w.reshape(k, n_heads, width)
    return jnp.pad(w, ((0, 0), (0, 0), (0, LANES - width))).reshape(k, n_heads * LANES)


def _swap_rope(w):
    half = MLA_ROPE_DIM // 2
    return jnp.concatenate([jnp.zeros_like(w[..., :MLA_NOPE_DIM]),
                            w[..., MLA_NOPE_DIM + half:], w[..., MLA_NOPE_DIM:MLA_NOPE_DIM + half]], axis=-1)


def _rope_lane_tables(s_len):
    half = MLA_ROPE_DIM // 2
    inv_freq = ROPE_THETA ** (-jnp.arange(0, MLA_ROPE_DIM, 2, dtype=F32) / MLA_ROPE_DIM)
    ang = jnp.arange(s_len, dtype=F32)[:, None] * inv_freq[None, :]
    cos, sin = jnp.cos(ang), jnp.sin(ang)
    ones = jnp.ones((s_len, MLA_NOPE_DIM), F32)
    zeros_n = jnp.zeros((s_len, MLA_NOPE_DIM), F32)
    zeros_t = jnp.zeros((s_len, LANES - MLA_QK_DIM), F32)
    ctab = jnp.concatenate([ones, cos, cos, zeros_t], axis=-1)
    stab = jnp.concatenate([zeros_n, -sin, sin, zeros_t], axis=-1)
    del half
    return ctab, stab


def _in_projection(x2, s_len, norm_g, w_in, q_lat_g, w_q_up, kv_lat_g, w_kv_up, q_g, k_g, tm):
    t, d = x2.shape
    w_kr = w_in[:, IN_KR - LANES + 0:][:, :0]
    del w_kr
    off_kr = 3 * SB_WIDTH + MLA_Q_RANK + MLA_KV_RANK
    w_rope = w_in[:, off_kr:]
    zeros_n = jnp.zeros((d, MLA_NOPE_DIM), F32)
    zeros_t = jnp.zeros((d, LANES - MLA_QK_DIM), F32)
    half = MLA_ROPE_DIM // 2
    w_in_pad = jnp.concatenate(
        [w_in[:, :off_kr], zeros_n, w_rope, zeros_t,
         zeros_n, w_rope[:, half:], w_rope[:, :half], zeros_t], axis=-1).astype(BF16)
    wq3 = w_q_up.reshape(MLA_Q_RANK, MLA_HEADS, MLA_QK_DIM)
    wq_pad = jnp.concatenate(
        [_pad_heads(w_q_up, MLA_HEADS, MLA_QK_DIM),
         _pad_heads(_swap_rope(wq3).reshape(MLA_Q_RANK, -1), MLA_HEADS, MLA_QK_DIM)], axis=-1).astype(BF16)
    wkv3 = w_kv_up.reshape(MLA_KV_RANK, MLA_HEADS, MLA_NOPE_DIM + MLA_V_DIM)
    wkv_pad = jnp.concatenate(
        [_pad_heads(wkv3[..., :MLA_NOPE_DIM].reshape(MLA_KV_RANK, -1), MLA_HEADS, MLA_NOPE_DIM),
         wkv3[..., MLA_NOPE_DIM:].reshape(MLA_KV_RANK, -1)], axis=-1).astype(BF16)
    ctab, stab = _rope_lane_tables(s_len)
    pad_g = lambda g: jnp.pad(g, (0, LANES - MLA_QK_DIM)).reshape(1, LANES)
    swap_g = lambda g: jnp.pad(jnp.concatenate(
        [jnp.zeros((MLA_NOPE_DIM,), F32), g[MLA_NOPE_DIM + half:], g[MLA_NOPE_DIM:MLA_NOPE_DIM + half]]),
        (0, LANES - MLA_QK_DIM)).reshape(1, LANES)
    n_pos = s_len // tm
    row = lambda i: (i, 0)
    return pl.pallas_call(
        _inproj_kernel,
        grid=(t // tm,),
        in_specs=[
            pl.BlockSpec((tm, d), row),
            _const_spec((1, d)),
            _const_spec((d, IN_COLS_PAD)),
            _const_spec((1, MLA_Q_RANK)),
            _const_spec((MLA_Q_RANK, 2 * MLA_PAD)),
            _const_spec((1, MLA_KV_RANK)),
            _const_spec((MLA_KV_RANK, MLA_PAD + MLA_WIDTH)),
            pl.BlockSpec((tm, LANES), lambda i: (i % n_pos, 0)),
            pl.BlockSpec((tm, LANES), lambda i: (i % n_pos, 0)),
            _const_spec((1, LANES)), _const_spec((1, LANES)),
            _const_spec((1, LANES)), _const_spec((1, LANES)),
        ],
        out_specs=[
            pl.BlockSpec((tm, IN_SB), row),
            pl.BlockSpec((tm, MLA_PAD), row),
            pl.BlockSpec((tm, MLA_PAD), row),
            pl.BlockSpec((tm, MLA_WIDTH), row),
        ],
        out_shape=[
            jax.ShapeDtypeStruct((t, IN_SB), BF16),
            jax.ShapeDtypeStruct((t, MLA_PAD), BF16),
            jax.ShapeDtypeStruct((t, MLA_PAD), BF16),
            jax.ShapeDtypeStruct((t, MLA_WIDTH), BF16),
        ],
        compiler_params=_params("arbitrary"),
        name="in_projection",
    )(x2, norm_g.reshape(1, d), w_in_pad, q_lat_g.reshape(1, -1), wq_pad, kv_lat_g.reshape(1, -1), wkv_pad,
      ctab, stab, pad_g(q_g), swap_g(q_g), pad_g(k_g), swap_g(k_g))


SB_EXP_UNDERFLOW = 110.0


def _sb_kernel(q_ref, k_ref, v_ref, u_ref, o_ref, acc_ref, carry_ref, *, tq, tk):
    qi = pl.program_id(2)
    lane = lax.broadcasted_iota(jnp.int32, (1, LANES), 1)
    row = lax.broadcasted_iota(jnp.int32, (tq, tk), 0)
    col = lax.broadcasted_iota(jnp.int32, (tq, tk), 1)
    q = q_ref[...]
    q_heads = [jnp.where((lane >= j * SB_HEAD_DIM) & (lane < (j + 1) * SB_HEAD_DIM), q, jnp.zeros_like(q))
               for j in range(2)]
    acc_ref[...] = jnp.zeros_like(acc_ref)
    carry_ref[...] = jnp.zeros_like(carry_ref)

    def sweep(start, diagonal):
        start = pl.multiple_of(start, tk)
        kc = k_ref[pl.ds(start, tk), :]
        vc = v_ref[pl.ds(start, tk), :]
        if diagonal:
            earlier = (start + col) < (qi * tq + row)
        for j in range(2):
            z = _dot_nt(q_heads[j], kc)
            log_fail = -(jnp.maximum(z, 0.0) + jnp.log(1.0 + jnp.exp(-jnp.abs(z))))
            if diagonal:
                log_fail = jnp.where(earlier, log_fail, 0.0)
            hi = log_fail.astype(BF16)
            lo = (log_fail - hi.astype(F32)).astype(BF16)
            log_stick = _dot(hi, u_ref[...]) + _dot(lo, u_ref[...])
            carry = carry_ref[j]
            w = jnp.exp(z + log_fail + log_stick + jnp.tile(carry, (1, tk // LANES)))
            if diagonal:
                w = jnp.where(earlier, w, 0.0)
            acc_ref[j] += _dot(w.astype(BF16), vc)
            carry_ref[j] = carry + (log_stick[:, :1] + log_fail[:, :1])

    for sub in reversed(range(tq // tk)):
        sweep(qi * tq + sub * tk, True)

    def sticks_alive():
        return jnp.max(carry_ref[...]) > -SB_EXP_UNDERFLOW

    def earlier_chunk(state):
        c, _ = state
        sweep(c * tk, False)
        return c - 1, sticks_alive()

    lax.while_loop(lambda state: (state[0] >= 0) & state[1], earlier_chunk,
                   (qi * (tq // tk) - 1, sticks_alive()))
    o_ref[...] = jnp.where(lane < SB_HEAD_DIM, acc_ref[0], acc_ref[1]).astype(o_ref.dtype)


def _sb_attention(qkv, tq, tk):
    b, s, _ = qkv.shape
    n_pairs = SB_WIDTH // LANES
    tri = (jnp.arange(tk)[:, None] > jnp.arange(tk)[None, :]).astype(BF16)
    return pl.pallas_call(
        functools.partial(_sb_kernel, tq=tq, tk=tk),
        grid=(b, n_pairs, s // tq),
        in_specs=[
            pl.BlockSpec((None, tq, LANES), lambda bi, hp, qi: (bi, qi, hp)),
            pl.BlockSpec((None, s, LANES), lambda bi, hp, qi: (bi, 0, n_pairs + hp)),
            pl.BlockSpec((None, s, LANES), lambda bi, hp, qi: (bi, 0, 2 * n_pairs + hp)),
            _const_spec((tk, tk)),
        ],
        out_specs=pl.BlockSpec((None, tq, LANES), lambda bi, hp, qi: (bi, qi, hp)),
        out_shape=jax.ShapeDtypeStruct((b, s, SB_WIDTH), BF16),
        scratch_shapes=[pltpu.VMEM((2, tq, LANES), F32), pltpu.VMEM((2, tq, LANES), F32)],
        compiler_params=_params("arbitrary", "arbitrary", "arbitrary"),
        name="stick_breaking_attention",
    )(qkv, qkv, qkv, tri)


def _mla_kernel(q_ref, k_ref, v_ref, o_ref, acc_ref, m_ref, l_ref, *, tq):
    qi = pl.program_id(2)
    lane = lax.broadcasted_iota(jnp.int32, (1, LANES), 1)
    acc_ref[...] = jnp.zeros_like(acc_ref)
    l_ref[...] = jnp.zeros_like(l_ref)
    m_ref[...] = jnp.full(m_ref.shape, -jnp.inf, F32)

    def sweep(c, diagonal):
        start = pl.multiple_of(c * tq, tq)
        vc = v_ref[pl.ds(start, tq), :]
        for j in range(2):
            s = _dot_nt(q_ref[:, j * LANES:(j + 1) * LANES], k_ref[pl.ds(start, tq), j * LANES:(j + 1) * LANES])
            if diagonal:
                row = lax.broadcasted_iota(jnp.int32, (tq, tq), 0)
                col = lax.broadcasted_iota(jnp.int32, (tq, tq), 1)
                s = jnp.where(col <= row, s, -jnp.inf)
            m_prev = m_ref[j]
            m_new = jnp.maximum(m_prev, jnp.max(s, axis=-1, keepdims=True))
            alpha = jnp.exp(m_prev - m_new)
            p = jnp.exp(s - jnp.tile(m_new, (1, tq // LANES)))
            l_ref[j] = alpha * l_ref[j] + jnp.sum(p, axis=-1, keepdims=True)
            acc_ref[j] = alpha * acc_ref[j] + _dot(p.astype(BF16), vc)
            m_ref[j] = m_new

    def earlier_block(c, _):
        sweep(c, False)
        return 0

    lax.fori_loop(0, qi, earlier_block, 0)
    sweep(qi, True)
    o_ref[...] = jnp.where(lane < MLA_V_DIM, acc_ref[0] / l_ref[0], acc_ref[1] / l_ref[1]).astype(o_ref.dtype)


def _mla_attention(q, k, v, tq):
    b, s, _ = q.shape
    n_pairs = MLA_WIDTH // LANES
    return pl.pallas_call(
        functools.partial(_mla_kernel, tq=tq),
        grid=(b, n_pairs, s // tq),
        in_specs=[
            pl.BlockSpec((None, tq, 2 * LANES), lambda bi, hp, qi: (bi, qi, hp)),
            pl.BlockSpec((None, s, 2 * LANES), lambda bi, hp, qi: (bi, 0, hp)),
            pl.BlockSpec((None, s, LANES), lambda bi, hp, qi: (bi, 0, hp)),
        ],
        out_specs=pl.BlockSpec((None, tq, LANES), lambda bi, hp, qi: (bi, qi, hp)),
        out_shape=jax.ShapeDtypeStruct((b, s, MLA_WIDTH), BF16),
        scratch_shapes=[pltpu.VMEM((2, tq, LANES), F32)] * 3,
        compiler_params=_params("arbitrary", "arbitrary", "arbitrary"),
        name="latent_attention",
    )(q, k, v)


def _outproj_kernel(x_ref, osb_ref, omla_ref, wsb_ref, wmla_ref, g_ref, h_ref, hn_ref):
    h = x_ref[...] + _dot(osb_ref[...], wsb_ref[...]) + _dot(omla_ref[...], wmla_ref[...])
    h_ref[...] = h
    hn_ref[...] = (_rms(h, h.shape[-1]) * g_ref[...]).astype(BF16)


def _out_projection(x2, o_sb, o_mla, w_out, next_norm_g, tm):
    t, d = x2.shape
    row = lambda i: (i, 0)
    w_sb = w_out[:SB_WIDTH].astype(BF16)
    w_mla = w_out[SB_WIDTH:].astype(BF16)
    return pl.pallas_call(
        _outproj_kernel,
        grid=(t // tm,),
        in_specs=[pl.BlockSpec((tm, d), row), pl.BlockSpec((tm, SB_WIDTH), row), pl.BlockSpec((tm, MLA_WIDTH), row),
                  _const_spec((SB_WIDTH, d)), _const_spec((MLA_WIDTH, d)), _const_spec((1, d))],
        out_specs=[pl.BlockSpec((tm, d), row), pl.BlockSpec((tm, d), row)],
        out_shape=[jax.ShapeDtypeStruct((t, d), F32), jax.ShapeDtypeStruct((t, d), BF16)],
        compiler_params=_params("arbitrary"),
        name="out_projection",
    )(x2, o_sb, o_mla, w_sb, w_mla, next_norm_g.reshape(1, d))


def _swiglu(x, wg_ref, wu_ref, wd_ref, f_chunk):
    d_ff = wg_ref.shape[-1]
    acc = None
    for c in range(d_ff // f_chunk):
        cols = slice(c * f_chunk, (c + 1) * f_chunk)
        gate = _dot(x, wg_ref[:, cols])
        up = _dot(x, wu_ref[:, cols])
        act = (gate * jax.nn.sigmoid(gate) * up).astype(BF16)
        part = _dot(act, wd_ref[cols, :])
        acc = part if acc is None else acc + part
    return acc


def _dense_ffn_kernel(h_ref, hn_ref, wg_ref, wu_ref, wd_ref, o_ref, *, f_chunk):
    o_ref[...] = h_ref[...] + _swiglu(hn_ref[...], wg_ref, wu_ref, wd_ref, f_chunk)


def _dense_ffn(h, hn, w_gate, w_up, w_down, tm, f_chunk):
    t, d = h.shape
    d_ff = w_gate.shape[-1]
    row = lambda i: (i, 0)
    resident = lambda shape: pl.BlockSpec(shape, lambda i: (0, 0), pipeline_mode=pl.Buffered(1))
    return pl.pallas_call(
        functools.partial(_dense_ffn_kernel, f_chunk=f_chunk),
        grid=(t // tm,),
        in_specs=[pl.BlockSpec((tm, d), row), pl.BlockSpec((tm, d), row),
                  resident((d, d_ff)), resident((d, d_ff)), resident((d_ff, d))],
        out_specs=pl.BlockSpec((tm, d), row),
        out_shape=jax.ShapeDtypeStruct((t, d), F32),
        compiler_params=_params("arbitrary"),
        name="dense_swiglu",
    )(h, hn, w_gate.astype(BF16), w_up.astype(BF16), w_down.astype(BF16))


DENSE_FFN_TILE = 512


def _layer0(x2, b, s, att_norm, att_w_in, q_lat_g, w_q_up, kv_lat_g, w_kv_up, q_g, k_g, w_out,
            dffn_norm, w_gate, w_up, w_down, tm, tq, tk, f_chunk):
    qkv, qm, km, vm = _in_projection(x2, s, att_norm, att_w_in, q_lat_g, w_q_up, kv_lat_g, w_kv_up, q_g, k_g, tm)
    o_sb = _sb_attention(qkv.reshape(b, s, -1), tq, tk).reshape(b * s, -1)
    o_mla = _mla_attention(qm.reshape(b, s, -1), km.reshape(b, s, -1), vm.reshape(b, s, -1), tq).reshape(b * s, -1)
    h1, h1n = _out_projection(x2, o_sb, o_mla, w_out, dffn_norm, tm)
    return _dense_ffn(h1, h1n, w_gate, w_up, w_down, DENSE_FFN_TILE, f_chunk)


SSM_SEGS = 8
SSM_SEG_LEN = 64
SSM_TILE = SSM_SEGS * SSM_SEG_LEN
SSM_CH_CHUNK = LANES
SSM_ST_CHUNK = SSM_CH_CHUNK // SSM_GROUP * SSM_STATE
SSM_SCAN_TILES = 8


def _s5_prep_kernel(are_ref, aim_ref, logdt_ref, bre_ref, bim_ref, cre_ref, cim_ref,
                    wbu_ref, wc_ref, pow_ref):
    n = SSM_NSTATE
    a_re = jnp.minimum(are_ref[...], EIG_RE_MAX)
    a_im = aim_ref[...]
    dt = jnp.exp(logdt_ref[...])
    mag = jnp.exp(a_re * dt)
    lam_re = mag * jnp.cos(a_im * dt)
    lam_im = mag * jnp.sin(a_im * dt)
    den = a_re * a_re + a_im * a_im
    num_re = lam_re - 1.0
    coef_re = (num_re * a_re + lam_im * a_im) / den
    coef_im = (lam_im * a_re - num_re * a_im) / den
    b_re = bre_ref[...]
    b_im = bim_ref[...]
    in_group = (lax.broadcasted_iota(jnp.int32, (SSM_CH_CHUNK, n), 0) // SSM_GROUP
                == lax.broadcasted_iota(jnp.int32, (SSM_CH_CHUNK, n), 1) % SSM_ST_CHUNK // SSM_STATE)
    wbu_ref[:, :n] = jnp.where(in_group, coef_re * b_re - coef_im * b_im, 0.0).astype(BF16)
    wbu_ref[:, n:] = jnp.where(in_group, coef_re * b_im + coef_im * b_re, 0.0).astype(BF16)
    in_group_t = (lax.broadcasted_iota(jnp.int32, (n, SSM_CH_CHUNK), 0) % SSM_ST_CHUNK // SSM_STATE
                  == lax.broadcasted_iota(jnp.int32, (n, SSM_CH_CHUNK), 1) // SSM_GROUP)
    wc_ref[:n, :] = jnp.where(in_group_t, cre_ref[...], 0.0).astype(BF16)
    wc_ref[n:, :] = jnp.where(in_group_t, -cim_ref[...], 0.0).astype(BF16)
    steps = jnp.where(lax.broadcasted_iota(jnp.int32, (SSM_SEGS, n), 0) == 0, 1.0, float(SSM_SEG_LEN))
    mag_k = jnp.exp(steps * (a_re * dt))
    ang_k = steps * (a_im * dt)
    pow_ref[:, :n] = mag_k * jnp.cos(ang_k)
    pow_ref[:, n:] = mag_k * jnp.sin(ang_k)


def _s5_prepare(a_re, a_im, log_dt, b_re, b_im, c_re, c_im):
    n = SSM_NSTATE
    row = lambda a: a.reshape(1, n)
    reps = SSM_CH_CHUNK // SSM_GROUP
    b_t = lambda w: jnp.tile(w.transpose(2, 0, 1).reshape(SSM_GROUP, n), (reps, 1))
    c_t = lambda w: jnp.tile(w.transpose(0, 2, 1).reshape(n, SSM_GROUP), (1, reps))
    return pl.pallas_call(
        _s5_prep_kernel,
        out_shape=[jax.ShapeDtypeStruct((SSM_CH_CHUNK, 2 * n), BF16),
                   jax.ShapeDtypeStruct((2 * n, SSM_CH_CHUNK), BF16),
                   jax.ShapeDtypeStruct((SSM_SEGS, 2 * n), F32)],
        compiler_params=pltpu.CompilerParams(vmem_limit_bytes=VMEM_LIMIT),
        name="s5_discretize",
    )(row(a_re), row(a_im), row(jnp.repeat(log_dt, SSM_STATE)), b_t(b_re), b_t(b_im), c_t(c_re), c_t(c_im))


def _s5_kernel(h_ref, g_ref, win_ref, wbu_ref, pow_ref, wc_ref, d_ref, wglu_ref, o_ref,
               ut_ref, up_ref, st_ref, carry_ref, yp_ref, yt_ref):
    n = SSM_NSTATE
    nt = n // LANES
    n_ch = SSM_WIDTH // LANES
    per = SSM_ST_CHUNK // LANES
    d_model = h_ref.shape[-1]

    @pl.when(pl.program_id(1) == 0)
    def _():
        carry_ref[...] = jnp.zeros_like(carry_ref)

    x = h_ref[...]
    xn = _rms(x, d_model) * g_ref[...]
    u = _dot(xn.astype(BF16), win_ref[...])
    for k in range(n_ch):
        ut_ref[k] = u[:, k * LANES:(k + 1) * LANES]

    def to_step_major(i, _):
        dst = pl.ds(pl.multiple_of(i * SSM_SEGS, SSM_SEGS), SSM_SEGS)
        for k in range(n_ch):
            up_ref[k, dst, :] = ut_ref[k, pl.ds(i, SSM_SEGS, stride=SSM_SEG_LEN), :]
        return 0

    lax.fori_loop(0, SSM_SEG_LEN, to_step_major, 0)

    for c in range(n_ch):
        u16 = up_ref[c].astype(BF16)
        for part in range(2):
            cols = slice(part * n + c * SSM_ST_CHUNK, part * n + (c + 1) * SSM_ST_CHUNK)
            bu = _dot(u16, wbu_ref[:, cols])
            for k in range(per):
                st_ref[part * nt + c * per + k] = bu[:, k * LANES:(k + 1) * LANES]

    for grp in range(nt // SSM_SCAN_TILES):
        tiles = range(grp * SSM_SCAN_TILES, (grp + 1) * SSM_SCAN_TILES)
        re_l = [slice(k * LANES, (k + 1) * LANES) for k in tiles]
        im_l = [slice(n + k * LANES, n + (k + 1) * LANES) for k in tiles]
        lam_re = [jnp.broadcast_to(pow_ref[0:1, c], (SSM_SEGS, LANES)) for c in re_l]
        lam_im = [jnp.broadcast_to(pow_ref[0:1, c], (SSM_SEGS, LANES)) for c in im_l]

        def scan(state, store):
            def step(i, state):
                rows = pl.ds(pl.multiple_of(i * SSM_SEGS, SSM_SEGS), SSM_SEGS)
                new_state = []
                for idx, k in enumerate(tiles):
                    s_re, s_im = state[idx]
                    n_re = lam_re[idx] * s_re - lam_im[idx] * s_im + st_ref[k, rows, :]
                    n_im = lam_re[idx] * s_im + lam_im[idx] * s_re + st_ref[nt + k, rows, :]
                    if store:
                        st_ref[k, rows, :] = n_re
                        st_ref[nt + k, rows, :] = n_im
                    new_state.append((n_re, n_im))
                return tuple(new_state)

            return lax.fori_loop(0, SSM_SEG_LEN, step, state)

        zero = jnp.zeros((SSM_SEGS, LANES), F32)
        ends = scan(tuple((zero, zero) for _ in tiles), False)
        starts = []
        for idx in range(len(tiles)):
            e_re, e_im = ends[idx]
            p_re = pow_ref[1:2, re_l[idx]]
            p_im = pow_ref[1:2, im_l[idx]]
            c_re = carry_ref[:, re_l[idx]]
            c_im = carry_ref[:, im_l[idx]]
            rows_re, rows_im = [], []
            for j in range(SSM_SEGS):
                rows_re.append(c_re)
                rows_im.append(c_im)
                c_re, c_im = (e_re[j:j + 1] + (p_re * c_re - p_im * c_im),
                              e_im[j:j + 1] + (p_re * c_im + p_im * c_re))
            carry_ref[:, re_l[idx]] = c_re
            carry_ref[:, im_l[idx]] = c_im
            starts.append((jnp.concatenate(rows_re, axis=0), jnp.concatenate(rows_im, axis=0)))
        scan(tuple(starts), True)

    for c in range(n_ch):
        h_re = jnp.concatenate([st_ref[c * per + k] for k in range(per)], axis=-1).astype(BF16)
        h_im = jnp.concatenate([st_ref[nt + c * per + k] for k in range(per)], axis=-1).astype(BF16)
        rows_re = slice(c * SSM_ST_CHUNK, (c + 1) * SSM_ST_CHUNK)
        rows_im = slice(n + c * SSM_ST_CHUNK, n + (c + 1) * SSM_ST_CHUNK)
        yp_ref[c] = _dot(h_re, wc_ref[rows_re, :]) + _dot(h_im, wc_ref[rows_im, :])

    def to_token_major(i, _):
        src = pl.ds(pl.multiple_of(i * SSM_SEGS, SSM_SEGS), SSM_SEGS)
        for k in range(n_ch):
            yt_ref[k, pl.ds(i, SSM_SEGS, stride=SSM_SEG_LEN), :] = yp_ref[k, src, :]
        return 0

    lax.fori_loop(0, SSM_SEG_LEN, to_token_major, 0)
    y = jnp.concatenate([yt_ref[k] for k in range(n_ch)], axis=-1)
    y = jax.nn.gelu(y + d_ref[...] * u)
    z = _dot(y.astype(BF16), wglu_ref[...])
    o_ref[...] = x + z[:, :d_model] * jax.nn.sigmoid(z[:, d_model:])


def _s5_layer(h, b, s, norm_g, w_in, a_re, a_im, log_dt, b_re, b_im, c_re, c_im, d_skip, w_glu):
    t, d = h.shape
    n = SSM_NSTATE
    w_bu, w_c, powers = _s5_prepare(a_re, a_im, log_dt, b_re, b_im, c_re, c_im)
    tiles = s // SSM_TILE
    resident = lambda shape: pl.BlockSpec(shape, lambda bi, ti: (0, 0), pipeline_mode=pl.Buffered(1))
    row = lambda bi, ti: (bi * tiles + ti, 0)
    ch_tiles = pltpu.VMEM((SSM_WIDTH // LANES, SSM_TILE, LANES), F32)
    return pl.pallas_call(
        _s5_kernel,
        grid=(b, tiles),
        in_specs=[pl.BlockSpec((SSM_TILE, d), row), resident((1, d)), resident((d, SSM_WIDTH)),
                  resident((SSM_CH_CHUNK, 2 * n)), resident((SSM_SEGS, 2 * n)), resident((2 * n, SSM_CH_CHUNK)),
                  resident((1, SSM_WIDTH)), resident((SSM_WIDTH, 2 * d))],
        out_specs=pl.BlockSpec((SSM_TILE, d), row),
        out_shape=jax.ShapeDtypeStruct((t, d), F32),
        scratch_shapes=[ch_tiles, ch_tiles, pltpu.VMEM((2 * n // LANES, SSM_TILE, LANES), F32),
                        pltpu.VMEM((1, 2 * n), F32), ch_tiles, ch_tiles],
        compiler_params=_params("arbitrary", "arbitrary"),
        name="s5_mixer",
    )(h, norm_g.reshape(1, d), w_in.astype(BF16), w_bu, powers, w_c, d_skip.reshape(1, -1), w_glu.astype(BF16))


MOE_BLOCK = 256
META_E1, META_E2, META_G1, META_G2, META_R1, META_R2 = range(6)


def _router_kernel(h_ref, g_ref, wr_ref, tri_ref, xn_ref, meta_ref, counts_ref, run_ref):
    @pl.when(pl.program_id(0) == 0)
    def _():
        run_ref[...] = jnp.zeros_like(run_ref)

    x = h_ref[...]
    xn = _rms(x, x.shape[-1]) * g_ref[...]
    xn_ref[...] = xn
    logits = jnp.dot(xn, wr_ref[...], preferred_element_type=F32, precision=lax.Precision.HIGHEST)
    tm = logits.shape[0]
    e = lax.broadcasted_iota(jnp.int32, (tm, N_EXPERTS), 1)
    v1 = jnp.max(logits, axis=-1, keepdims=True)
    i1 = jnp.min(jnp.where(logits == v1, e, N_EXPERTS), axis=-1, keepdims=True)
    rest = jnp.where(e == i1, -jnp.inf, logits)
    v2 = jnp.max(rest, axis=-1, keepdims=True)
    i2 = jnp.min(jnp.where(rest == v2, e, N_EXPERTS), axis=-1, keepdims=True)
    ex = jnp.exp(v2 - v1)
    g1 = 1.0 / (1.0 + ex)
    g2 = ex / (1.0 + ex)
    oh1 = (e == i1).astype(F32)
    oh2 = (e == i2).astype(F32)
    both = oh1 + oh2
    before = _dot(tri_ref[...], both.astype(BF16)) + run_ref[...]
    r1 = jnp.sum(oh1 * before, axis=-1, keepdims=True)
    r2 = jnp.sum(oh2 * before, axis=-1, keepdims=True)
    run_ref[...] += jnp.sum(both, axis=0, keepdims=True)
    counts_ref[...] = run_ref[...]
    meta = jnp.zeros((tm, N_EXPERTS), F32)
    for lane, val in ((META_E1, i1.astype(F32)), (META_E2, i2.astype(F32)), (META_G1, g1), (META_G2, g2),
                      (META_R1, r1), (META_R2, r2)):
        meta = jnp.where(e == lane, val, meta)
    meta_ref[...] = meta


def _router(h, norm_g, w_router, tm):
    t, d = h.shape
    row = lambda i: (i, 0)
    tri = (jnp.arange(tm)[:, None] > jnp.arange(tm)[None, :]).astype(BF16)
    return pl.pallas_call(
        _router_kernel,
        grid=(t // tm,),
        in_specs=[pl.BlockSpec((tm, d), row), _const_spec((1, d)), _const_spec((d, N_EXPERTS)),
                  _const_spec((tm, tm))],
        out_specs=[pl.BlockSpec((tm, d), row), pl.BlockSpec((tm, N_EXPERTS), row), _const_spec((1, N_EXPERTS))],
        out_shape=[jax.ShapeDtypeStruct((t, d), F32), jax.ShapeDtypeStruct((t, N_EXPERTS), F32),
                   jax.ShapeDtypeStruct((1, N_EXPERTS), F32)],
        scratch_shapes=[pltpu.VMEM((1, N_EXPERTS), F32)],
        compiler_params=_params("arbitrary"),
        name="moe_router",
    )(h, norm_g.reshape(1, d), w_router, tri)


def _row_copy(src_ref, src_row, dst_ref, dst_row, sem):
    return pltpu.make_async_copy(src_ref.at[pl.ds(src_row, 1)], dst_ref.at[pl.ds(dst_row, 1)], sem)


def _expert_ffn_kernel(be_ref, bv_ref, tok_ref, xn_ref, wg_ref, wu_ref, wd_ref, ys_ref, buf_ref, sem, *, f_chunk):
    del be_ref
    i = pl.program_id(0)
    last = pl.num_programs(0) - 1
    cur = i % 2

    def gather(block, buf):
        base = block * MOE_BLOCK
        for r in range(MOE_BLOCK):
            _row_copy(xn_ref, tok_ref[base + r], buf_ref.at[buf], r, sem.at[buf]).start()

    def wait_gather(buf):
        pltpu.make_async_copy(xn_ref.at[pl.ds(0, MOE_BLOCK)], buf_ref.at[buf], sem.at[buf]).wait()

    @pl.when(i == 0)
    def _():
        gather(0, 0)

    gather(jnp.minimum(i + 1, last), 1 - cur)
    wait_gather(cur)
    valid = bv_ref[i] != 0

    @pl.when(valid)
    def _():
        ys_ref[...] = _swiglu(buf_ref[cur].astype(BF16), wg_ref, wu_ref, wd_ref, f_chunk)

    @pl.when(jnp.logical_not(valid))
    def _():
        ys_ref[...] = jnp.zeros_like(ys_ref)

    @pl.when(i == last)
    def _():
        wait_gather(1 - cur)


def _expert_ffn(xn, slot_tok, blk_expert, blk_valid, w_gate, w_up, w_down, f_chunk):
    n_slots = slot_tok.shape[0]
    d = xn.shape[-1]
    d_ff = w_gate.shape[-1]
    return pl.pallas_call(
        functools.partial(_expert_ffn_kernel, f_chunk=f_chunk),
        grid_spec=pltpu.PrefetchScalarGridSpec(
            num_scalar_prefetch=3, grid=(n_slots // MOE_BLOCK,),
            in_specs=[pl.BlockSpec(memory_space=pl.ANY),
                      pl.BlockSpec((None, d, d_ff), lambda i, be, bv, st: (be[i], 0, 0)),
                      pl.BlockSpec((None, d, d_ff), lambda i, be, bv, st: (be[i], 0, 0)),
                      pl.BlockSpec((None, d_ff, d), lambda i, be, bv, st: (be[i], 0, 0))],
            out_specs=pl.BlockSpec((MOE_BLOCK, d), lambda i, be, bv, st: (i, 0)),
            scratch_shapes=[pltpu.VMEM((2, MOE_BLOCK, d), F32), pltpu.SemaphoreType.DMA((2,))]),
        out_shape=jax.ShapeDtypeStruct((n_slots, d), F32),
        compiler_params=_params("arbitrary"),
        name="expert_swiglu",
    )(blk_expert, blk_valid, slot_tok, xn, w_gate.astype(BF16), w_up.astype(BF16), w_down.astype(BF16))


def _combine_kernel(d1_ref, d2_ref, h_ref, meta_ref, ys_ref, o_ref, a_ref, b_ref, sem, *, tm):
    base = pl.program_id(0) * tm

    def issue(r, _):
        tok = base + r
        _row_copy(ys_ref, d1_ref[tok], a_ref, r, sem).start()
        _row_copy(ys_ref, d2_ref[tok], b_ref, r, sem).start()
        return 0

    lax.fori_loop(0, tm, issue, 0)

    pltpu.make_async_copy(ys_ref.at[pl.ds(0, tm)], a_ref, sem).wait()
    pltpu.make_async_copy(ys_ref.at[pl.ds(0, tm)], b_ref, sem).wait()
    meta = meta_ref[...]
    g1 = meta[:, META_G1:META_G1 + 1]
    g2 = meta[:, META_G2:META_G2 + 1]
    o_ref[...] = h_ref[...] + (g1 * a_ref[...] + g2 * b_ref[...])


def _combine(h, meta, ys, dest1, dest2, tm):
    t, d = h.shape
    row = lambda i, d1, d2: (i, 0)
    return pl.pallas_call(
        functools.partial(_combine_kernel, tm=tm),
        grid_spec=pltpu.PrefetchScalarGridSpec(
            num_scalar_prefetch=2, grid=(t // tm,),
            in_specs=[pl.BlockSpec((tm, d), row), pl.BlockSpec((tm, N_EXPERTS), row),
                      pl.BlockSpec(memory_space=pl.ANY)],
            out_specs=pl.BlockSpec((tm, d), row),
            scratch_shapes=[pltpu.VMEM((tm, d), F32), pltpu.VMEM((tm, d), F32), pltpu.SemaphoreType.DMA(())]),
        out_shape=jax.ShapeDtypeStruct((t, d), F32),
        compiler_params=_params("arbitrary"),
        name="moe_combine",
    )(dest1, dest2, h, meta, ys)


def _moe_layer(h, norm_g, w_router, w_gate, w_up, w_down, tm, f_chunk):
    t, d = h.shape
    xn, meta, counts = _router(h, norm_g, w_router, tm)
    counts = counts.reshape(N_EXPERTS).astype(jnp.int32)
    padded = (counts + MOE_BLOCK - 1) // MOE_BLOCK * MOE_BLOCK
    pad_end = jnp.cumsum(padded)
    pad_start = pad_end - padded
    e1 = meta[:, META_E1].astype(jnp.int32)
    e2 = meta[:, META_E2].astype(jnp.int32)
    dest1 = pad_start[e1] + meta[:, META_R1].astype(jnp.int32)
    dest2 = pad_start[e2] + meta[:, META_R2].astype(jnp.int32)
    n_slots = 2 * t + N_EXPERTS * MOE_BLOCK
    blk_start = jnp.arange(n_slots // MOE_BLOCK, dtype=jnp.int32) * MOE_BLOCK
    blk_expert = jnp.minimum(jnp.sum(blk_start[:, None] >= pad_end[None, :], axis=1), N_EXPERTS - 1).astype(jnp.int32)
    blk_valid = (blk_start < pad_end[-1]).astype(jnp.int32)
    tok = jnp.arange(t, dtype=jnp.int32)
    slot_tok = jnp.zeros((n_slots,), jnp.int32).at[dest1].set(tok).at[dest2].set(tok)
    ys = _expert_ffn(xn, slot_tok, blk_expert, blk_valid, w_gate, w_up, w_down, f_chunk)
    return _combine(h, meta, ys, dest1, dest2, tm)


def kernel(x, att_norm, att_w_in, att_q_latent_norm, att_w_q_up, att_kv_latent_norm, att_w_kv_up, att_q_norm, att_k_norm, att_w_out, dffn_norm, dffn_w_gate, dffn_w_up, dffn_w_down, ssm_norm, ssm_w_in, ssm_a_re, ssm_a_im, ssm_log_dt, ssm_b_re, ssm_b_im, ssm_c_re, ssm_c_im, ssm_d, ssm_w_glu, moe_norm, moe_router, moe_w_gate, moe_w_up, moe_w_down):
    b, s, d = x.shape
    x2 = x.reshape(b * s, d)
    h = _layer0(x2, b, s, att_norm[0], att_w_in[0], att_q_latent_norm[0], att_w_q_up[0], att_kv_latent_norm[0],
                att_w_kv_up[0], att_q_norm[0], att_k_norm[0], att_w_out[0], dffn_norm[0], dffn_w_gate[0],
                dffn_w_up[0], dffn_w_down[0], tm=256, tq=512, tk=256, f_chunk=512)
    h = _s5_layer(h, b, s, ssm_norm[0], ssm_w_in[0], ssm_a_re[0], ssm_a_im[0], ssm_log_dt[0], ssm_b_re[0],
                  ssm_b_im[0], ssm_c_re[0], ssm_c_im[0], ssm_d[0], ssm_w_glu[0])
    h = _moe_layer(h, moe_norm[0], moe_router[0], moe_w_gate[0], moe_w_up[0], moe_w_down[0], tm=256, f_chunk=512)
    return h.reshape(b, s, d)
```

```python
import functools
import math

import jax
import jax.numpy as jnp
from jax import lax
from jax.experimental import pallas as pl
from jax.experimental.pallas import tpu as pltpu

F32 = jnp.float32
BF16 = jnp.bfloat16

EPS = 1e-6
LANES = 128
SB_HEADS = 8
SB_HEAD_DIM = 64
SB_WIDTH = SB_HEADS * SB_HEAD_DIM
MLA_HEADS = 8
MLA_Q_RANK = 256
MLA_KV_RANK = 128
MLA_NOPE_DIM = 64
MLA_ROPE_DIM = 32
MLA_QK_DIM = MLA_NOPE_DIM + MLA_ROPE_DIM
MLA_V_DIM = 64
MLA_WIDTH = MLA_HEADS * MLA_V_DIM
ROPE_THETA = 10000.0
SSM_GROUP = 16
SSM_GROUPS = 32
SSM_STATE = 64
SSM_WIDTH = SSM_GROUP * SSM_GROUPS
SSM_NSTATE = SSM_GROUPS * SSM_STATE
EIG_RE_MAX = -1e-4
N_EXPERTS = 8
VMEM_LIMIT = 56 * 1024 * 1024


def _params(*sem):
    return pltpu.CompilerParams(dimension_semantics=sem, vmem_limit_bytes=VMEM_LIMIT)


def _rms(x, n):
    return x * lax.rsqrt(jnp.sum(x * x, axis=-1, keepdims=True) * (1.0 / n) + EPS)


def _dot(a, b):
    return jnp.dot(a, b, preferred_element_type=F32)


def _dot_nt(a, b):
    return lax.dot_general(a, b, (((1,), (1,)), ((), ())), preferred_element_type=F32)


def _const_spec(shape):
    return pl.BlockSpec(shape, lambda *_: (0,) * len(shape))


IN_SB = 3 * SB_WIDTH
IN_CQ = IN_SB
IN_CKV = IN_CQ + MLA_Q_RANK
IN_KR = IN_CKV + MLA_KV_RANK
IN_KRS = IN_KR + LANES
IN_COLS_PAD = IN_KRS + LANES
MLA_PAD = MLA_HEADS * LANES


def _inproj_kernel(x_ref, g_ref, win_ref, qlg_ref, wq_ref, kvlg_ref, wkv_ref, ctab_ref, stab_ref,
                   qg_ref, qgs_ref, kg_ref, kgs_ref, qkv_ref, qm_ref, km_ref, vm_ref):
    x = x_ref[...]
    xn = _rms(x, x.shape[-1]) * g_ref[...]
    hh = _dot(xn.astype(BF16), win_ref[...])
    qkv_ref[:, :SB_WIDTH] = (hh[:, :SB_WIDTH] * (SB_HEAD_DIM ** -0.5)).astype(BF16)
    qkv_ref[:, SB_WIDTH:] = hh[:, SB_WIDTH:IN_SB].astype(BF16)
    cq = _rms(hh[:, IN_CQ:IN_CKV], MLA_Q_RANK) * qlg_ref[...]
    qf = _dot(cq.astype(BF16), wq_ref[...])
    ckv = _rms(hh[:, IN_CKV:IN_KR], MLA_KV_RANK) * kvlg_ref[...]
    kvf = _dot(ckv.astype(BF16), wkv_ref[...])
    vm_ref[...] = kvf[:, MLA_PAD:].T.astype(BF16)
    kr = hh[:, IN_KR:IN_KRS]
    krs = hh[:, IN_KRS:IN_COLS_PAD]
    ctab = ctab_ref[...]
    stab = stab_ref[...]
    scale = MLA_QK_DIM ** -0.5
    cq_t = ctab * (qg_ref[...] * scale)
    sq_t = stab * (qgs_ref[...] * scale)
    ck_t = ctab * kg_ref[...]
    sk_t = stab * kgs_ref[...]
    for h in range(MLA_HEADS):
        lo, hi = h * LANES, (h + 1) * LANES
        qh = qf[:, lo:hi]
        qs = qf[:, MLA_PAD + lo:MLA_PAD + hi]
        q_inv = lax.rsqrt(jnp.sum(qh * qh, axis=-1, keepdims=True) * (1.0 / MLA_QK_DIM) + EPS)
        qm_ref[:, lo:hi] = ((qh * cq_t + qs * sq_t) * q_inv).astype(BF16)
        kh = kvf[:, lo:hi] + kr
        k_inv = lax.rsqrt(jnp.sum(kh * kh, axis=-1, keepdims=True) * (1.0 / MLA_QK_DIM) + EPS)
        km_ref[:, lo:hi] = ((kh * ck_t + krs * sk_t) * k_inv).astype(BF16)


def _pad_heads(w, n_heads, width):
    k = w.shape[0]
    w = w.reshape(k, n_heads, width)
    return jnp.pad(w, ((0, 0), (0, 0), (0, LANES - width))).reshape(k, n_heads * LANES)


def _swap_rope(w):
    half = MLA_ROPE_DIM // 2
    return jnp.concatenate([jnp.zeros_like(w[..., :MLA_NOPE_DIM]),
                            w[..., MLA_NOPE_DIM + half:], w[..., MLA_NOPE_DIM:MLA_NOPE_DIM + half]], axis=-1)


def _rope_lane_tables(s_len):
    half = MLA_ROPE_DIM // 2
    inv_freq = ROPE_THETA ** (-jnp.arange(0, MLA_ROPE_DIM, 2, dtype=F32) / MLA_ROPE_DIM)
    ang = jnp.arange(s_len, dtype=F32)[:, None] * inv_freq[None, :]
    cos, sin = jnp.cos(ang), jnp.sin(ang)
    ones = jnp.ones((s_len, MLA_NOPE_DIM), F32)
    zeros_n = jnp.zeros((s_len, MLA_NOPE_DIM), F32)
    zeros_t = jnp.zeros((s_len, LANES - MLA_QK_DIM), F32)
    ctab = jnp.concatenate([ones, cos, cos, zeros_t], axis=-1)
    stab = jnp.concatenate([zeros_n, -sin, sin, zeros_t], axis=-1)
    del half
    return ctab, stab


def _in_projection(x2, s_len, norm_g, w_in, q_lat_g, w_q_up, kv_lat_g, w_kv_up, q_g, k_g, tm):
    t, d = x2.shape
    w_kr = w_in[:, IN_KR - LANES + 0:][:, :0]
    del w_kr
    off_kr = 3 * SB_WIDTH + MLA_Q_RANK + MLA_KV_RANK
    w_rope = w_in[:, off_kr:]
    zeros_n = jnp.zeros((d, MLA_NOPE_DIM), F32)
    zeros_t = jnp.zeros((d, LANES - MLA_QK_DIM), F32)
    half = MLA_ROPE_DIM // 2
    w_in_pad = jnp.concatenate(
        [w_in[:, :off_kr], zeros_n, w_rope, zeros_t,
         zeros_n, w_rope[:, half:], w_rope[:, :half], zeros_t], axis=-1).astype(BF16)
    wq3 = w_q_up.reshape(MLA_Q_RANK, MLA_HEADS, MLA_QK_DIM)
    wq_pad = jnp.concatenate(
        [_pad_heads(w_q_up, MLA_HEADS, MLA_QK_DIM),
         _pad_heads(_swap_rope(wq3).reshape(MLA_Q_RANK, -1), MLA_HEADS, MLA_QK_DIM)], axis=-1).astype(BF16)
    wkv3 = w_kv_up.reshape(MLA_KV_RANK, MLA_HEADS, MLA_NOPE_DIM + MLA_V_DIM)
    wkv_pad = jnp.concatenate(
        [_pad_heads(wkv3[..., :MLA_NOPE_DIM].reshape(MLA_KV_RANK, -1), MLA_HEADS, MLA_NOPE_DIM),
         wkv3[..., MLA_NOPE_DIM:].reshape(MLA_KV_RANK, -1)], axis=-1).astype(BF16)
    ctab, stab = _rope_lane_tables(s_len)
    pad_g = lambda g: jnp.pad(g, (0, LANES - MLA_QK_DIM)).reshape(1, LANES)
    swap_g = lambda g: jnp.pad(jnp.concatenate(
        [jnp.zeros((MLA_NOPE_DIM,), F32), g[MLA_NOPE_DIM + half:], g[MLA_NOPE_DIM:MLA_NOPE_DIM + half]]),
        (0, LANES - MLA_QK_DIM)).reshape(1, LANES)
    n_pos = s_len // tm
    row = lambda i: (i, 0)
    return pl.pallas_call(
        _inproj_kernel,
        grid=(t // tm,),
        in_specs=[
            pl.BlockSpec((tm, d), row),
            _const_spec((1, d)),
            _const_spec((d, IN_COLS_PAD)),
            _const_spec((1, MLA_Q_RANK)),
            _const_spec((MLA_Q_RANK, 2 * MLA_PAD)),
            _const_spec((1, MLA_KV_RANK)),
            _const_spec((MLA_KV_RANK, MLA_PAD + MLA_WIDTH)),
            pl.BlockSpec((tm, LANES), lambda i: (i % n_pos, 0)),
            pl.BlockSpec((tm, LANES), lambda i: (i % n_pos, 0)),
            _const_spec((1, LANES)), _const_spec((1, LANES)),
            _const_spec((1, LANES)), _const_spec((1, LANES)),
        ],
        out_specs=[
            pl.BlockSpec((tm, IN_SB), row),
            pl.BlockSpec((tm, MLA_PAD), row),
            pl.BlockSpec((tm, MLA_PAD), row),
            pl.BlockSpec((MLA_WIDTH, tm), lambda i: (0, i)),
        ],
        out_shape=[
            jax.ShapeDtypeStruct((t, IN_SB), BF16),
            jax.ShapeDtypeStruct((t, MLA_PAD), BF16),
            jax.ShapeDtypeStruct((t, MLA_PAD), BF16),
            jax.ShapeDtypeStruct((MLA_WIDTH, t), BF16),
        ],
        compiler_params=_params("arbitrary"),
        name="in_projection",
    )(x2, norm_g.reshape(1, d), w_in_pad, q_lat_g.reshape(1, -1), wq_pad, kv_lat_g.reshape(1, -1), wkv_pad,
      ctab, stab, pad_g(q_g), swap_g(q_g), pad_g(k_g), swap_g(k_g))


SB_EXP_UNDERFLOW = 110.0


def _sb_kernel(q_ref, k_ref, v_ref, u_ref, o_ref, acc_ref, carry_ref, z_ref, *, tq, tk):
    qi = pl.program_id(2)
    lane = lax.broadcasted_iota(jnp.int32, (1, LANES), 1)
    row = lax.broadcasted_iota(jnp.int32, (tq, tk), 0)
    col = lax.broadcasted_iota(jnp.int32, (tq, tk), 1)
    q = q_ref[...]
    q_heads = [jnp.where((lane >= j * SB_HEAD_DIM) & (lane < (j + 1) * SB_HEAD_DIM), q, jnp.zeros_like(q))
               for j in range(2)]
    acc_ref[...] = jnp.zeros_like(acc_ref)
    carry_ref[...] = jnp.zeros_like(carry_ref)

    def logits(j, c):
        start = pl.multiple_of(c * tk, tk)
        z_ref[j] = _dot_nt(q_heads[j], k_ref[pl.ds(start, tk), :])

    def absorb(j, c, diagonal):
        start = pl.multiple_of(c * tk, tk)
        z = z_ref[j]
        log_fail = -(jnp.maximum(z, 0.0) + jnp.log(1.0 + jnp.exp(-jnp.abs(z))))
        if diagonal:
            earlier = (start + col) < (qi * tq + row)
            log_fail = jnp.where(earlier, log_fail, 0.0)
        hi = log_fail.astype(BF16)
        lo = (log_fail - hi.astype(F32)).astype(BF16)
        log_stick = _dot(hi, u_ref[...]) + _dot(lo, u_ref[...])
        carry = carry_ref[j]
        w = jnp.exp(z + log_fail + log_stick + jnp.tile(carry, (1, tk // LANES)))
        if diagonal:
            w = jnp.where(earlier, w, 0.0)
        acc_ref[j] += _dot(w.astype(BF16), v_ref[pl.ds(start, tk), :])
        carry_ref[j] = carry + (log_stick[:, :1] + log_fail[:, :1])

    def sweep(c, diagonal):
        logits(1, c)
        absorb(0, c, diagonal)
        logits(0, jnp.maximum(c - 1, 0))
        absorb(1, c, diagonal)

    first = qi * (tq // tk) + tq // tk - 1
    logits(0, first)
    for sub in range(tq // tk):
        sweep(first - sub, True)

    def sticks_alive():
        return jnp.max(carry_ref[...]) > -SB_EXP_UNDERFLOW

    def earlier_chunk(state):
        c, _ = state
        sweep(c, False)
        return c - 1, sticks_alive()

    lax.while_loop(lambda state: (state[0] >= 0) & state[1], earlier_chunk,
                   (qi * (tq // tk) - 1, sticks_alive()))
    o_ref[...] = jnp.where(lane < SB_HEAD_DIM, acc_ref[0], acc_ref[1]).astype(o_ref.dtype)


def _sb_attention(qkv, tq, tk):
    b, s, _ = qkv.shape
    n_pairs = SB_WIDTH // LANES
    tri = (jnp.arange(tk)[:, None] > jnp.arange(tk)[None, :]).astype(BF16)
    return pl.pallas_call(
        functools.partial(_sb_kernel, tq=tq, tk=tk),
        grid=(b, n_pairs, s // tq),
        in_specs=[
            pl.BlockSpec((None, tq, LANES), lambda bi, hp, qi: (bi, qi, hp)),
            pl.BlockSpec((None, s, LANES), lambda bi, hp, qi: (bi, 0, n_pairs + hp)),
            pl.BlockSpec((None, s, LANES), lambda bi, hp, qi: (bi, 0, 2 * n_pairs + hp)),
            _const_spec((tk, tk)),
        ],
        out_specs=pl.BlockSpec((None, tq, LANES), lambda bi, hp, qi: (bi, qi, hp)),
        out_shape=jax.ShapeDtypeStruct((b, s, SB_WIDTH), BF16),
        scratch_shapes=[pltpu.VMEM((2, tq, LANES), F32), pltpu.VMEM((2, tq, LANES), F32),
                        pltpu.VMEM((2, tq, tk), F32)],
        compiler_params=_params("arbitrary", "arbitrary", "arbitrary"),
        name="stick_breaking_attention",
    )(qkv, qkv, qkv, tri)


def _mla_kernel(q_ref, k_ref, vt_ref, o_ref, acc_ref, m_ref, l_ref, s_ref, *, tq):
    qi = pl.program_id(2)
    acc_ref[...] = jnp.zeros_like(acc_ref)
    l_ref[...] = jnp.zeros_like(l_ref)
    m_ref[...] = jnp.full(m_ref.shape, -jnp.inf, F32)

    def scores(j, c):
        start = pl.multiple_of(c * tq, tq)
        s_ref[j] = _dot_nt(k_ref[pl.ds(start, tq), j * LANES:(j + 1) * LANES],
                           q_ref[:, j * LANES:(j + 1) * LANES])

    def update(j, c, diagonal):
        start = pl.multiple_of(c * tq, tq)
        s = s_ref[j]
        if diagonal:
            key = lax.broadcasted_iota(jnp.int32, (tq, tq), 0)
            qry = lax.broadcasted_iota(jnp.int32, (tq, tq), 1)
            s = jnp.where(key <= qry, s, -jnp.inf)
        m_prev = m_ref[j]
        m_new = jnp.maximum(m_prev, jnp.max(s, axis=0, keepdims=True))
        alpha = jnp.exp(m_prev - m_new)
        p = jnp.exp(s - m_new)
        l_ref[j] = alpha * l_ref[j] + jnp.sum(p, axis=0, keepdims=True)
        vt = vt_ref[j * MLA_V_DIM:(j + 1) * MLA_V_DIM, pl.ds(start, tq)]
        acc_ref[j] = alpha * acc_ref[j] + _dot(vt, p.astype(BF16))
        m_ref[j] = m_new

    scores(0, 0)

    def earlier_block(c, _):
        scores(1, c)
        update(0, c, False)
        scores(0, c + 1)
        update(1, c, False)
        return 0

    lax.fori_loop(0, qi, earlier_block, 0)
    scores(1, qi)
    update(0, qi, True)
    update(1, qi, True)
    out_t = jnp.concatenate([acc_ref[0] / l_ref[0], acc_ref[1] / l_ref[1]], axis=0)
    o_ref[...] = out_t.T.astype(o_ref.dtype)


def _mla_attention(q, k, vt, tq):
    b, s, _ = q.shape
    n_pairs = MLA_WIDTH // LANES
    return pl.pallas_call(
        functools.partial(_mla_kernel, tq=tq),
        grid=(b, n_pairs, s // tq),
        in_specs=[
            pl.BlockSpec((None, tq, 2 * LANES), lambda bi, hp, qi: (bi, qi, hp)),
            pl.BlockSpec((None, s, 2 * LANES), lambda bi, hp, qi: (bi, 0, hp)),
            pl.BlockSpec((LANES, s), lambda bi, hp, qi: (hp, bi)),
        ],
        out_specs=pl.BlockSpec((None, tq, LANES), lambda bi, hp, qi: (bi, qi, hp)),
        out_shape=jax.ShapeDtypeStruct((b, s, MLA_WIDTH), BF16),
        scratch_shapes=[pltpu.VMEM((2, MLA_V_DIM, tq), F32), pltpu.VMEM((2, 1, tq), F32),
                        pltpu.VMEM((2, 1, tq), F32), pltpu.VMEM((2, tq, tq), F32)],
        compiler_params=_params("arbitrary", "arbitrary", "arbitrary"),
        name="latent_attention",
    )(q, k, vt)


def _outproj_kernel(x_ref, osb_ref, omla_ref, wsb_ref, wmla_ref, g_ref, h_ref, hn_ref):
    h = x_ref[...] + _dot(osb_ref[...], wsb_ref[...]) + _dot(omla_ref[...], wmla_ref[...])
    h_ref[...] = h
    hn_ref[...] = (_rms(h, h.shape[-1]) * g_ref[...]).astype(BF16)


def _out_projection(x2, o_sb, o_mla, w_out, next_norm_g, tm):
    t, d = x2.shape
    row = lambda i: (i, 0)
    w_sb = w_out[:SB_WIDTH].astype(BF16)
    w_mla = w_out[SB_WIDTH:].astype(BF16)
    return pl.pallas_call(
        _outproj_kernel,
        grid=(t // tm,),
        in_specs=[pl.BlockSpec((tm, d), row), pl.BlockSpec((tm, SB_WIDTH), row), pl.BlockSpec((tm, MLA_WIDTH), row),
                  _const_spec((SB_WIDTH, d)), _const_spec((MLA_WIDTH, d)), _const_spec((1, d))],
        out_specs=[pl.BlockSpec((tm, d), row), pl.BlockSpec((tm, d), row)],
        out_shape=[jax.ShapeDtypeStruct((t, d), F32), jax.ShapeDtypeStruct((t, d), BF16)],
        compiler_params=_params("arbitrary"),
        name="out_projection",
    )(x2, o_sb, o_mla, w_sb, w_mla, next_norm_g.reshape(1, d))


def _swiglu(x, wg_ref, wu_ref, wd_ref, f_chunk):
    d_ff = wg_ref.shape[-1]
    acc = None
    for c in range(d_ff // f_chunk):
        cols = slice(c * f_chunk, (c + 1) * f_chunk)
        gate = _dot(x, wg_ref[:, cols])
        up = _dot(x, wu_ref[:, cols])
        act = (gate * jax.nn.sigmoid(gate) * up).astype(BF16)
        part = _dot(act, wd_ref[cols, :])
        acc = part if acc is None else acc + part
    return acc


def _dense_ffn_kernel(h_ref, hn_ref, wg_ref, wu_ref, wd_ref, o_ref, *, f_chunk):
    o_ref[...] = h_ref[...] + _swiglu(hn_ref[...], wg_ref, wu_ref, wd_ref, f_chunk)


def _dense_ffn(h, hn, w_gate, w_up, w_down, tm, f_chunk):
    t, d = h.shape
    d_ff = w_gate.shape[-1]
    row = lambda i: (i, 0)
    resident = lambda shape: pl.BlockSpec(shape, lambda i: (0, 0), pipeline_mode=pl.Buffered(1))
    return pl.pallas_call(
        functools.partial(_dense_ffn_kernel, f_chunk=f_chunk),
        grid=(t // tm,),
        in_specs=[pl.BlockSpec((tm, d), row), pl.BlockSpec((tm, d), row),
                  resident((d, d_ff)), resident((d, d_ff)), resident((d_ff, d))],
        out_specs=pl.BlockSpec((tm, d), row),
        out_shape=jax.ShapeDtypeStruct((t, d), F32),
        compiler_params=_params("arbitrary"),
        name="dense_swiglu",
    )(h, hn, w_gate.astype(BF16), w_up.astype(BF16), w_down.astype(BF16))


DENSE_FFN_TILE = 512


def _layer0(x2, b, s, att_norm, att_w_in, q_lat_g, w_q_up, kv_lat_g, w_kv_up, q_g, k_g, w_out,
            dffn_norm, w_gate, w_up, w_down, tm, tq, tk, f_chunk):
    qkv, qm, km, vm = _in_projection(x2, s, att_norm, att_w_in, q_lat_g, w_q_up, kv_lat_g, w_kv_up, q_g, k_g, tm)
    o_sb = _sb_attention(qkv.reshape(b, s, -1), tq, tk).reshape(b * s, -1)
    o_mla = _mla_attention(qm.reshape(b, s, -1), km.reshape(b, s, -1), vm, tq).reshape(b * s, -1)
    h1, h1n = _out_projection(x2, o_sb, o_mla, w_out, dffn_norm, tm)
    return _dense_ffn(h1, h1n, w_gate, w_up, w_down, DENSE_FFN_TILE, f_chunk)


SSM_SEGS = 8
SSM_SEG_LEN = 64
SSM_TILE = SSM_SEGS * SSM_SEG_LEN
SSM_CH_CHUNK = LANES
SSM_ST_CHUNK = SSM_CH_CHUNK // SSM_GROUP * SSM_STATE
SSM_SCAN_TILES = 8


def _s5_prep_kernel(are_ref, aim_ref, logdt_ref, bre_ref, bim_ref, cre_ref, cim_ref,
                    wbu_ref, wc_ref, pow_ref):
    n = SSM_NSTATE
    a_re = jnp.minimum(are_ref[...], EIG_RE_MAX)
    a_im = aim_ref[...]
    dt = jnp.exp(logdt_ref[...])
    mag = jnp.exp(a_re * dt)
    lam_re = mag * jnp.cos(a_im * dt)
    lam_im = mag * jnp.sin(a_im * dt)
    den = a_re * a_re + a_im * a_im
    num_re = lam_re - 1.0
    coef_re = (num_re * a_re + lam_im * a_im) / den
    coef_im = (lam_im * a_re - num_re * a_im) / den
    b_re = bre_ref[...]
    b_im = bim_ref[...]
    in_group = (lax.broadcasted_iota(jnp.int32, (SSM_CH_CHUNK, n), 0) // SSM_GROUP
                == lax.broadcasted_iota(jnp.int32, (SSM_CH_CHUNK, n), 1) % SSM_ST_CHUNK // SSM_STATE)
    wbu_ref[:, :n] = jnp.where(in_group, coef_re * b_re - coef_im * b_im, 0.0).astype(BF16)
    wbu_ref[:, n:] = jnp.where(in_group, coef_re * b_im + coef_im * b_re, 0.0).astype(BF16)
    in_group_t = (lax.broadcasted_iota(jnp.int32, (n, SSM_CH_CHUNK), 0) % SSM_ST_CHUNK // SSM_STATE
                  == lax.broadcasted_iota(jnp.int32, (n, SSM_CH_CHUNK), 1) // SSM_GROUP)
    wc_ref[:n, :] = jnp.where(in_group_t, cre_ref[...], 0.0).astype(BF16)
    wc_ref[n:, :] = jnp.where(in_group_t, -cim_ref[...], 0.0).astype(BF16)
    steps = jnp.where(lax.broadcasted_iota(jnp.int32, (SSM_SEGS, n), 0) == 0, 1.0, float(SSM_SEG_LEN))
    mag_k = jnp.exp(steps * (a_re * dt))
    ang_k = steps * (a_im * dt)
    pow_ref[:, :n] = mag_k * jnp.cos(ang_k)
    pow_ref[:, n:] = mag_k * jnp.sin(ang_k)


def _s5_prepare(a_re, a_im, log_dt, b_re, b_im, c_re, c_im):
    n = SSM_NSTATE
    row = lambda a: a.reshape(1, n)
    reps = SSM_CH_CHUNK // SSM_GROUP
    b_t = lambda w: jnp.tile(w.transpose(2, 0, 1).reshape(SSM_GROUP, n), (reps, 1))
    c_t = lambda w: jnp.tile(w.transpose(0, 2, 1).reshape(n, SSM_GROUP), (1, reps))
    return pl.pallas_call(
        _s5_prep_kernel,
        out_shape=[jax.ShapeDtypeStruct((SSM_CH_CHUNK, 2 * n), BF16),
                   jax.ShapeDtypeStruct((2 * n, SSM_CH_CHUNK), BF16),
                   jax.ShapeDtypeStruct((SSM_SEGS, 2 * n), F32)],
        compiler_params=pltpu.CompilerParams(vmem_limit_bytes=VMEM_LIMIT),
        name="s5_discretize",
    )(row(a_re), row(a_im), row(jnp.repeat(log_dt, SSM_STATE)), b_t(b_re), b_t(b_im), c_t(c_re), c_t(c_im))


def _s5_kernel(h_ref, g_ref, win_ref, wbu_ref, pow_ref, wc_ref, d_ref, wglu_ref, o_ref,
               ut_ref, up_ref, st_ref, carry_ref, yp_ref, yt_ref):
    n = SSM_NSTATE
    nt = n // LANES
    n_ch = SSM_WIDTH // LANES
    per = SSM_ST_CHUNK // LANES
    d_model = h_ref.shape[-1]

    @pl.when(pl.program_id(1) == 0)
    def _():
        carry_ref[...] = jnp.zeros_like(carry_ref)

    x = h_ref[...]
    xn = _rms(x, d_model) * g_ref[...]
    u = _dot(xn.astype(BF16), win_ref[...])
    for k in range(n_ch):
        ut_ref[k] = u[:, k * LANES:(k + 1) * LANES]

    def to_step_major(i, _):
        dst = pl.ds(pl.multiple_of(i * SSM_SEGS, SSM_SEGS), SSM_SEGS)
        for k in range(n_ch):
            up_ref[k, dst, :] = ut_ref[k, pl.ds(i, SSM_SEGS, stride=SSM_SEG_LEN), :]
        return 0

    lax.fori_loop(0, SSM_SEG_LEN, to_step_major, 0)

    for c in range(n_ch):
        u16 = up_ref[c].astype(BF16)
        for part in range(2):
            cols = slice(part * n + c * SSM_ST_CHUNK, part * n + (c + 1) * SSM_ST_CHUNK)
            bu = _dot(u16, wbu_ref[:, cols])
            for k in range(per):
                st_ref[part * nt + c * per + k] = bu[:, k * LANES:(k + 1) * LANES]

    for grp in range(nt // SSM_SCAN_TILES):
        tiles = range(grp * SSM_SCAN_TILES, (grp + 1) * SSM_SCAN_TILES)
        re_l = [slice(k * LANES, (k + 1) * LANES) for k in tiles]
        im_l = [slice(n + k * LANES, n + (k + 1) * LANES) for k in tiles]
        lam_re = [jnp.broadcast_to(pow_ref[0:1, c], (SSM_SEGS, LANES)) for c in re_l]
        lam_im = [jnp.broadcast_to(pow_ref[0:1, c], (SSM_SEGS, LANES)) for c in im_l]

        def scan(state, store):
            def step(i, state):
                rows = pl.ds(pl.multiple_of(i * SSM_SEGS, SSM_SEGS), SSM_SEGS)
                new_state = []
                for idx, k in enumerate(tiles):
                    s_re, s_im = state[idx]
                    n_re = lam_re[idx] * s_re - lam_im[idx] * s_im + st_ref[k, rows, :]
                    n_im = lam_re[idx] * s_im + lam_im[idx] * s_re + st_ref[nt + k, rows, :]
                    if store:
                        st_ref[k, rows, :] = n_re
                        st_ref[nt + k, rows, :] = n_im
                    new_state.append((n_re, n_im))
                return tuple(new_state)

            return lax.fori_loop(0, SSM_SEG_LEN, step, state)

        zero = jnp.zeros((SSM_SEGS, LANES), F32)
        ends = scan(tuple((zero, zero) for _ in tiles), False)
        starts = []
        for idx in range(len(tiles)):
            e_re, e_im = ends[idx]
            p_re = pow_ref[1:2, re_l[idx]]
            p_im = pow_ref[1:2, im_l[idx]]
            c_re = carry_ref[:, re_l[idx]]
            c_im = carry_ref[:, im_l[idx]]
            rows_re, rows_im = [], []
            for j in range(SSM_SEGS):
                rows_re.append(c_re)
                rows_im.append(c_im)
                c_re, c_im = (e_re[j:j + 1] + (p_re * c_re - p_im * c_im),
                              e_im[j:j + 1] + (p_re * c_im + p_im * c_re))
            carry_ref[:, re_l[idx]] = c_re
            carry_ref[:, im_l[idx]] = c_im
            starts.append((jnp.concatenate(rows_re, axis=0), jnp.concatenate(rows_im, axis=0)))
        scan(tuple(starts), True)

    for c in range(n_ch):
        h_re = jnp.concatenate([st_ref[c * per + k] for k in range(per)], axis=-1).astype(BF16)
        h_im = jnp.concatenate([st_ref[nt + c * per + k] for k in range(per)], axis=-1).astype(BF16)
        rows_re = slice(c * SSM_ST_CHUNK, (c + 1) * SSM_ST_CHUNK)
        rows_im = slice(n + c * SSM_ST_CHUNK, n + (c + 1) * SSM_ST_CHUNK)
        yp_ref[c] = _dot(h_re, wc_ref[rows_re, :]) + _dot(h_im, wc_ref[rows_im, :])

    def to_token_major(i, _):
        src = pl.ds(pl.multiple_of(i * SSM_SEGS, SSM_SEGS), SSM_SEGS)
        for k in range(n_ch):
            yt_ref[k, pl.ds(i, SSM_SEGS, stride=SSM_SEG_LEN), :] = yp_ref[k, src, :]
        return 0

    lax.fori_loop(0, SSM_SEG_LEN, to_token_major, 0)
    y = jnp.concatenate([yt_ref[k] for k in range(n_ch)], axis=-1)
    y = jax.nn.gelu(y + d_ref[...] * u)
    z = _dot(y.astype(BF16), wglu_ref[...])
    o_ref[...] = x + z[:, :d_model] * jax.nn.sigmoid(z[:, d_model:])


def _s5_layer(h, b, s, norm_g, w_in, a_re, a_im, log_dt, b_re, b_im, c_re, c_im, d_skip, w_glu):
    t, d = h.shape
    n = SSM_NSTATE
    w_bu, w_c, powers = _s5_prepare(a_re, a_im, log_dt, b_re, b_im, c_re, c_im)
    tiles = s // SSM_TILE
    resident = lambda shape: pl.BlockSpec(shape, lambda bi, ti: (0, 0), pipeline_mode=pl.Buffered(1))
    row = lambda bi, ti: (bi * tiles + ti, 0)
    ch_tiles = pltpu.VMEM((SSM_WIDTH // LANES, SSM_TILE, LANES), F32)
    return pl.pallas_call(
        _s5_kernel,
        grid=(b, tiles),
        in_specs=[pl.BlockSpec((SSM_TILE, d), row), resident((1, d)), resident((d, SSM_WIDTH)),
                  resident((SSM_CH_CHUNK, 2 * n)), resident((SSM_SEGS, 2 * n)), resident((2 * n, SSM_CH_CHUNK)),
                  resident((1, SSM_WIDTH)), resident((SSM_WIDTH, 2 * d))],
        out_specs=pl.BlockSpec((SSM_TILE, d), row),
        out_shape=jax.ShapeDtypeStruct((t, d), F32),
        scratch_shapes=[ch_tiles, ch_tiles, pltpu.VMEM((2 * n // LANES, SSM_TILE, LANES), F32),
                        pltpu.VMEM((1, 2 * n), F32), ch_tiles, ch_tiles],
        compiler_params=_params("arbitrary", "arbitrary"),
        name="s5_mixer",
    )(h, norm_g.reshape(1, d), w_in.astype(BF16), w_bu, powers, w_c, d_skip.reshape(1, -1), w_glu.astype(BF16))


MOE_BLOCK = 256
META_E1, META_E2, META_G1, META_G2, META_R1, META_R2 = range(6)


def _router_kernel(h_ref, g_ref, wr_ref, tri_ref, xn_ref, meta_ref, counts_ref, run_ref):
    @pl.when(pl.program_id(0) == 0)
    def _():
        run_ref[...] = jnp.zeros_like(run_ref)

    x = h_ref[...]
    xn = _rms(x, x.shape[-1]) * g_ref[...]
    xn_ref[...] = xn
    logits = jnp.dot(xn, wr_ref[...], preferred_element_type=F32, precision=lax.Precision.HIGHEST)
    tm = logits.shape[0]
    e = lax.broadcasted_iota(jnp.int32, (tm, N_EXPERTS), 1)
    v1 = jnp.max(logits, axis=-1, keepdims=True)
    i1 = jnp.min(jnp.where(logits == v1, e, N_EXPERTS), axis=-1, keepdims=True)
    rest = jnp.where(e == i1, -jnp.inf, logits)
    v2 = jnp.max(rest, axis=-1, keepdims=True)
    i2 = jnp.min(jnp.where(rest == v2, e, N_EXPERTS), axis=-1, keepdims=True)
    ex = jnp.exp(v2 - v1)
    g1 = 1.0 / (1.0 + ex)
    g2 = ex / (1.0 + ex)
    oh1 = (e == i1).astype(F32)
    oh2 = (e == i2).astype(F32)
    both = oh1 + oh2
    before = _dot(tri_ref[...], both.astype(BF16)) + run_ref[...]
    r1 = jnp.sum(oh1 * before, axis=-1, keepdims=True)
    r2 = jnp.sum(oh2 * before, axis=-1, keepdims=True)
    run_ref[...] += jnp.sum(both, axis=0, keepdims=True)
    counts_ref[...] = run_ref[...]
    meta = jnp.zeros((tm, N_EXPERTS), F32)
    for lane, val in ((META_E1, i1.astype(F32)), (META_E2, i2.astype(F32)), (META_G1, g1), (META_G2, g2),
                      (META_R1, r1), (META_R2, r2)):
        meta = jnp.where(e == lane, val, meta)
    meta_ref[...] = meta


def _router(h, norm_g, w_router, tm):
    t, d = h.shape
    row = lambda i: (i, 0)
    tri = (jnp.arange(tm)[:, None] > jnp.arange(tm)[None, :]).astype(BF16)
    return pl.pallas_call(
        _router_kernel,
        grid=(t // tm,),
        in_specs=[pl.BlockSpec((tm, d), row), _const_spec((1, d)), _const_spec((d, N_EXPERTS)),
                  _const_spec((tm, tm))],
        out_specs=[pl.BlockSpec((tm, d), row), pl.BlockSpec((tm, N_EXPERTS), row), _const_spec((1, N_EXPERTS))],
        out_shape=[jax.ShapeDtypeStruct((t, d), F32), jax.ShapeDtypeStruct((t, N_EXPERTS), F32),
                   jax.ShapeDtypeStruct((1, N_EXPERTS), F32)],
        scratch_shapes=[pltpu.VMEM((1, N_EXPERTS), F32)],
        compiler_params=_params("arbitrary"),
        name="moe_router",
    )(h, norm_g.reshape(1, d), w_router, tri)


def _row_copy(src_ref, src_row, dst_ref, dst_row, sem):
    return pltpu.make_async_copy(src_ref.at[pl.ds(src_row, 1)], dst_ref.at[pl.ds(dst_row, 1)], sem)


def _expert_ffn_kernel(be_ref, bv_ref, tok_ref, xn_ref, wg_ref, wu_ref, wd_ref, ys_ref, buf_ref, sem, *, f_chunk):
    del be_ref
    i = pl.program_id(0)
    last = pl.num_programs(0) - 1
    cur = i % 2

    def gather(block, buf):
        base = block * MOE_BLOCK
        for r in range(MOE_BLOCK):
            _row_copy(xn_ref, tok_ref[base + r], buf_ref.at[buf], r, sem.at[buf]).start()

    def wait_gather(buf):
        pltpu.make_async_copy(xn_ref.at[pl.ds(0, MOE_BLOCK)], buf_ref.at[buf], sem.at[buf]).wait()

    @pl.when(i == 0)
    def _():
        gather(0, 0)

    valid = bv_ref[i] != 0
    prev_valid = (i == 0) | (bv_ref[jnp.maximum(i - 1, 0)] != 0)

    @pl.when(valid)
    def _():
        wait_gather(cur)
        x = buf_ref[cur].astype(BF16)
        gather(jnp.minimum(i + 1, last), 1 - cur)
        ys_ref[...] = _swiglu(x, wg_ref, wu_ref, wd_ref, f_chunk)

    @pl.when(jnp.logical_not(valid))
    def _():
        @pl.when(prev_valid)
        def _():
            wait_gather(cur)

        ys_ref[...] = jnp.zeros_like(ys_ref)

    @pl.when(valid & (i == last))
    def _():
        wait_gather(1 - cur)


def _expert_ffn(xn, slot_tok, blk_expert, blk_valid, w_gate, w_up, w_down, f_chunk):
    n_slots = slot_tok.shape[0]
    d = xn.shape[-1]
    d_ff = w_gate.shape[-1]
    return pl.pallas_call(
        functools.partial(_expert_ffn_kernel, f_chunk=f_chunk),
        grid_spec=pltpu.PrefetchScalarGridSpec(
            num_scalar_prefetch=3, grid=(n_slots // MOE_BLOCK,),
            in_specs=[pl.BlockSpec(memory_space=pl.ANY),
                      pl.BlockSpec((None, d, d_ff), lambda i, be, bv, st: (be[i], 0, 0)),
                      pl.BlockSpec((None, d, d_ff), lambda i, be, bv, st: (be[i], 0, 0)),
                      pl.BlockSpec((None, d_ff, d), lambda i, be, bv, st: (be[i], 0, 0))],
            out_specs=pl.BlockSpec((MOE_BLOCK, d), lambda i, be, bv, st: (i, 0)),
            scratch_shapes=[pltpu.VMEM((2, MOE_BLOCK, d), F32), pltpu.SemaphoreType.DMA((2,))]),
        out_shape=jax.ShapeDtypeStruct((n_slots, d), F32),
        compiler_params=_params("arbitrary"),
        name="expert_swiglu",
    )(blk_expert, blk_valid, slot_tok, xn, w_gate.astype(BF16), w_up.astype(BF16), w_down.astype(BF16))


def _combine_kernel(d1_ref, d2_ref, h_ref, meta_ref, ys_ref, o_ref, a_ref, b_ref, sem, *, tm):
    base = pl.program_id(0) * tm

    def issue(r, _):
        tok = base + r
        _row_copy(ys_ref, d1_ref[tok], a_ref, r, sem).start()
        _row_copy(ys_ref, d2_ref[tok], b_ref, r, sem).start()
        return 0

    lax.fori_loop(0, tm, issue, 0, unroll=8)

    pltpu.make_async_copy(ys_ref.at[pl.ds(0, tm)], a_ref, sem).wait()
    pltpu.make_async_copy(ys_ref.at[pl.ds(0, tm)], b_ref, sem).wait()
    meta = meta_ref[...]
    g1 = meta[:, META_G1:META_G1 + 1]
    g2 = meta[:, META_G2:META_G2 + 1]
    o_ref[...] = h_ref[...] + (g1 * a_ref[...] + g2 * b_ref[...])


def _combine(h, meta, ys, dest1, dest2, tm):
    t, d = h.shape
    row = lambda i, d1, d2: (i, 0)
    return pl.pallas_call(
        functools.partial(_combine_kernel, tm=tm),
        grid_spec=pltpu.PrefetchScalarGridSpec(
            num_scalar_prefetch=2, grid=(t // tm,),
            in_specs=[pl.BlockSpec((tm, d), row), pl.BlockSpec((tm, N_EXPERTS), row),
                      pl.BlockSpec(memory_space=pl.ANY)],
            out_specs=pl.BlockSpec((tm, d), row),
            scratch_shapes=[pltpu.VMEM((tm, d), F32), pltpu.VMEM((tm, d), F32), pltpu.SemaphoreType.DMA(())]),
        out_shape=jax.ShapeDtypeStruct((t, d), F32),
        compiler_params=_params("arbitrary"),
        name="moe_combine",
    )(dest1, dest2, h, meta, ys)


def _moe_layer(h, norm_g, w_router, w_gate, w_up, w_down, tm, f_chunk):
    t, d = h.shape
    xn, meta, counts = _router(h, norm_g, w_router, tm)
    counts = counts.reshape(N_EXPERTS).astype(jnp.int32)
    padded = (counts + MOE_BLOCK - 1) // MOE_BLOCK * MOE_BLOCK
    pad_end = jnp.cumsum(padded)
    pad_start = pad_end - padded
    e1 = meta[:, META_E1].astype(jnp.int32)
    e2 = meta[:, META_E2].astype(jnp.int32)
    dest1 = pad_start[e1] + meta[:, META_R1].astype(jnp.int32)
    dest2 = pad_start[e2] + meta[:, META_R2].astype(jnp.int32)
    n_slots = 2 * t + N_EXPERTS * MOE_BLOCK
    blk_start = jnp.arange(n_slots // MOE_BLOCK, dtype=jnp.int32) * MOE_BLOCK
    blk_expert = jnp.minimum(jnp.sum(blk_start[:, None] >= pad_end[None, :], axis=1), N_EXPERTS - 1).astype(jnp.int32)
    blk_valid = (blk_start < pad_end[-1]).astype(jnp.int32)
    tok = jnp.arange(t, dtype=jnp.int32)
    slot_tok = jnp.zeros((n_slots,), jnp.int32).at[jnp.concatenate([dest1, dest2])].set(
        jnp.concatenate([tok, tok]), unique_indices=True)
    ys = _expert_ffn(xn, slot_tok, blk_expert, blk_valid, w_gate, w_up, w_down, f_chunk)
    return _combine(h, meta, ys, dest1, dest2, tm)


def kernel(x, att_norm, att_w_in, att_q_latent_norm, att_w_q_up, att_kv_latent_norm, att_w_kv_up, att_q_norm, att_k_norm, att_w_out, dffn_norm, dffn_w_gate, dffn_w_up, dffn_w_down, ssm_norm, ssm_w_in, ssm_a_re, ssm_a_im, ssm_log_dt, ssm_b_re, ssm_b_im, ssm_c_re, ssm_c_im, ssm_d, ssm_w_glu, moe_norm, moe_router, moe_w_gate, moe_w_up, moe_w_down):
    b, s, d = x.shape
    x2 = x.reshape(b * s, d)
    h = _layer0(x2, b, s, att_norm[0], att_w_in[0], att_q_latent_norm[0], att_w_q_up[0], att_kv_latent_norm[0],
                att_w_kv_up[0], att_q_norm[0], att_k_norm[0], att_w_out[0], dffn_norm[0], dffn_w_gate[0],
                dffn_w_up[0], dffn_w_down[0], tm=256, tq=512, tk=256, f_chunk=512)
    h = _s5_layer(h, b, s, ssm_norm[0], ssm_w_in[0], ssm_a_re[0], ssm_a_im[0], ssm_log_dt[0], ssm_b_re[0],
                  ssm_b_im[0], ssm_c_re[0], ssm_c_im[0], ssm_d[0], ssm_w_glu[0])
    h = _moe_layer(h, moe_norm[0], moe_router[0], moe_w_gate[0], moe_w_up[0], moe_w_down[0], tm=256, f_chunk=512)
    return h.reshape(b, s, d)
```

```python
import functools
import math

import jax
import jax.numpy as jnp
from jax import lax
from jax.experimental import pallas as pl
from jax.experimental.pallas import tpu as pltpu

F32 = jnp.float32
BF16 = jnp.bfloat16

EPS = 1e-6
LANES = 128
SB_HEADS = 8
SB_HEAD_DIM = 64
SB_WIDTH = SB_HEADS * SB_HEAD_DIM
MLA_HEADS = 8
MLA_Q_RANK = 256
MLA_KV_RANK = 128
MLA_NOPE_DIM = 64
MLA_ROPE_DIM = 32
MLA_QK_DIM = MLA_NOPE_DIM + MLA_ROPE_DIM
MLA_V_DIM = 64
MLA_WIDTH = MLA_HEADS * MLA_V_DIM
ROPE_THETA = 10000.0
SSM_GROUP = 16
SSM_GROUPS = 32
SSM_STATE = 64
SSM_WIDTH = SSM_GROUP * SSM_GROUPS
SSM_NSTATE = SSM_GROUPS * SSM_STATE
EIG_RE_MAX = -1e-4
N_EXPERTS = 8
VMEM_LIMIT = 56 * 1024 * 1024


def _params(*sem):
    return pltpu.CompilerParams(dimension_semantics=sem, vmem_limit_bytes=VMEM_LIMIT)


def _rms(x, n):
    return x * lax.rsqrt(jnp.sum(x * x, axis=-1, keepdims=True) * (1.0 / n) + EPS)


def _dot(a, b):
    return jnp.dot(a, b, preferred_element_type=F32)


def _dot_nt(a, b):
    return lax.dot_general(a, b, (((1,), (1,)), ((), ())), preferred_element_type=F32)


def _const_spec(shape):
    return pl.BlockSpec(shape, lambda *_: (0,) * len(shape))


IN_SB = 3 * SB_WIDTH
IN_CQ = IN_SB
IN_CKV = IN_CQ + MLA_Q_RANK
IN_KR = IN_CKV + MLA_KV_RANK
IN_KRS = IN_KR + LANES
IN_COLS_PAD = IN_KRS + LANES
MLA_PAD = MLA_HEADS * LANES


def _inproj_kernel(x_ref, g_ref, win_ref, qlg_ref, wq_ref, kvlg_ref, wkv_ref, ctab_ref, stab_ref,
                   qg_ref, qgs_ref, kg_ref, kgs_ref, qkv_ref, qm_ref, km_ref, vm_ref):
    x = x_ref[...]
    xn = _rms(x, x.shape[-1]) * g_ref[...]
    hh = _dot(xn.astype(BF16), win_ref[...])
    qkv_ref[:, :SB_WIDTH] = (hh[:, :SB_WIDTH] * (SB_HEAD_DIM ** -0.5)).astype(BF16)
    qkv_ref[:, SB_WIDTH:] = hh[:, SB_WIDTH:IN_SB].astype(BF16)
    cq = _rms(hh[:, IN_CQ:IN_CKV], MLA_Q_RANK) * qlg_ref[...]
    qf = _dot(cq.astype(BF16), wq_ref[...])
    ckv = _rms(hh[:, IN_CKV:IN_KR], MLA_KV_RANK) * kvlg_ref[...]
    kvf = _dot(ckv.astype(BF16), wkv_ref[...])
    vm_ref[...] = kvf[:, MLA_PAD:].T.astype(BF16)
    kr = hh[:, IN_KR:IN_KRS]
    krs = hh[:, IN_KRS:IN_COLS_PAD]
    ctab = ctab_ref[...]
    stab = stab_ref[...]
    scale = MLA_QK_DIM ** -0.5
    cq_t = ctab * (qg_ref[...] * scale)
    sq_t = stab * (qgs_ref[...] * scale)
    ck_t = ctab * kg_ref[...]
    sk_t = stab * kgs_ref[...]
    for h in range(MLA_HEADS):
        lo, hi = h * LANES, (h + 1) * LANES
        qh = qf[:, lo:hi]
        qs = qf[:, MLA_PAD + lo:MLA_PAD + hi]
        q_inv = lax.rsqrt(jnp.sum(qh * qh, axis=-1, keepdims=True) * (1.0 / MLA_QK_DIM) + EPS)
        qm_ref[:, lo:hi] = ((qh * cq_t + qs * sq_t) * q_inv).astype(BF16)
        kh = kvf[:, lo:hi] + kr
        k_inv = lax.rsqrt(jnp.sum(kh * kh, axis=-1, keepdims=True) * (1.0 / MLA_QK_DIM) + EPS)
        km_ref[:, lo:hi] = ((kh * ck_t + krs * sk_t) * k_inv).astype(BF16)


def _pad_heads(w, n_heads, width):
    k = w.shape[0]
    w = w.reshape(k, n_heads, width)
    return jnp.pad(w, ((0, 0), (0, 0), (0, LANES - width))).reshape(k, n_heads * LANES)


def _swap_rope(w):
    half = MLA_ROPE_DIM // 2
    return jnp.concatenate([jnp.zeros_like(w[..., :MLA_NOPE_DIM]),
                            w[..., MLA_NOPE_DIM + half:], w[..., MLA_NOPE_DIM:MLA_NOPE_DIM + half]], axis=-1)


def _rope_lane_tables(s_len):
    half = MLA_ROPE_DIM // 2
    inv_freq = ROPE_THETA ** (-jnp.arange(0, MLA_ROPE_DIM, 2, dtype=F32) / MLA_ROPE_DIM)
    ang = jnp.arange(s_len, dtype=F32)[:, None] * inv_freq[None, :]
    cos, sin = jnp.cos(ang), jnp.sin(ang)
    ones = jnp.ones((s_len, MLA_NOPE_DIM), F32)
    zeros_n = jnp.zeros((s_len, MLA_NOPE_DIM), F32)
    zeros_t = jnp.zeros((s_len, LANES - MLA_QK_DIM), F32)
    ctab = jnp.concatenate([ones, cos, cos, zeros_t], axis=-1)
    stab = jnp.concatenate([zeros_n, -sin, sin, zeros_t], axis=-1)
    del half
    return ctab, stab


def _in_projection(x2, s_len, norm_g, w_in, q_lat_g, w_q_up, kv_lat_g, w_kv_up, q_g, k_g, tm):
    t, d = x2.shape
    w_kr = w_in[:, IN_KR - LANES + 0:][:, :0]
    del w_kr
    off_kr = 3 * SB_WIDTH + MLA_Q_RANK + MLA_KV_RANK
    w_rope = w_in[:, off_kr:]
    zeros_n = jnp.zeros((d, MLA_NOPE_DIM), F32)
    zeros_t = jnp.zeros((d, LANES - MLA_QK_DIM), F32)
    half = MLA_ROPE_DIM // 2
    w_in_pad = jnp.concatenate(
        [w_in[:, :off_kr], zeros_n, w_rope, zeros_t,
         zeros_n, w_rope[:, half:], w_rope[:, :half], zeros_t], axis=-1).astype(BF16)
    wq3 = w_q_up.reshape(MLA_Q_RANK, MLA_HEADS, MLA_QK_DIM)
    wq_pad = jnp.concatenate(
        [_pad_heads(w_q_up, MLA_HEADS, MLA_QK_DIM),
         _pad_heads(_swap_rope(wq3).reshape(MLA_Q_RANK, -1), MLA_HEADS, MLA_QK_DIM)], axis=-1).astype(BF16)
    wkv3 = w_kv_up.reshape(MLA_KV_RANK, MLA_HEADS, MLA_NOPE_DIM + MLA_V_DIM)
    wkv_pad = jnp.concatenate(
        [_pad_heads(wkv3[..., :MLA_NOPE_DIM].reshape(MLA_KV_RANK, -1), MLA_HEADS, MLA_NOPE_DIM),
         wkv3[..., MLA_NOPE_DIM:].reshape(MLA_KV_RANK, -1)], axis=-1).astype(BF16)
    ctab, stab = _rope_lane_tables(s_len)
    pad_g = lambda g: jnp.pad(g, (0, LANES - MLA_QK_DIM)).reshape(1, LANES)
    swap_g = lambda g: jnp.pad(jnp.concatenate(
        [jnp.zeros((MLA_NOPE_DIM,), F32), g[MLA_NOPE_DIM + half:], g[MLA_NOPE_DIM:MLA_NOPE_DIM + half]]),
        (0, LANES - MLA_QK_DIM)).reshape(1, LANES)
    n_pos = s_len // tm
    row = lambda i: (i, 0)
    return pl.pallas_call(
        _inproj_kernel,
        grid=(t // tm,),
        in_specs=[
            pl.BlockSpec((tm, d), row),
            _const_spec((1, d)),
            _const_spec((d, IN_COLS_PAD)),
            _const_spec((1, MLA_Q_RANK)),
            _const_spec((MLA_Q_RANK, 2 * MLA_PAD)),
            _const_spec((1, MLA_KV_RANK)),
            _const_spec((MLA_KV_RANK, MLA_PAD + MLA_WIDTH)),
            pl.BlockSpec((tm, LANES), lambda i: (i % n_pos, 0)),
            pl.BlockSpec((tm, LANES), lambda i: (i % n_pos, 0)),
            _const_spec((1, LANES)), _const_spec((1, LANES)),
            _const_spec((1, LANES)), _const_spec((1, LANES)),
        ],
        out_specs=[
            pl.BlockSpec((tm, IN_SB), row),
            pl.BlockSpec((tm, MLA_PAD), row),
            pl.BlockSpec((tm, MLA_PAD), row),
            pl.BlockSpec((MLA_WIDTH, tm), lambda i: (0, i)),
        ],
        out_shape=[
            jax.ShapeDtypeStruct((t, IN_SB), BF16),
            jax.ShapeDtypeStruct((t, MLA_PAD), BF16),
            jax.ShapeDtypeStruct((t, MLA_PAD), BF16),
            jax.ShapeDtypeStruct((MLA_WIDTH, t), BF16),
        ],
        compiler_params=_params("arbitrary"),
        name="in_projection",
    )(x2, norm_g.reshape(1, d), w_in_pad, q_lat_g.reshape(1, -1), wq_pad, kv_lat_g.reshape(1, -1), wkv_pad,
      ctab, stab, pad_g(q_g), swap_g(q_g), pad_g(k_g), swap_g(k_g))


SB_EXP_UNDERFLOW = 110.0


def _sb_kernel(q_ref, k_ref, v_ref, u_ref, o_ref, acc_ref, carry_ref, z_ref, *, tq, tk):
    qi = pl.program_id(2)
    lane = lax.broadcasted_iota(jnp.int32, (1, LANES), 1)
    q = q_ref[...]
    q_heads = [jnp.where((lane >= j * SB_HEAD_DIM) & (lane < (j + 1) * SB_HEAD_DIM), q, jnp.zeros_like(q))
               for j in range(2)]
    acc_ref[...] = jnp.zeros_like(acc_ref)
    carry_ref[...] = jnp.zeros_like(carry_ref)

    def logits(j, c, lo_row=0):
        start = pl.multiple_of(c * tk, tk)
        z_ref[j, lo_row:, :] = _dot_nt(q_heads[j][lo_row:], k_ref[pl.ds(start, tk), :])

    def absorb(j, c, diagonal, lo_row=0):
        start = pl.multiple_of(c * tk, tk)
        z = z_ref[j, lo_row:, :]
        log_fail = -(jnp.maximum(z, 0.0) + jnp.log(1.0 + jnp.exp(-jnp.abs(z))))
        if diagonal:
            key = start + lax.broadcasted_iota(jnp.int32, z.shape, 1)
            qry = qi * tq + lo_row + lax.broadcasted_iota(jnp.int32, z.shape, 0)
            earlier = key < qry
            log_fail = jnp.where(earlier, log_fail, 0.0)
        hi = log_fail.astype(BF16)
        lo = (log_fail - hi.astype(F32)).astype(BF16)
        log_stick = _dot(hi, u_ref[...]) + _dot(lo, u_ref[...])
        carry = carry_ref[j, lo_row:, :]
        w = jnp.exp(z + log_fail + log_stick + jnp.tile(carry, (1, tk // LANES)))
        if diagonal:
            w = jnp.where(earlier, w, 0.0)
        acc_ref[j, lo_row:, :] += _dot(w.astype(BF16), v_ref[pl.ds(start, tk), :])
        carry_ref[j, lo_row:, :] = carry + (log_stick[:, :1] + log_fail[:, :1])

    def sweep(c, diagonal, lo_row=0, next_lo_row=0):
        logits(1, c, lo_row)
        absorb(0, c, diagonal, lo_row)
        logits(0, jnp.maximum(c - 1, 0), next_lo_row)
        absorb(1, c, diagonal, lo_row)

    n_sub = tq // tk
    first = qi * n_sub + n_sub - 1
    logits(0, first, (n_sub - 1) * tk)
    for sub in reversed(range(n_sub)):
        sweep(qi * n_sub + sub, True, sub * tk, max(sub - 1, 0) * tk)

    def sticks_alive():
        return jnp.max(carry_ref[...]) > -SB_EXP_UNDERFLOW

    def earlier_chunk(state):
        c, _ = state
        sweep(c, False)
        return c - 1, sticks_alive()

    lax.while_loop(lambda state: (state[0] >= 0) & state[1], earlier_chunk,
                   (qi * (tq // tk) - 1, sticks_alive()))
    o_ref[...] = jnp.where(lane < SB_HEAD_DIM, acc_ref[0], acc_ref[1]).astype(o_ref.dtype)


def _sb_attention(qkv, tq, tk):
    b, s, _ = qkv.shape
    n_pairs = SB_WIDTH // LANES
    tri = (jnp.arange(tk)[:, None] > jnp.arange(tk)[None, :]).astype(BF16)
    return pl.pallas_call(
        functools.partial(_sb_kernel, tq=tq, tk=tk),
        grid=(b, n_pairs, s // tq),
        in_specs=[
            pl.BlockSpec((None, tq, LANES), lambda bi, hp, qi: (bi, qi, hp)),
            pl.BlockSpec((None, s, LANES), lambda bi, hp, qi: (bi, 0, n_pairs + hp)),
            pl.BlockSpec((None, s, LANES), lambda bi, hp, qi: (bi, 0, 2 * n_pairs + hp)),
            _const_spec((tk, tk)),
        ],
        out_specs=pl.BlockSpec((None, tq, LANES), lambda bi, hp, qi: (bi, qi, hp)),
        out_shape=jax.ShapeDtypeStruct((b, s, SB_WIDTH), BF16),
        scratch_shapes=[pltpu.VMEM((2, tq, LANES), F32), pltpu.VMEM((2, tq, LANES), F32),
                        pltpu.VMEM((2, tq, tk), F32)],
        compiler_params=_params("arbitrary", "arbitrary", "arbitrary"),
        name="stick_breaking_attention",
    )(qkv, qkv, qkv, tri)


def _mla_kernel(q_ref, k_ref, vt_ref, o_ref, acc_ref, m_ref, l_ref, s_ref, *, tq):
    qi = pl.program_id(2)
    acc_ref[...] = jnp.zeros_like(acc_ref)
    l_ref[...] = jnp.zeros_like(l_ref)
    m_ref[...] = jnp.full(m_ref.shape, -jnp.inf, F32)

    def scores(j, c):
        start = pl.multiple_of(c * tq, tq)
        s_ref[j] = _dot_nt(k_ref[pl.ds(start, tq), j * LANES:(j + 1) * LANES],
                           q_ref[:, j * LANES:(j + 1) * LANES])

    def update(j, c, diagonal):
        start = pl.multiple_of(c * tq, tq)
        s = s_ref[j]
        if diagonal:
            key = lax.broadcasted_iota(jnp.int32, (tq, tq), 0)
            qry = lax.broadcasted_iota(jnp.int32, (tq, tq), 1)
            s = jnp.where(key <= qry, s, -jnp.inf)
        m_prev = m_ref[j]
        m_new = jnp.maximum(m_prev, jnp.max(s, axis=0, keepdims=True))
        alpha = jnp.exp(m_prev - m_new)
        p = jnp.exp(s - m_new)
        l_ref[j] = alpha * l_ref[j] + jnp.sum(p, axis=0, keepdims=True)
        vt = vt_ref[j * MLA_V_DIM:(j + 1) * MLA_V_DIM, pl.ds(start, tq)]
        acc_ref[j] = alpha * acc_ref[j] + _dot(vt, p.astype(BF16))
        m_ref[j] = m_new

    scores(0, 0)

    def earlier_block(c, _):
        scores(1, c)
        update(0, c, False)
        scores(0, c + 1)
        update(1, c, False)
        return 0

    lax.fori_loop(0, qi, earlier_block, 0)
    scores(1, qi)
    update(0, qi, True)
    update(1, qi, True)
    out_t = jnp.concatenate([acc_ref[0] / l_ref[0], acc_ref[1] / l_ref[1]], axis=0)
    o_ref[...] = out_t.T.astype(o_ref.dtype)


def _mla_attention(q, k, vt, tq):
    b, s, _ = q.shape
    n_pairs = MLA_WIDTH // LANES
    return pl.pallas_call(
        functools.partial(_mla_kernel, tq=tq),
        grid=(b, n_pairs, s // tq),
        in_specs=[
            pl.BlockSpec((None, tq, 2 * LANES), lambda bi, hp, qi: (bi, qi, hp)),
            pl.BlockSpec((None, s, 2 * LANES), lambda bi, hp, qi: (bi, 0, hp)),
            pl.BlockSpec((LANES, s), lambda bi, hp, qi: (hp, bi)),
        ],
        out_specs=pl.BlockSpec((None, tq, LANES), lambda bi, hp, qi: (bi, qi, hp)),
        out_shape=jax.ShapeDtypeStruct((b, s, MLA_WIDTH), BF16),
        scratch_shapes=[pltpu.VMEM((2, MLA_V_DIM, tq), F32), pltpu.VMEM((2, 1, tq), F32),
                        pltpu.VMEM((2, 1, tq), F32), pltpu.VMEM((2, tq, tq), F32)],
        compiler_params=_params("arbitrary", "arbitrary", "arbitrary"),
        name="latent_attention",
    )(q, k, vt)


def _outproj_kernel(x_ref, osb_ref, omla_ref, wsb_ref, wmla_ref, g_ref, h_ref, hn_ref):
    h = x_ref[...] + _dot(osb_ref[...], wsb_ref[...]) + _dot(omla_ref[...], wmla_ref[...])
    h_ref[...] = h
    hn_ref[...] = (_rms(h, h.shape[-1]) * g_ref[...]).astype(BF16)


def _out_projection(x2, o_sb, o_mla, w_out, next_norm_g, tm):
    t, d = x2.shape
    row = lambda i: (i, 0)
    w_sb = w_out[:SB_WIDTH].astype(BF16)
    w_mla = w_out[SB_WIDTH:].astype(BF16)
    return pl.pallas_call(
        _outproj_kernel,
        grid=(t // tm,),
        in_specs=[pl.BlockSpec((tm, d), row), pl.BlockSpec((tm, SB_WIDTH), row), pl.BlockSpec((tm, MLA_WIDTH), row),
                  _const_spec((SB_WIDTH, d)), _const_spec((MLA_WIDTH, d)), _const_spec((1, d))],
        out_specs=[pl.BlockSpec((tm, d), row), pl.BlockSpec((tm, d), row)],
        out_shape=[jax.ShapeDtypeStruct((t, d), F32), jax.ShapeDtypeStruct((t, d), BF16)],
        compiler_params=_params("arbitrary"),
        name="out_projection",
    )(x2, o_sb, o_mla, w_sb, w_mla, next_norm_g.reshape(1, d))


def _swiglu(x, wg_ref, wu_ref, wd_ref, f_chunk):
    d_ff = wg_ref.shape[-1]
    acc = None
    for c in range(d_ff // f_chunk):
        cols = slice(c * f_chunk, (c + 1) * f_chunk)
        gate = _dot(x, wg_ref[:, cols])
        up = _dot(x, wu_ref[:, cols])
        act = (gate * jax.nn.sigmoid(gate) * up).astype(BF16)
        part = _dot(act, wd_ref[cols, :])
        acc = part if acc is None else acc + part
    return acc


def _dense_ffn_kernel(h_ref, hn_ref, wg_ref, wu_ref, wd_ref, o_ref, *, f_chunk):
    o_ref[...] = h_ref[...] + _swiglu(hn_ref[...], wg_ref, wu_ref, wd_ref, f_chunk)


def _dense_ffn(h, hn, w_gate, w_up, w_down, tm, f_chunk):
    t, d = h.shape
    d_ff = w_gate.shape[-1]
    row = lambda i: (i, 0)
    resident = lambda shape: pl.BlockSpec(shape, lambda i: (0, 0), pipeline_mode=pl.Buffered(1))
    return pl.pallas_call(
        functools.partial(_dense_ffn_kernel, f_chunk=f_chunk),
        grid=(t // tm,),
        in_specs=[pl.BlockSpec((tm, d), row), pl.BlockSpec((tm, d), row),
                  resident((d, d_ff)), resident((d, d_ff)), resident((d_ff, d))],
        out_specs=pl.BlockSpec((tm, d), row),
        out_shape=jax.ShapeDtypeStruct((t, d), F32),
        compiler_params=_params("arbitrary"),
        name="dense_swiglu",
    )(h, hn, w_gate.astype(BF16), w_up.astype(BF16), w_down.astype(BF16))


DENSE_FFN_TILE = 512


def _layer0(x2, b, s, att_norm, att_w_in, q_lat_g, w_q_up, kv_lat_g, w_kv_up, q_g, k_g, w_out,
            dffn_norm, w_gate, w_up, w_down, tm, tq, tk, f_chunk):
    qkv, qm, km, vm = _in_projection(x2, s, att_norm, att_w_in, q_lat_g, w_q_up, kv_lat_g, w_kv_up, q_g, k_g, tm)
    o_sb = _sb_attention(qkv.reshape(b, s, -1), tq, tk).reshape(b * s, -1)
    o_mla = _mla_attention(qm.reshape(b, s, -1), km.reshape(b, s, -1), vm, tq).reshape(b * s, -1)
    h1, h1n = _out_projection(x2, o_sb, o_mla, w_out, dffn_norm, tm)
    return _dense_ffn(h1, h1n, w_gate, w_up, w_down, DENSE_FFN_TILE, f_chunk)


SSM_SEGS = 8
SSM_SEG_LEN = 64
SSM_TILE = SSM_SEGS * SSM_SEG_LEN
SSM_CH_CHUNK = LANES
SSM_ST_CHUNK = SSM_CH_CHUNK // SSM_GROUP * SSM_STATE
SSM_SCAN_TILES = 8


def _s5_prep_kernel(are_ref, aim_ref, logdt_ref, bre_ref, bim_ref, cre_ref, cim_ref,
                    wbu_ref, wc_ref, pow_ref):
    n = SSM_NSTATE
    a_re = jnp.minimum(are_ref[...], EIG_RE_MAX)
    a_im = aim_ref[...]
    dt = jnp.exp(logdt_ref[...])
    mag = jnp.exp(a_re * dt)
    lam_re = mag * jnp.cos(a_im * dt)
    lam_im = mag * jnp.sin(a_im * dt)
    den = a_re * a_re + a_im * a_im
    num_re = lam_re - 1.0
    coef_re = (num_re * a_re + lam_im * a_im) / den
    coef_im = (lam_im * a_re - num_re * a_im) / den
    b_re = bre_ref[...]
    b_im = bim_ref[...]
    in_group = (lax.broadcasted_iota(jnp.int32, (SSM_CH_CHUNK, n), 0) // SSM_GROUP
                == lax.broadcasted_iota(jnp.int32, (SSM_CH_CHUNK, n), 1) % SSM_ST_CHUNK // SSM_STATE)
    wbu_ref[:, :n] = jnp.where(in_group, coef_re * b_re - coef_im * b_im, 0.0).astype(BF16)
    wbu_ref[:, n:] = jnp.where(in_group, coef_re * b_im + coef_im * b_re, 0.0).astype(BF16)
    in_group_t = (lax.broadcasted_iota(jnp.int32, (n, SSM_CH_CHUNK), 0) % SSM_ST_CHUNK // SSM_STATE
                  == lax.broadcasted_iota(jnp.int32, (n, SSM_CH_CHUNK), 1) // SSM_GROUP)
    wc_ref[:n, :] = jnp.where(in_group_t, cre_ref[...], 0.0).astype(BF16)
    wc_ref[n:, :] = jnp.where(in_group_t, -cim_ref[...], 0.0).astype(BF16)
    steps = jnp.where(lax.broadcasted_iota(jnp.int32, (SSM_SEGS, n), 0) == 0, 1.0, float(SSM_SEG_LEN))
    mag_k = jnp.exp(steps * (a_re * dt))
    ang_k = steps * (a_im * dt)
    pow_ref[:, :n] = mag_k * jnp.cos(ang_k)
    pow_ref[:, n:] = mag_k * jnp.sin(ang_k)


def _s5_prepare(a_re, a_im, log_dt, b_re, b_im, c_re, c_im):
    n = SSM_NSTATE
    row = lambda a: a.reshape(1, n)
    reps = SSM_CH_CHUNK // SSM_GROUP
    b_t = lambda w: jnp.tile(w.transpose(2, 0, 1).reshape(SSM_GROUP, n), (reps, 1))
    c_t = lambda w: jnp.tile(w.transpose(0, 2, 1).reshape(n, SSM_GROUP), (1, reps))
    return pl.pallas_call(
        _s5_prep_kernel,
        out_shape=[jax.ShapeDtypeStruct((SSM_CH_CHUNK, 2 * n), BF16),
                   jax.ShapeDtypeStruct((2 * n, SSM_CH_CHUNK), BF16),
                   jax.ShapeDtypeStruct((SSM_SEGS, 2 * n), F32)],
        compiler_params=pltpu.CompilerParams(vmem_limit_bytes=VMEM_LIMIT),
        name="s5_discretize",
    )(row(a_re), row(a_im), row(jnp.repeat(log_dt, SSM_STATE)), b_t(b_re), b_t(b_im), c_t(c_re), c_t(c_im))


def _s5_kernel(h_ref, g_ref, win_ref, wbu_ref, pow_ref, wc_ref, d_ref, wglu_ref, o_ref,
               ut_ref, up_ref, st_ref, carry_ref, yp_ref, yt_ref):
    n = SSM_NSTATE
    nt = n // LANES
    n_ch = SSM_WIDTH // LANES
    per = SSM_ST_CHUNK // LANES
    d_model = h_ref.shape[-1]

    @pl.when(pl.program_id(1) == 0)
    def _():
        carry_ref[...] = jnp.zeros_like(carry_ref)

    x = h_ref[...]
    xn = _rms(x, d_model) * g_ref[...]
    u = _dot(xn.astype(BF16), win_ref[...])
    for k in range(n_ch):
        ut_ref[k] = u[:, k * LANES:(k + 1) * LANES]

    def to_step_major(i, _):
        dst = pl.ds(pl.multiple_of(i * SSM_SEGS, SSM_SEGS), SSM_SEGS)
        for k in range(n_ch):
            up_ref[k, dst, :] = ut_ref[k, pl.ds(i, SSM_SEGS, stride=SSM_SEG_LEN), :]
        return 0

    lax.fori_loop(0, SSM_SEG_LEN, to_step_major, 0)

    for c in range(n_ch):
        u16 = up_ref[c].astype(BF16)
        for part in range(2):
            cols = slice(part * n + c * SSM_ST_CHUNK, part * n + (c + 1) * SSM_ST_CHUNK)
            bu = _dot(u16, wbu_ref[:, cols])
            for k in range(per):
                st_ref[part * nt + c * per + k] = bu[:, k * LANES:(k + 1) * LANES]

    for grp in range(nt // SSM_SCAN_TILES):
        tiles = range(grp * SSM_SCAN_TILES, (grp + 1) * SSM_SCAN_TILES)
        re_l = [slice(k * LANES, (k + 1) * LANES) for k in tiles]
        im_l = [slice(n + k * LANES, n + (k + 1) * LANES) for k in tiles]
        lam_re = [jnp.broadcast_to(pow_ref[0:1, c], (SSM_SEGS, LANES)) for c in re_l]
        lam_im = [jnp.broadcast_to(pow_ref[0:1, c], (SSM_SEGS, LANES)) for c in im_l]

        def scan(state, store):
            def step(i, state):
                rows = pl.ds(pl.multiple_of(i * SSM_SEGS, SSM_SEGS), SSM_SEGS)
                new_state = []
                for idx, k in enumerate(tiles):
                    s_re, s_im = state[idx]
                    n_re = lam_re[idx] * s_re - lam_im[idx] * s_im + st_ref[k, rows, :]
                    n_im = lam_re[idx] * s_im + lam_im[idx] * s_re + st_ref[nt + k, rows, :]
                    if store:
                        st_ref[k, rows, :] = n_re
                        st_ref[nt + k, rows, :] = n_im
                    new_state.append((n_re, n_im))
                return tuple(new_state)

            return lax.fori_loop(0, SSM_SEG_LEN, step, state)

        zero = jnp.zeros((SSM_SEGS, LANES), F32)
        ends = scan(tuple((zero, zero) for _ in tiles), False)
        starts = []
        for idx in range(len(tiles)):
            e_re, e_im = ends[idx]
            p_re = pow_ref[1:2, re_l[idx]]
            p_im = pow_ref[1:2, im_l[idx]]
            c_re = carry_ref[:, re_l[idx]]
            c_im = carry_ref[:, im_l[idx]]
            rows_re, rows_im = [], []
            for j in range(SSM_SEGS):
                rows_re.append(c_re)
                rows_im.append(c_im)
                c_re, c_im = (e_re[j:j + 1] + (p_re * c_re - p_im * c_im),
                              e_im[j:j + 1] + (p_re * c_im + p_im * c_re))
            carry_ref[:, re_l[idx]] = c_re
            carry_ref[:, im_l[idx]] = c_im
            starts.append((jnp.concatenate(rows_re, axis=0), jnp.concatenate(rows_im, axis=0)))
        scan(tuple(starts), True)

    for c in range(n_ch):
        h_re = jnp.concatenate([st_ref[c * per + k] for k in range(per)], axis=-1).astype(BF16)
        h_im = jnp.concatenate([st_ref[nt + c * per + k] for k in range(per)], axis=-1).astype(BF16)
        rows_re = slice(c * SSM_ST_CHUNK, (c + 1) * SSM_ST_CHUNK)
        rows_im = slice(n + c * SSM_ST_CHUNK, n + (c + 1) * SSM_ST_CHUNK)
        yp_ref[c] = _dot(h_re, wc_ref[rows_re, :]) + _dot(h_im, wc_ref[rows_im, :])

    def to_token_major(i, _):
        src = pl.ds(pl.multiple_of(i * SSM_SEGS, SSM_SEGS), SSM_SEGS)
        for k in range(n_ch):
            yt_ref[k, pl.ds(i, SSM_SEGS, stride=SSM_SEG_LEN), :] = yp_ref[k, src, :]
        return 0

    lax.fori_loop(0, SSM_SEG_LEN, to_token_major, 0)
    y = jnp.concatenate([yt_ref[k] for k in range(n_ch)], axis=-1)
    y = jax.nn.gelu(y + d_ref[...] * u)
    z = _dot(y.astype(BF16), wglu_ref[...])
    o_ref[...] = x + z[:, :d_model] * jax.nn.sigmoid(z[:, d_model:])


def _s5_layer(h, b, s, norm_g, w_in, a_re, a_im, log_dt, b_re, b_im, c_re, c_im, d_skip, w_glu):
    t, d = h.shape
    n = SSM_NSTATE
    w_bu, w_c, powers = _s5_prepare(a_re, a_im, log_dt, b_re, b_im, c_re, c_im)
    tiles = s // SSM_TILE
    resident = lambda shape: pl.BlockSpec(shape, lambda bi, ti: (0, 0), pipeline_mode=pl.Buffered(1))
    row = lambda bi, ti: (bi * tiles + ti, 0)
    ch_tiles = pltpu.VMEM((SSM_WIDTH // LANES, SSM_TILE, LANES), F32)
    return pl.pallas_call(
        _s5_kernel,
        grid=(b, tiles),
        in_specs=[pl.BlockSpec((SSM_TILE, d), row), resident((1, d)), resident((d, SSM_WIDTH)),
                  resident((SSM_CH_CHUNK, 2 * n)), resident((SSM_SEGS, 2 * n)), resident((2 * n, SSM_CH_CHUNK)),
                  resident((1, SSM_WIDTH)), resident((SSM_WIDTH, 2 * d))],
        out_specs=pl.BlockSpec((SSM_TILE, d), row),
        out_shape=jax.ShapeDtypeStruct((t, d), F32),
        scratch_shapes=[ch_tiles, ch_tiles, pltpu.VMEM((2 * n // LANES, SSM_TILE, LANES), F32),
                        pltpu.VMEM((1, 2 * n), F32), ch_tiles, ch_tiles],
        compiler_params=_params("arbitrary", "arbitrary"),
        name="s5_mixer",
    )(h, norm_g.reshape(1, d), w_in.astype(BF16), w_bu, powers, w_c, d_skip.reshape(1, -1), w_glu.astype(BF16))


MOE_BLOCK = 256
META_E1, META_E2, META_G1, META_G2, META_R1, META_R2 = range(6)


def _router_kernel(h_ref, g_ref, wr_ref, tri_ref, xn_ref, meta_ref, counts_ref, run_ref):
    @pl.when(pl.program_id(0) == 0)
    def _():
        run_ref[...] = jnp.zeros_like(run_ref)

    x = h_ref[...]
    xn = _rms(x, x.shape[-1]) * g_ref[...]
    xn_ref[...] = xn
    logits = jnp.dot(xn, wr_ref[...], preferred_element_type=F32, precision=lax.Precision.HIGHEST)
    tm = logits.shape[0]
    e = lax.broadcasted_iota(jnp.int32, (tm, N_EXPERTS), 1)
    v1 = jnp.max(logits, axis=-1, keepdims=True)
    i1 = jnp.min(jnp.where(logits == v1, e, N_EXPERTS), axis=-1, keepdims=True)
    rest = jnp.where(e == i1, -jnp.inf, logits)
    v2 = jnp.max(rest, axis=-1, keepdims=True)
    i2 = jnp.min(jnp.where(rest == v2, e, N_EXPERTS), axis=-1, keepdims=True)
    ex = jnp.exp(v2 - v1)
    g1 = 1.0 / (1.0 + ex)
    g2 = ex / (1.0 + ex)
    oh1 = (e == i1).astype(F32)
    oh2 = (e == i2).astype(F32)
    both = oh1 + oh2
    before = _dot(tri_ref[...], both.astype(BF16)) + run_ref[...]
    r1 = jnp.sum(oh1 * before, axis=-1, keepdims=True)
    r2 = jnp.sum(oh2 * before, axis=-1, keepdims=True)
    run_ref[...] += jnp.sum(both, axis=0, keepdims=True)
    counts_ref[...] = run_ref[...]
    meta = jnp.zeros((tm, N_EXPERTS), F32)
    for lane, val in ((META_E1, i1.astype(F32)), (META_E2, i2.astype(F32)), (META_G1, g1), (META_G2, g2),
                      (META_R1, r1), (META_R2, r2)):
        meta = jnp.where(e == lane, val, meta)
    meta_ref[...] = meta


def _router(h, norm_g, w_router, tm):
    t, d = h.shape
    row = lambda i: (i, 0)
    tri = (jnp.arange(tm)[:, None] > jnp.arange(tm)[None, :]).astype(BF16)
    return pl.pallas_call(
        _router_kernel,
        grid=(t // tm,),
        in_specs=[pl.BlockSpec((tm, d), row), _const_spec((1, d)), _const_spec((d, N_EXPERTS)),
                  _const_spec((tm, tm))],
        out_specs=[pl.BlockSpec((tm, d), row), pl.BlockSpec((tm, N_EXPERTS), row), _const_spec((1, N_EXPERTS))],
        out_shape=[jax.ShapeDtypeStruct((t, d), F32), jax.ShapeDtypeStruct((t, N_EXPERTS), F32),
                   jax.ShapeDtypeStruct((1, N_EXPERTS), F32)],
        scratch_shapes=[pltpu.VMEM((1, N_EXPERTS), F32)],
        compiler_params=_params("arbitrary"),
        name="moe_router",
    )(h, norm_g.reshape(1, d), w_router, tri)


def _slot_map_kernel(d1_ref, d2_ref, tok_ref):
    def clear(slot, _):
        tok_ref[slot] = 0
        return 0

    lax.fori_loop(0, tok_ref.shape[0], clear, 0, unroll=16)

    def place(tok, _):
        tok_ref[d1_ref[tok]] = tok
        tok_ref[d2_ref[tok]] = tok
        return 0

    lax.fori_loop(0, d1_ref.shape[0], place, 0, unroll=8)


def _slot_map(dest1, dest2, n_slots):
    smem = pl.BlockSpec(memory_space=pltpu.SMEM)
    return pl.pallas_call(
        _slot_map_kernel,
        in_specs=[smem, smem], out_specs=smem,
        out_shape=jax.ShapeDtypeStruct((n_slots,), jnp.int32),
        name="moe_slot_map",
    )(dest1, dest2)


def _row_copy(src_ref, src_row, dst_ref, dst_row, sem):
    return pltpu.make_async_copy(src_ref.at[pl.ds(src_row, 1)], dst_ref.at[pl.ds(dst_row, 1)], sem)


def _expert_ffn_kernel(be_ref, bv_ref, tok_ref, xn_ref, wg_ref, wu_ref, wd_ref, ys_ref, buf_ref, sem, *, f_chunk):
    del be_ref
    i = pl.program_id(0)
    last = pl.num_programs(0) - 1
    cur = i % 2

    def gather(block, buf):
        base = block * MOE_BLOCK
        for r in range(MOE_BLOCK):
            _row_copy(xn_ref, tok_ref[base + r], buf_ref.at[buf], r, sem.at[buf]).start()

    def wait_gather(buf):
        pltpu.make_async_copy(xn_ref.at[pl.ds(0, MOE_BLOCK)], buf_ref.at[buf], sem.at[buf]).wait()

    @pl.when(i == 0)
    def _():
        gather(0, 0)

    gather(jnp.minimum(i + 1, last), 1 - cur)
    wait_gather(cur)
    valid = bv_ref[i] != 0

    @pl.when(valid)
    def _():
        ys_ref[...] = _swiglu(buf_ref[cur].astype(BF16), wg_ref, wu_ref, wd_ref, f_chunk)

    @pl.when(jnp.logical_not(valid))
    def _():
        ys_ref[...] = jnp.zeros_like(ys_ref)

    @pl.when(i == last)
    def _():
        wait_gather(1 - cur)


def _expert_ffn(xn, slot_tok, blk_expert, blk_valid, w_gate, w_up, w_down, f_chunk):
    n_slots = slot_tok.shape[0]
    d = xn.shape[-1]
    d_ff = w_gate.shape[-1]
    return pl.pallas_call(
        functools.partial(_expert_ffn_kernel, f_chunk=f_chunk),
        grid_spec=pltpu.PrefetchScalarGridSpec(
            num_scalar_prefetch=3, grid=(n_slots // MOE_BLOCK,),
            in_specs=[pl.BlockSpec(memory_space=pl.ANY),
                      pl.BlockSpec((None, d, d_ff), lambda i, be, bv, st: (be[i], 0, 0)),
                      pl.BlockSpec((None, d, d_ff), lambda i, be, bv, st: (be[i], 0, 0)),
                      pl.BlockSpec((None, d_ff, d), lambda i, be, bv, st: (be[i], 0, 0))],
            out_specs=pl.BlockSpec((MOE_BLOCK, d), lambda i, be, bv, st: (i, 0)),
            scratch_shapes=[pltpu.VMEM((2, MOE_BLOCK, d), F32), pltpu.SemaphoreType.DMA((2,))]),
        out_shape=jax.ShapeDtypeStruct((n_slots, d), F32),
        compiler_params=_params("arbitrary"),
        name="expert_swiglu",
    )(blk_expert, blk_valid, slot_tok, xn, w_gate.astype(BF16), w_up.astype(BF16), w_down.astype(BF16))


def _combine_kernel(d1_ref, d2_ref, h_ref, meta_ref, ys_ref, o_ref, a_ref, b_ref, sem, *, tm):
    base = pl.program_id(0) * tm

    def issue(r, _):
        tok = base + r
        _row_copy(ys_ref, d1_ref[tok], a_ref, r, sem).start()
        _row_copy(ys_ref, d2_ref[tok], b_ref, r, sem).start()
        return 0

    lax.fori_loop(0, tm, issue, 0, unroll=8)

    pltpu.make_async_copy(ys_ref.at[pl.ds(0, tm)], a_ref, sem).wait()
    pltpu.make_async_copy(ys_ref.at[pl.ds(0, tm)], b_ref, sem).wait()
    meta = meta_ref[...]
    g1 = meta[:, META_G1:META_G1 + 1]
    g2 = meta[:, META_G2:META_G2 + 1]
    o_ref[...] = h_ref[...] + (g1 * a_ref[...] + g2 * b_ref[...])


def _combine(h, meta, ys, dest1, dest2, tm):
    t, d = h.shape
    row = lambda i, d1, d2: (i, 0)
    return pl.pallas_call(
        functools.partial(_combine_kernel, tm=tm),
        grid_spec=pltpu.PrefetchScalarGridSpec(
            num_scalar_prefetch=2, grid=(t // tm,),
            in_specs=[pl.BlockSpec((tm, d), row), pl.BlockSpec((tm, N_EXPERTS), row),
                      pl.BlockSpec(memory_space=pl.ANY)],
            out_specs=pl.BlockSpec((tm, d), row),
            scratch_shapes=[pltpu.VMEM((tm, d), F32), pltpu.VMEM((tm, d), F32), pltpu.SemaphoreType.DMA(())]),
        out_shape=jax.ShapeDtypeStruct((t, d), F32),
        compiler_params=_params("arbitrary"),
        name="moe_combine",
    )(dest1, dest2, h, meta, ys)


def _moe_layer(h, norm_g, w_router, w_gate, w_up, w_down, tm, f_chunk):
    t, d = h.shape
    xn, meta, counts = _router(h, norm_g, w_router, tm)
    counts = counts.reshape(N_EXPERTS).astype(jnp.int32)
    padded = (counts + MOE_BLOCK - 1) // MOE_BLOCK * MOE_BLOCK
    pad_end = jnp.cumsum(padded)
    pad_start = pad_end - padded
    e1 = meta[:, META_E1].astype(jnp.int32)
    e2 = meta[:, META_E2].astype(jnp.int32)
    dest1 = pad_start[e1] + meta[:, META_R1].astype(jnp.int32)
    dest2 = pad_start[e2] + meta[:, META_R2].astype(jnp.int32)
    n_slots = 2 * t + N_EXPERTS * MOE_BLOCK
    blk_start = jnp.arange(n_slots // MOE_BLOCK, dtype=jnp.int32) * MOE_BLOCK
    blk_expert = jnp.minimum(jnp.sum(blk_start[:, None] >= pad_end[None, :], axis=1), N_EXPERTS - 1).astype(jnp.int32)
    blk_valid = (blk_start < pad_end[-1]).astype(jnp.int32)
    slot_tok = _slot_map(dest1, dest2, n_slots)
    ys = _expert_ffn(xn, slot_tok, blk_expert, blk_valid, w_gate, w_up, w_down, f_chunk)
    return _combine(h, meta, ys, dest1, dest2, tm)


def kernel(x, att_norm, att_w_in, att_q_latent_norm, att_w_q_up, att_kv_latent_norm, att_w_kv_up, att_q_norm, att_k_norm, att_w_out, dffn_norm, dffn_w_gate, dffn_w_up, dffn_w_down, ssm_norm, ssm_w_in, ssm_a_re, ssm_a_im, ssm_log_dt, ssm_b_re, ssm_b_im, ssm_c_re, ssm_c_im, ssm_d, ssm_w_glu, moe_norm, moe_router, moe_w_gate, moe_w_up, moe_w_down):
    b, s, d = x.shape
    x2 = x.reshape(b * s, d)
    h = _layer0(x2, b, s, att_norm[0], att_w_in[0], att_q_latent_norm[0], att_w_q_up[0], att_kv_latent_norm[0],
                att_w_kv_up[0], att_q_norm[0], att_k_norm[0], att_w_out[0], dffn_norm[0], dffn_w_gate[0],
                dffn_w_up[0], dffn_w_down[0], tm=256, tq=512, tk=256, f_chunk=512)
    h = _s5_layer(h, b, s, ssm_norm[0], ssm_w_in[0], ssm_a_re[0], ssm_a_im[0], ssm_log_dt[0], ssm_b_re[0],
                  ssm_b_im[0], ssm_c_re[0], ssm_c_im[0], ssm_d[0], ssm_w_glu[0])
    h = _moe_layer(h, moe_norm[0], moe_router[0], moe_w_gate[0], moe_w_up[0], moe_w_down[0], tm=256, f_chunk=512)
    return h.reshape(b, s, d)
```

```python
import functools
import math

import jax
import jax.numpy as jnp
from jax import lax
from jax.experimental import pallas as pl
from jax.experimental.pallas import tpu as pltpu

F32 = jnp.float32
BF16 = jnp.bfloat16

EPS = 1e-6
LANES = 128
SB_HEADS = 8
SB_HEAD_DIM = 64
SB_WIDTH = SB_HEADS * SB_HEAD_DIM
MLA_HEADS = 8
MLA_Q_RANK = 256
MLA_KV_RANK = 128
MLA_NOPE_DIM = 64
MLA_ROPE_DIM = 32
MLA_QK_DIM = MLA_NOPE_DIM + MLA_ROPE_DIM
MLA_V_DIM = 64
MLA_WIDTH = MLA_HEADS * MLA_V_DIM
ROPE_THETA = 10000.0
SSM_GROUP = 16
SSM_GROUPS = 32
SSM_STATE = 64
SSM_WIDTH = SSM_GROUP * SSM_GROUPS
SSM_NSTATE = SSM_GROUPS * SSM_STATE
EIG_RE_MAX = -1e-4
N_EXPERTS = 8
VMEM_LIMIT = 56 * 1024 * 1024


def _params(*sem):
    return pltpu.CompilerParams(dimension_semantics=sem, vmem_limit_bytes=VMEM_LIMIT)


def _rms(x, n):
    return x * lax.rsqrt(jnp.sum(x * x, axis=-1, keepdims=True) * (1.0 / n) + EPS)


def _dot(a, b):
    return jnp.dot(a, b, preferred_element_type=F32)


def _dot_nt(a, b):
    return lax.dot_general(a, b, (((1,), (1,)), ((), ())), preferred_element_type=F32)


def _const_spec(shape):
    return pl.BlockSpec(shape, lambda *_: (0,) * len(shape))


IN_SB = 3 * SB_WIDTH
IN_CQ = IN_SB
IN_CKV = IN_CQ + MLA_Q_RANK
IN_KR = IN_CKV + MLA_KV_RANK
IN_KRS = IN_KR + LANES
IN_COLS_PAD = IN_KRS + LANES
MLA_PAD = MLA_HEADS * LANES
MLA_V_ROWS = MLA_V_DIM + 16
MLA_VT_WIDTH = MLA_HEADS * MLA_V_ROWS


def _inproj_kernel(x_ref, g_ref, win_ref, qlg_ref, wq_ref, kvlg_ref, wkv_ref, ctab_ref, stab_ref,
                   qg_ref, qgs_ref, kg_ref, kgs_ref, qkv_ref, qm_ref, km_ref, vm_ref):
    x = x_ref[...]
    xn = _rms(x, x.shape[-1]) * g_ref[...]
    hh = _dot(xn.astype(BF16), win_ref[...])
    qkv_ref[:, :SB_WIDTH] = (hh[:, :SB_WIDTH] * (SB_HEAD_DIM ** -0.5)).astype(BF16)
    qkv_ref[:, SB_WIDTH:] = hh[:, SB_WIDTH:IN_SB].astype(BF16)
    cq = _rms(hh[:, IN_CQ:IN_CKV], MLA_Q_RANK) * qlg_ref[...]
    qf = _dot(cq.astype(BF16), wq_ref[...])
    ckv = _rms(hh[:, IN_CKV:IN_KR], MLA_KV_RANK) * kvlg_ref[...]
    kvf = _dot(ckv.astype(BF16), wkv_ref[...])
    v_col = lax.broadcasted_iota(jnp.int32, (1, MLA_VT_WIDTH), 1)
    v_ones = jnp.where(v_col % MLA_V_ROWS >= MLA_V_DIM, 1.0, 0.0)
    vm_ref[...] = (kvf[:, MLA_PAD:] + v_ones).T.astype(BF16)
    kr = hh[:, IN_KR:IN_KRS]
    krs = hh[:, IN_KRS:IN_COLS_PAD]
    ctab = ctab_ref[...]
    stab = stab_ref[...]
    scale = MLA_QK_DIM ** -0.5 * math.log2(math.e)
    cq_t = ctab * (qg_ref[...] * scale)
    sq_t = stab * (qgs_ref[...] * scale)
    ck_t = ctab * kg_ref[...]
    sk_t = stab * kgs_ref[...]
    for h in range(MLA_HEADS):
        lo, hi = h * LANES, (h + 1) * LANES
        qh = qf[:, lo:hi]
        qs = qf[:, MLA_PAD + lo:MLA_PAD + hi]
        q_inv = lax.rsqrt(jnp.sum(qh * qh, axis=-1, keepdims=True) * (1.0 / MLA_QK_DIM) + EPS)
        qm_ref[:, lo:hi] = ((qh * cq_t + qs * sq_t) * q_inv).astype(BF16)
        kh = kvf[:, lo:hi] + kr
        k_inv = lax.rsqrt(jnp.sum(kh * kh, axis=-1, keepdims=True) * (1.0 / MLA_QK_DIM) + EPS)
        km_ref[:, lo:hi] = ((kh * ck_t + krs * sk_t) * k_inv).astype(BF16)


def _pad_heads(w, n_heads, width):
    k = w.shape[0]
    w = w.reshape(k, n_heads, width)
    return jnp.pad(w, ((0, 0), (0, 0), (0, LANES - width))).reshape(k, n_heads * LANES)


def _swap_rope(w):
    half = MLA_ROPE_DIM // 2
    return jnp.concatenate([jnp.zeros_like(w[..., :MLA_NOPE_DIM]),
                            w[..., MLA_NOPE_DIM + half:], w[..., MLA_NOPE_DIM:MLA_NOPE_DIM + half]], axis=-1)


def _rope_lane_tables(s_len):
    half = MLA_ROPE_DIM // 2
    inv_freq = ROPE_THETA ** (-jnp.arange(0, MLA_ROPE_DIM, 2, dtype=F32) / MLA_ROPE_DIM)
    ang = jnp.arange(s_len, dtype=F32)[:, None] * inv_freq[None, :]
    cos, sin = jnp.cos(ang), jnp.sin(ang)
    ones = jnp.ones((s_len, MLA_NOPE_DIM), F32)
    zeros_n = jnp.zeros((s_len, MLA_NOPE_DIM), F32)
    zeros_t = jnp.zeros((s_len, LANES - MLA_QK_DIM), F32)
    ctab = jnp.concatenate([ones, cos, cos, zeros_t], axis=-1)
    stab = jnp.concatenate([zeros_n, -sin, sin, zeros_t], axis=-1)
    del half
    return ctab, stab


def _in_projection(x2, s_len, norm_g, w_in, q_lat_g, w_q_up, kv_lat_g, w_kv_up, q_g, k_g, tm):
    t, d = x2.shape
    w_kr = w_in[:, IN_KR - LANES + 0:][:, :0]
    del w_kr
    off_kr = 3 * SB_WIDTH + MLA_Q_RANK + MLA_KV_RANK
    w_rope = w_in[:, off_kr:]
    zeros_n = jnp.zeros((d, MLA_NOPE_DIM), F32)
    zeros_t = jnp.zeros((d, LANES - MLA_QK_DIM), F32)
    half = MLA_ROPE_DIM // 2
    w_in_pad = jnp.concatenate(
        [w_in[:, :off_kr], zeros_n, w_rope, zeros_t,
         zeros_n, w_rope[:, half:], w_rope[:, :half], zeros_t], axis=-1).astype(BF16)
    wq3 = w_q_up.reshape(MLA_Q_RANK, MLA_HEADS, MLA_QK_DIM)
    wq_pad = jnp.concatenate(
        [_pad_heads(w_q_up, MLA_HEADS, MLA_QK_DIM),
         _pad_heads(_swap_rope(wq3).reshape(MLA_Q_RANK, -1), MLA_HEADS, MLA_QK_DIM)], axis=-1).astype(BF16)
    wkv3 = w_kv_up.reshape(MLA_KV_RANK, MLA_HEADS, MLA_NOPE_DIM + MLA_V_DIM)
    wkv_pad = jnp.concatenate(
        [_pad_heads(wkv3[..., :MLA_NOPE_DIM].reshape(MLA_KV_RANK, -1), MLA_HEADS, MLA_NOPE_DIM),
         jnp.pad(wkv3[..., MLA_NOPE_DIM:], ((0, 0), (0, 0), (0, MLA_V_ROWS - MLA_V_DIM))).reshape(MLA_KV_RANK, -1)],
        axis=-1).astype(BF16)
    ctab, stab = _rope_lane_tables(s_len)
    pad_g = lambda g: jnp.pad(g, (0, LANES - MLA_QK_DIM)).reshape(1, LANES)
    swap_g = lambda g: jnp.pad(jnp.concatenate(
        [jnp.zeros((MLA_NOPE_DIM,), F32), g[MLA_NOPE_DIM + half:], g[MLA_NOPE_DIM:MLA_NOPE_DIM + half]]),
        (0, LANES - MLA_QK_DIM)).reshape(1, LANES)
    n_pos = s_len // tm
    row = lambda i: (i, 0)
    return pl.pallas_call(
        _inproj_kernel,
        grid=(t // tm,),
        in_specs=[
            pl.BlockSpec((tm, d), row),
            _const_spec((1, d)),
            _const_spec((d, IN_COLS_PAD)),
            _const_spec((1, MLA_Q_RANK)),
            _const_spec((MLA_Q_RANK, 2 * MLA_PAD)),
            _const_spec((1, MLA_KV_RANK)),
            _const_spec((MLA_KV_RANK, MLA_PAD + MLA_VT_WIDTH)),
            pl.BlockSpec((tm, LANES), lambda i: (i % n_pos, 0)),
            pl.BlockSpec((tm, LANES), lambda i: (i % n_pos, 0)),
            _const_spec((1, LANES)), _const_spec((1, LANES)),
            _const_spec((1, LANES)), _const_spec((1, LANES)),
        ],
        out_specs=[
            pl.BlockSpec((tm, IN_SB), row),
            pl.BlockSpec((tm, MLA_PAD), row),
            pl.BlockSpec((tm, MLA_PAD), row),
            pl.BlockSpec((MLA_VT_WIDTH, tm), lambda i: (0, i)),
        ],
        out_shape=[
            jax.ShapeDtypeStruct((t, IN_SB), BF16),
            jax.ShapeDtypeStruct((t, MLA_PAD), BF16),
            jax.ShapeDtypeStruct((t, MLA_PAD), BF16),
            jax.ShapeDtypeStruct((MLA_VT_WIDTH, t), BF16),
        ],
        compiler_params=_params("arbitrary"),
        name="in_projection",
    )(x2, norm_g.reshape(1, d), w_in_pad, q_lat_g.reshape(1, -1), wq_pad, kv_lat_g.reshape(1, -1), wkv_pad,
      ctab, stab, pad_g(q_g), swap_g(q_g), pad_g(k_g), swap_g(k_g))


SB_EXP_UNDERFLOW = 110.0


def _sb_kernel(q_ref, k_ref, v_ref, u_ref, o_ref, acc_ref, carry_ref, z_ref, *, tq, tk):
    qi = pl.program_id(2)
    lane = lax.broadcasted_iota(jnp.int32, (1, LANES), 1)
    q = q_ref[...]
    q_heads = [jnp.where((lane >= j * SB_HEAD_DIM) & (lane < (j + 1) * SB_HEAD_DIM), q, jnp.zeros_like(q))
               for j in range(2)]
    acc_ref[...] = jnp.zeros_like(acc_ref)
    carry_ref[...] = jnp.zeros_like(carry_ref)

    def logits(j, c, lo_row=0):
        start = pl.multiple_of(c * tk, tk)
        z_ref[j, lo_row:, :] = _dot_nt(q_heads[j][lo_row:], k_ref[pl.ds(start, tk), :])

    def absorb(j, c, diagonal, lo_row=0):
        start = pl.multiple_of(c * tk, tk)
        z = z_ref[j, lo_row:, :]
        log_fail = -(jnp.maximum(z, 0.0) + jnp.log(1.0 + jnp.exp(-jnp.abs(z))))
        if diagonal:
            key = start + lax.broadcasted_iota(jnp.int32, z.shape, 1)
            qry = qi * tq + lo_row + lax.broadcasted_iota(jnp.int32, z.shape, 0)
            earlier = key < qry
            log_fail = jnp.where(earlier, log_fail, 0.0)
        hi = log_fail.astype(BF16)
        lo = (log_fail - hi.astype(F32)).astype(BF16)
        log_stick = _dot(hi, u_ref[...]) + _dot(lo, u_ref[...])
        carry = carry_ref[j, lo_row:, :]
        w = jnp.exp(z + log_fail + log_stick + jnp.tile(carry, (1, tk // LANES)))
        if diagonal:
            w = jnp.where(earlier, w, 0.0)
        acc_ref[j, lo_row:, :] += _dot(w.astype(BF16), v_ref[pl.ds(start, tk), :])
        carry_ref[j, lo_row:, :] = carry + (log_stick[:, :1] + log_fail[:, :1])

    def sweep(c, diagonal, lo_row=0, next_lo_row=0):
        logits(1, c, lo_row)
        absorb(0, c, diagonal, lo_row)
        logits(0, jnp.maximum(c - 1, 0), next_lo_row)
        absorb(1, c, diagonal, lo_row)

    n_sub = tq // tk
    first = qi * n_sub + n_sub - 1
    logits(0, first, (n_sub - 1) * tk)
    for sub in reversed(range(n_sub)):
        sweep(qi * n_sub + sub, True, sub * tk, max(sub - 1, 0) * tk)

    def sticks_alive():
        return jnp.max(carry_ref[...]) > -SB_EXP_UNDERFLOW

    def earlier_chunk(state):
        c, _ = state
        sweep(c, False)
        return c - 1, sticks_alive()

    lax.while_loop(lambda state: (state[0] >= 0) & state[1], earlier_chunk,
                   (qi * (tq // tk) - 1, sticks_alive()))
    o_ref[...] = jnp.where(lane < SB_HEAD_DIM, acc_ref[0], acc_ref[1]).astype(o_ref.dtype)


def _sb_attention(qkv, tq, tk):
    b, s, _ = qkv.shape
    n_pairs = SB_WIDTH // LANES
    tri = (jnp.arange(tk)[:, None] > jnp.arange(tk)[None, :]).astype(BF16)
    return pl.pallas_call(
        functools.partial(_sb_kernel, tq=tq, tk=tk),
        grid=(b, n_pairs, s // tq),
        in_specs=[
            pl.BlockSpec((None, tq, LANES), lambda bi, hp, qi: (bi, qi, hp)),
            pl.BlockSpec((None, s, LANES), lambda bi, hp, qi: (bi, 0, n_pairs + hp)),
            pl.BlockSpec((None, s, LANES), lambda bi, hp, qi: (bi, 0, 2 * n_pairs + hp)),
            _const_spec((tk, tk)),
        ],
        out_specs=pl.BlockSpec((None, tq, LANES), lambda bi, hp, qi: (bi, qi, hp)),
        out_shape=jax.ShapeDtypeStruct((b, s, SB_WIDTH), BF16),
        scratch_shapes=[pltpu.VMEM((2, tq, LANES), F32), pltpu.VMEM((2, tq, LANES), F32),
                        pltpu.VMEM((2, tq, tk), F32)],
        compiler_params=_params("arbitrary", "arbitrary", "arbitrary"),
        name="stick_breaking_attention",
    )(qkv, qkv, qkv, tri)


def _mla_kernel(q_ref, k_ref, vt_ref, o_ref, acc_ref, m_ref, s_ref, *, tq):
    qi = pl.program_id(2)
    acc_ref[...] = jnp.zeros_like(acc_ref)
    m_ref[...] = jnp.full(m_ref.shape, -jnp.inf, F32)

    def scores(j, c):
        start = pl.multiple_of(c * tq, tq)
        s_ref[j] = _dot_nt(k_ref[pl.ds(start, tq), j * LANES:(j + 1) * LANES],
                           q_ref[:, j * LANES:(j + 1) * LANES])

    def update(j, c, diagonal):
        start = pl.multiple_of(c * tq, tq)
        s = s_ref[j]
        if diagonal:
            key = lax.broadcasted_iota(jnp.int32, (tq, tq), 0)
            qry = lax.broadcasted_iota(jnp.int32, (tq, tq), 1)
            s = jnp.where(key <= qry, s, -jnp.inf)
        m_prev = m_ref[j]
        m_new = jnp.maximum(m_prev, jnp.max(s, axis=0, keepdims=True))
        p = jnp.exp2(s - m_new)
        vt = vt_ref[j * MLA_V_ROWS:(j + 1) * MLA_V_ROWS, pl.ds(start, tq)]
        acc_ref[j] = jnp.exp2(m_prev - m_new) * acc_ref[j] + _dot(vt, p.astype(BF16))
        m_ref[j] = m_new

    scores(0, 0)

    def earlier_block(c, _):
        scores(1, c)
        update(0, c, False)
        scores(0, c + 1)
        update(1, c, False)
        return 0

    lax.fori_loop(0, qi, earlier_block, 0)
    scores(1, qi)
    update(0, qi, True)
    update(1, qi, True)
    out_t = jnp.concatenate([acc_ref[j, :MLA_V_DIM, :] / acc_ref[j, MLA_V_DIM:MLA_V_DIM + 1, :] for j in range(2)],
                            axis=0)
    o_ref[...] = out_t.T.astype(o_ref.dtype)


def _mla_attention(q, k, vt, tq):
    b, s, _ = q.shape
    n_pairs = MLA_WIDTH // LANES
    return pl.pallas_call(
        functools.partial(_mla_kernel, tq=tq),
        grid=(b, n_pairs, s // tq),
        in_specs=[
            pl.BlockSpec((None, tq, 2 * LANES), lambda bi, hp, qi: (bi, qi, hp)),
            pl.BlockSpec((None, s, 2 * LANES), lambda bi, hp, qi: (bi, 0, hp)),
            pl.BlockSpec((2 * MLA_V_ROWS, s), lambda bi, hp, qi: (hp, bi)),
        ],
        out_specs=pl.BlockSpec((None, tq, LANES), lambda bi, hp, qi: (bi, qi, hp)),
        out_shape=jax.ShapeDtypeStruct((b, s, MLA_WIDTH), BF16),
        scratch_shapes=[pltpu.VMEM((2, MLA_V_ROWS, tq), F32), pltpu.VMEM((2, 1, tq), F32),
                        pltpu.VMEM((2, tq, tq), F32)],
        compiler_params=_params("arbitrary", "arbitrary", "arbitrary"),
        name="latent_attention",
    )(q, k, vt)


def _outproj_kernel(x_ref, osb_ref, omla_ref, wsb_ref, wmla_ref, g_ref, h_ref, hn_ref):
    h = x_ref[...] + _dot(osb_ref[...], wsb_ref[...]) + _dot(omla_ref[...], wmla_ref[...])
    h_ref[...] = h
    hn_ref[...] = (_rms(h, h.shape[-1]) * g_ref[...]).astype(BF16)


def _out_projection(x2, o_sb, o_mla, w_out, next_norm_g, tm):
    t, d = x2.shape
    row = lambda i: (i, 0)
    w_sb = w_out[:SB_WIDTH].astype(BF16)
    w_mla = w_out[SB_WIDTH:].astype(BF16)
    return pl.pallas_call(
        _outproj_kernel,
        grid=(t // tm,),
        in_specs=[pl.BlockSpec((tm, d), row), pl.BlockSpec((tm, SB_WIDTH), row), pl.BlockSpec((tm, MLA_WIDTH), row),
                  _const_spec((SB_WIDTH, d)), _const_spec((MLA_WIDTH, d)), _const_spec((1, d))],
        out_specs=[pl.BlockSpec((tm, d), row), pl.BlockSpec((tm, d), row)],
        out_shape=[jax.ShapeDtypeStruct((t, d), F32), jax.ShapeDtypeStruct((t, d), BF16)],
        compiler_params=_params("arbitrary"),
        name="out_projection",
    )(x2, o_sb, o_mla, w_sb, w_mla, next_norm_g.reshape(1, d))


def _swiglu(x, wg_ref, wu_ref, wd_ref, f_chunk):
    d_ff = wg_ref.shape[-1]
    acc = None
    for c in range(d_ff // f_chunk):
        cols = slice(c * f_chunk, (c + 1) * f_chunk)
        gate = _dot(x, wg_ref[:, cols])
        up = _dot(x, wu_ref[:, cols])
        act = (gate * jax.nn.sigmoid(gate) * up).astype(BF16)
        part = _dot(act, wd_ref[cols, :])
        acc = part if acc is None else acc + part
    return acc


def _dense_ffn_kernel(h_ref, hn_ref, wg_ref, wu_ref, wd_ref, o_ref, *, f_chunk):
    o_ref[...] = h_ref[...] + _swiglu(hn_ref[...], wg_ref, wu_ref, wd_ref, f_chunk)


def _dense_ffn(h, hn, w_gate, w_up, w_down, tm, f_chunk):
    t, d = h.shape
    d_ff = w_gate.shape[-1]
    row = lambda i: (i, 0)
    resident = lambda shape: pl.BlockSpec(shape, lambda i: (0, 0), pipeline_mode=pl.Buffered(1))
    return pl.pallas_call(
        functools.partial(_dense_ffn_kernel, f_chunk=f_chunk),
        grid=(t // tm,),
        in_specs=[pl.BlockSpec((tm, d), row), pl.BlockSpec((tm, d), row),
                  resident((d, d_ff)), resident((d, d_ff)), resident((d_ff, d))],
        out_specs=pl.BlockSpec((tm, d), row),
        out_shape=jax.ShapeDtypeStruct((t, d), F32),
        compiler_params=_params("arbitrary"),
        name="dense_swiglu",
    )(h, hn, w_gate.astype(BF16), w_up.astype(BF16), w_down.astype(BF16))


DENSE_FFN_TILE = 512


def _layer0(x2, b, s, att_norm, att_w_in, q_lat_g, w_q_up, kv_lat_g, w_kv_up, q_g, k_g, w_out,
            dffn_norm, w_gate, w_up, w_down, tm, tq, tk, f_chunk):
    qkv, qm, km, vm = _in_projection(x2, s, att_norm, att_w_in, q_lat_g, w_q_up, kv_lat_g, w_kv_up, q_g, k_g, tm)
    o_sb = _sb_attention(qkv.reshape(b, s, -1), tq, tk).reshape(b * s, -1)
    o_mla = _mla_attention(qm.reshape(b, s, -1), km.reshape(b, s, -1), vm, tq).reshape(b * s, -1)
    h1, h1n = _out_projection(x2, o_sb, o_mla, w_out, dffn_norm, tm)
    return _dense_ffn(h1, h1n, w_gate, w_up, w_down, DENSE_FFN_TILE, f_chunk)


SSM_SEGS = 8
SSM_SEG_LEN = 64
SSM_TILE = SSM_SEGS * SSM_SEG_LEN
SSM_CH_CHUNK = LANES
SSM_ST_CHUNK = SSM_CH_CHUNK // SSM_GROUP * SSM_STATE
SSM_SCAN_TILES = 8


def _s5_prep_kernel(are_ref, aim_ref, logdt_ref, bre_ref, bim_ref, cre_ref, cim_ref,
                    wbu_ref, wc_ref, pow_ref):
    n = SSM_NSTATE
    a_re = jnp.minimum(are_ref[...], EIG_RE_MAX)
    a_im = aim_ref[...]
    dt = jnp.exp(logdt_ref[...])
    mag = jnp.exp(a_re * dt)
    lam_re = mag * jnp.cos(a_im * dt)
    lam_im = mag * jnp.sin(a_im * dt)
    den = a_re * a_re + a_im * a_im
    num_re = lam_re - 1.0
    coef_re = (num_re * a_re + lam_im * a_im) / den
    coef_im = (lam_im * a_re - num_re * a_im) / den
    b_re = bre_ref[...]
    b_im = bim_ref[...]
    in_group = (lax.broadcasted_iota(jnp.int32, (SSM_CH_CHUNK, n), 0) // SSM_GROUP
                == lax.broadcasted_iota(jnp.int32, (SSM_CH_CHUNK, n), 1) % SSM_ST_CHUNK // SSM_STATE)
    wbu_ref[:, :n] = jnp.where(in_group, coef_re * b_re - coef_im * b_im, 0.0).astype(BF16)
    wbu_ref[:, n:] = jnp.where(in_group, coef_re * b_im + coef_im * b_re, 0.0).astype(BF16)
    in_group_t = (lax.broadcasted_iota(jnp.int32, (n, SSM_CH_CHUNK), 0) % SSM_ST_CHUNK // SSM_STATE
                  == lax.broadcasted_iota(jnp.int32, (n, SSM_CH_CHUNK), 1) // SSM_GROUP)
    wc_ref[:n, :] = jnp.where(in_group_t, cre_ref[...], 0.0).astype(BF16)
    wc_ref[n:, :] = jnp.where(in_group_t, -cim_ref[...], 0.0).astype(BF16)
    steps = jnp.where(lax.broadcasted_iota(jnp.int32, (SSM_SEGS, n), 0) == 0, 1.0, float(SSM_SEG_LEN))
    mag_k = jnp.exp(steps * (a_re * dt))
    ang_k = steps * (a_im * dt)
    pow_ref[:, :n] = mag_k * jnp.cos(ang_k)
    pow_ref[:, n:] = mag_k * jnp.sin(ang_k)


def _s5_prepare(a_re, a_im, log_dt, b_re, b_im, c_re, c_im):
    n = SSM_NSTATE
    row = lambda a: a.reshape(1, n)
    reps = SSM_CH_CHUNK // SSM_GROUP
    b_t = lambda w: jnp.tile(w.transpose(2, 0, 1).reshape(SSM_GROUP, n), (reps, 1))
    c_t = lambda w: jnp.tile(w.transpose(0, 2, 1).reshape(n, SSM_GROUP), (1, reps))
    return pl.pallas_call(
        _s5_prep_kernel,
        out_shape=[jax.ShapeDtypeStruct((SSM_CH_CHUNK, 2 * n), BF16),
                   jax.ShapeDtypeStruct((2 * n, SSM_CH_CHUNK), BF16),
                   jax.ShapeDtypeStruct((SSM_SEGS, 2 * n), F32)],
        compiler_params=pltpu.CompilerParams(vmem_limit_bytes=VMEM_LIMIT),
        name="s5_discretize",
    )(row(a_re), row(a_im), row(jnp.repeat(log_dt, SSM_STATE)), b_t(b_re), b_t(b_im), c_t(c_re), c_t(c_im))


def _s5_kernel(h_ref, g_ref, win_ref, wbu_ref, pow_ref, wc_ref, d_ref, wglu_ref, o_ref,
               ut_ref, up_ref, st_ref, carry_ref, yp_ref, yt_ref):
    n = SSM_NSTATE
    nt = n // LANES
    n_ch = SSM_WIDTH // LANES
    per = SSM_ST_CHUNK // LANES
    d_model = h_ref.shape[-1]

    @pl.when(pl.program_id(1) == 0)
    def _():
        carry_ref[...] = jnp.zeros_like(carry_ref)

    x = h_ref[...]
    xn = _rms(x, d_model) * g_ref[...]
    u = _dot(xn.astype(BF16), win_ref[...])
    for k in range(n_ch):
        ut_ref[k] = u[:, k * LANES:(k + 1) * LANES]

    def to_step_major(i, _):
        dst = pl.ds(pl.multiple_of(i * SSM_SEGS, SSM_SEGS), SSM_SEGS)
        for k in range(n_ch):
            up_ref[k, dst, :] = ut_ref[k, pl.ds(i, SSM_SEGS, stride=SSM_SEG_LEN), :]
        return 0

    lax.fori_loop(0, SSM_SEG_LEN, to_step_major, 0)

    for c in range(n_ch):
        u16 = up_ref[c].astype(BF16)
        for part in range(2):
            cols = slice(part * n + c * SSM_ST_CHUNK, part * n + (c + 1) * SSM_ST_CHUNK)
            bu = _dot(u16, wbu_ref[:, cols])
            for k in range(per):
                st_ref[part * nt + c * per + k] = bu[:, k * LANES:(k + 1) * LANES]

    for grp in range(nt // SSM_SCAN_TILES):
        tiles = range(grp * SSM_SCAN_TILES, (grp + 1) * SSM_SCAN_TILES)
        re_l = [slice(k * LANES, (k + 1) * LANES) for k in tiles]
        im_l = [slice(n + k * LANES, n + (k + 1) * LANES) for k in tiles]
        lam_re = [jnp.broadcast_to(pow_ref[0:1, c], (SSM_SEGS, LANES)) for c in re_l]
        lam_im = [jnp.broadcast_to(pow_ref[0:1, c], (SSM_SEGS, LANES)) for c in im_l]

        def scan(state, store):
            def step(i, state):
                rows = pl.ds(pl.multiple_of(i * SSM_SEGS, SSM_SEGS), SSM_SEGS)
                new_state = []
                for idx, k in enumerate(tiles):
                    s_re, s_im = state[idx]
                    n_re = lam_re[idx] * s_re - lam_im[idx] * s_im + st_ref[k, rows, :]
                    n_im = lam_re[idx] * s_im + lam_im[idx] * s_re + st_ref[nt + k, rows, :]
                    if store:
                        st_ref[k, rows, :] = n_re
                        st_ref[nt + k, rows, :] = n_im
                    new_state.append((n_re, n_im))
                return tuple(new_state)

            return lax.fori_loop(0, SSM_SEG_LEN, step, state)

        zero = jnp.zeros((SSM_SEGS, LANES), F32)
        ends = scan(tuple((zero, zero) for _ in tiles), False)
        starts = []
        for idx in range(len(tiles)):
            e_re, e_im = ends[idx]
            p_re = pow_ref[1:2, re_l[idx]]
            p_im = pow_ref[1:2, im_l[idx]]
            c_re = carry_ref[:, re_l[idx]]
            c_im = carry_ref[:, im_l[idx]]
            rows_re, rows_im = [], []
            for j in range(SSM_SEGS):
                rows_re.append(c_re)
                rows_im.append(c_im)
                c_re, c_im = (e_re[j:j + 1] + (p_re * c_re - p_im * c_im),
                              e_im[j:j + 1] + (p_re * c_im + p_im * c_re))
            carry_ref[:, re_l[idx]] = c_re
            carry_ref[:, im_l[idx]] = c_im
            starts.append((jnp.concatenate(rows_re, axis=0), jnp.concatenate(rows_im, axis=0)))
        scan(tuple(starts), True)

    for c in range(n_ch):
        h_re = jnp.concatenate([st_ref[c * per + k] for k in range(per)], axis=-1).astype(BF16)
        h_im = jnp.concatenate([st_ref[nt + c * per + k] for k in range(per)], axis=-1).astype(BF16)
        rows_re = slice(c * SSM_ST_CHUNK, (c + 1) * SSM_ST_CHUNK)
        rows_im = slice(n + c * SSM_ST_CHUNK, n + (c + 1) * SSM_ST_CHUNK)
        yp_ref[c] = _dot(h_re, wc_ref[rows_re, :]) + _dot(h_im, wc_ref[rows_im, :])

    def to_token_major(i, _):
        src = pl.ds(pl.multiple_of(i * SSM_SEGS, SSM_SEGS), SSM_SEGS)
        for k in range(n_ch):
            yt_ref[k, pl.ds(i, SSM_SEGS, stride=SSM_SEG_LEN), :] = yp_ref[k, src, :]
        return 0

    lax.fori_loop(0, SSM_SEG_LEN, to_token_major, 0)
    y = jnp.concatenate([yt_ref[k] for k in range(n_ch)], axis=-1)
    y = jax.nn.gelu(y + d_ref[...] * u)
    z = _dot(y.astype(BF16), wglu_ref[...])
    o_ref[...] = x + z[:, :d_model] * jax.nn.sigmoid(z[:, d_model:])


def _s5_layer(h, b, s, norm_g, w_in, a_re, a_im, log_dt, b_re, b_im, c_re, c_im, d_skip, w_glu):
    t, d = h.shape
    n = SSM_NSTATE
    w_bu, w_c, powers = _s5_prepare(a_re, a_im, log_dt, b_re, b_im, c_re, c_im)
    tiles = s // SSM_TILE
    resident = lambda shape: pl.BlockSpec(shape, lambda bi, ti: (0, 0), pipeline_mode=pl.Buffered(1))
    row = lambda bi, ti: (bi * tiles + ti, 0)
    ch_tiles = pltpu.VMEM((SSM_WIDTH // LANES, SSM_TILE, LANES), F32)
    return pl.pallas_call(
        _s5_kernel,
        grid=(b, tiles),
        in_specs=[pl.BlockSpec((SSM_TILE, d), row), resident((1, d)), resident((d, SSM_WIDTH)),
                  resident((SSM_CH_CHUNK, 2 * n)), resident((SSM_SEGS, 2 * n)), resident((2 * n, SSM_CH_CHUNK)),
                  resident((1, SSM_WIDTH)), resident((SSM_WIDTH, 2 * d))],
        out_specs=pl.BlockSpec((SSM_TILE, d), row),
        out_shape=jax.ShapeDtypeStruct((t, d), F32),
        scratch_shapes=[ch_tiles, ch_tiles, pltpu.VMEM((2 * n // LANES, SSM_TILE, LANES), F32),
                        pltpu.VMEM((1, 2 * n), F32), ch_tiles, ch_tiles],
        compiler_params=_params("arbitrary", "arbitrary"),
        name="s5_mixer",
    )(h, norm_g.reshape(1, d), w_in.astype(BF16), w_bu, powers, w_c, d_skip.reshape(1, -1), w_glu.astype(BF16))


MOE_BLOCK = 256
META_E1, META_E2, META_G1, META_G2, META_R1, META_R2 = range(6)


def _router_kernel(h_ref, g_ref, wr_ref, tri_ref, xn_ref, meta_ref, counts_ref, run_ref):
    @pl.when(pl.program_id(0) == 0)
    def _():
        run_ref[...] = jnp.zeros_like(run_ref)

    x = h_ref[...]
    xn = _rms(x, x.shape[-1]) * g_ref[...]
    xn_ref[...] = xn
    logits = jnp.dot(xn, wr_ref[...], preferred_element_type=F32, precision=lax.Precision.HIGHEST)
    tm = logits.shape[0]
    e = lax.broadcasted_iota(jnp.int32, (tm, N_EXPERTS), 1)
    v1 = jnp.max(logits, axis=-1, keepdims=True)
    i1 = jnp.min(jnp.where(logits == v1, e, N_EXPERTS), axis=-1, keepdims=True)
    rest = jnp.where(e == i1, -jnp.inf, logits)
    v2 = jnp.max(rest, axis=-1, keepdims=True)
    i2 = jnp.min(jnp.where(rest == v2, e, N_EXPERTS), axis=-1, keepdims=True)
    ex = jnp.exp(v2 - v1)
    g1 = 1.0 / (1.0 + ex)
    g2 = ex / (1.0 + ex)
    oh1 = (e == i1).astype(F32)
    oh2 = (e == i2).astype(F32)
    both = oh1 + oh2
    before = _dot(tri_ref[...], both.astype(BF16)) + run_ref[...]
    r1 = jnp.sum(oh1 * before, axis=-1, keepdims=True)
    r2 = jnp.sum(oh2 * before, axis=-1, keepdims=True)
    run_ref[...] += jnp.sum(both, axis=0, keepdims=True)
    counts_ref[...] = run_ref[...]
    meta = jnp.zeros((tm, N_EXPERTS), F32)
    for lane, val in ((META_E1, i1.astype(F32)), (META_E2, i2.astype(F32)), (META_G1, g1), (META_G2, g2),
                      (META_R1, r1), (META_R2, r2)):
        meta = jnp.where(e == lane, val, meta)
    meta_ref[...] = meta


def _router(h, norm_g, w_router, tm):
    t, d = h.shape
    row = lambda i: (i, 0)
    tri = (jnp.arange(tm)[:, None] > jnp.arange(tm)[None, :]).astype(BF16)
    return pl.pallas_call(
        _router_kernel,
        grid=(t // tm,),
        in_specs=[pl.BlockSpec((tm, d), row), _const_spec((1, d)), _const_spec((d, N_EXPERTS)),
                  _const_spec((tm, tm))],
        out_specs=[pl.BlockSpec((tm, d), row), pl.BlockSpec((tm, N_EXPERTS), row), _const_spec((1, N_EXPERTS))],
        out_shape=[jax.ShapeDtypeStruct((t, d), F32), jax.ShapeDtypeStruct((t, N_EXPERTS), F32),
                   jax.ShapeDtypeStruct((1, N_EXPERTS), F32)],
        scratch_shapes=[pltpu.VMEM((1, N_EXPERTS), F32)],
        compiler_params=_params("arbitrary"),
        name="moe_router",
    )(h, norm_g.reshape(1, d), w_router, tri)


def _slot_map_kernel(d1_ref, d2_ref, tok_ref):
    def clear(slot, _):
        tok_ref[slot] = 0
        return 0

    lax.fori_loop(0, tok_ref.shape[0], clear, 0, unroll=16)

    def place(tok, _):
        tok_ref[d1_ref[tok]] = tok
        tok_ref[d2_ref[tok]] = tok
        return 0

    lax.fori_loop(0, d1_ref.shape[0], place, 0, unroll=8)


def _slot_map(dest1, dest2, n_slots):
    smem = pl.BlockSpec(memory_space=pltpu.SMEM)
    return pl.pallas_call(
        _slot_map_kernel,
        in_specs=[smem, smem], out_specs=smem,
        out_shape=jax.ShapeDtypeStruct((n_slots,), jnp.int32),
        name="moe_slot_map",
    )(dest1, dest2)


def _row_copy(src_ref, src_row, dst_ref, dst_row, sem):
    return pltpu.make_async_copy(src_ref.at[pl.ds(src_row, 1)], dst_ref.at[pl.ds(dst_row, 1)], sem)


def _expert_ffn_kernel(be_ref, bv_ref, tok_ref, xn_ref, wg_ref, wu_ref, wd_ref, ys_ref, buf_ref, sem, *, f_chunk):
    del be_ref
    i = pl.program_id(0)
    last = pl.num_programs(0) - 1
    cur = i % 2

    def gather(block, buf):
        base = block * MOE_BLOCK
        for r in range(MOE_BLOCK):
            _row_copy(xn_ref, tok_ref[base + r], buf_ref.at[buf], r, sem.at[buf]).start()

    def wait_gather(buf):
        pltpu.make_async_copy(xn_ref.at[pl.ds(0, MOE_BLOCK)], buf_ref.at[buf], sem.at[buf]).wait()

    @pl.when(i == 0)
    def _():
        gather(0, 0)

    gather(jnp.minimum(i + 1, last), 1 - cur)
    wait_gather(cur)
    valid = bv_ref[i] != 0

    @pl.when(valid)
    def _():
        ys_ref[...] = _swiglu(buf_ref[cur].astype(BF16), wg_ref, wu_ref, wd_ref, f_chunk)

    @pl.when(jnp.logical_not(valid))
    def _():
        ys_ref[...] = jnp.zeros_like(ys_ref)

    @pl.when(i == last)
    def _():
        wait_gather(1 - cur)


def _expert_ffn(xn, slot_tok, blk_expert, blk_valid, w_gate, w_up, w_down, f_chunk):
    n_slots = slot_tok.shape[0]
    d = xn.shape[-1]
    d_ff = w_gate.shape[-1]
    return pl.pallas_call(
        functools.partial(_expert_ffn_kernel, f_chunk=f_chunk),
        grid_spec=pltpu.PrefetchScalarGridSpec(
            num_scalar_prefetch=3, grid=(n_slots // MOE_BLOCK,),
            in_specs=[pl.BlockSpec(memory_space=pl.ANY),
                      pl.BlockSpec((None, d, d_ff), lambda i, be, bv, st: (be[i], 0, 0)),
                      pl.BlockSpec((None, d, d_ff), lambda i, be, bv, st: (be[i], 0, 0)),
                      pl.BlockSpec((None, d_ff, d), lambda i, be, bv, st: (be[i], 0, 0))],
            out_specs=pl.BlockSpec((MOE_BLOCK, d), lambda i, be, bv, st: (i, 0)),
            scratch_shapes=[pltpu.VMEM((2, MOE_BLOCK, d), F32), pltpu.SemaphoreType.DMA((2,))]),
        out_shape=jax.ShapeDtypeStruct((n_slots, d), F32),
        compiler_params=_params("arbitrary"),
        name="expert_swiglu",
    )(blk_expert, blk_valid, slot_tok, xn, w_gate.astype(BF16), w_up.astype(BF16), w_down.astype(BF16))


def _combine_kernel(d1_ref, d2_ref, h_ref, meta_ref, ys_ref, o_ref, a_ref, b_ref, sem, *, tm):
    base = pl.program_id(0) * tm

    def issue(r, _):
        tok = base + r
        _row_copy(ys_ref, d1_ref[tok], a_ref, r, sem).start()
        _row_copy(ys_ref, d2_ref[tok], b_ref, r, sem).start()
        return 0

    lax.fori_loop(0, tm, issue, 0, unroll=8)

    pltpu.make_async_copy(ys_ref.at[pl.ds(0, tm)], a_ref, sem).wait()
    pltpu.make_async_copy(ys_ref.at[pl.ds(0, tm)], b_ref, sem).wait()
    meta = meta_ref[...]
    g1 = meta[:, META_G1:META_G1 + 1]
    g2 = meta[:, META_G2:META_G2 + 1]
    o_ref[...] = h_ref[...] + (g1 * a_ref[...] + g2 * b_ref[...])


def _combine(h, meta, ys, dest1, dest2, tm):
    t, d = h.shape
    row = lambda i, d1, d2: (i, 0)
    return pl.pallas_call(
        functools.partial(_combine_kernel, tm=tm),
        grid_spec=pltpu.PrefetchScalarGridSpec(
            num_scalar_prefetch=2, grid=(t // tm,),
            in_specs=[pl.BlockSpec((tm, d), row), pl.BlockSpec((tm, N_EXPERTS), row),
                      pl.BlockSpec(memory_space=pl.ANY)],
            out_specs=pl.BlockSpec((tm, d), row),
            scratch_shapes=[pltpu.VMEM((tm, d), F32), pltpu.VMEM((tm, d), F32), pltpu.SemaphoreType.DMA(())]),
        out_shape=jax.ShapeDtypeStruct((t, d), F32),
        compiler_params=_params("arbitrary"),
        name="moe_combine",
    )(dest1, dest2, h, meta, ys)


def _moe_layer(h, norm_g, w_router, w_gate, w_up, w_down, tm, f_chunk):
    t, d = h.shape
    xn, meta, counts = _router(h, norm_g, w_router, tm)
    counts = counts.reshape(N_EXPERTS).astype(jnp.int32)
    padded = (counts + MOE_BLOCK - 1) // MOE_BLOCK * MOE_BLOCK
    pad_end = jnp.cumsum(padded)
    pad_start = pad_end - padded
    e1 = meta[:, META_E1].astype(jnp.int32)
    e2 = meta[:, META_E2].astype(jnp.int32)
    dest1 = pad_start[e1] + meta[:, META_R1].astype(jnp.int32)
    dest2 = pad_start[e2] + meta[:, META_R2].astype(jnp.int32)
    n_slots = 2 * t + N_EXPERTS * MOE_BLOCK
    blk_start = jnp.arange(n_slots // MOE_BLOCK, dtype=jnp.int32) * MOE_BLOCK
    blk_expert = jnp.minimum(jnp.sum(blk_start[:, None] >= pad_end[None, :], axis=1), N_EXPERTS - 1).astype(jnp.int32)
    blk_valid = (blk_start < pad_end[-1]).astype(jnp.int32)
    slot_tok = _slot_map(dest1, dest2, n_slots)
    ys = _expert_ffn(xn, slot_tok, blk_expert, blk_valid, w_gate, w_up, w_down, f_chunk)
    return _combine(h, meta, ys, dest1, dest2, tm)


def kernel(x, att_norm, att_w_in, att_q_latent_norm, att_w_q_up, att_kv_latent_norm, att_w_kv_up, att_q_norm, att_k_norm, att_w_out, dffn_norm, dffn_w_gate, dffn_w_up, dffn_w_down, ssm_norm, ssm_w_in, ssm_a_re, ssm_a_im, ssm_log_dt, ssm_b_re, ssm_b_im, ssm_c_re, ssm_c_im, ssm_d, ssm_w_glu, moe_norm, moe_router, moe_w_gate, moe_w_up, moe_w_down):
    b, s, d = x.shape
    x2 = x.reshape(b * s, d)
    h = _layer0(x2, b, s, att_norm[0], att_w_in[0], att_q_latent_norm[0], att_w_q_up[0], att_kv_latent_norm[0],
                att_w_kv_up[0], att_q_norm[0], att_k_norm[0], att_w_out[0], dffn_norm[0], dffn_w_gate[0],
                dffn_w_up[0], dffn_w_down[0], tm=512, tq=512, tk=256, f_chunk=512)
    h = _s5_layer(h, b, s, ssm_norm[0], ssm_w_in[0], ssm_a_re[0], ssm_a_im[0], ssm_log_dt[0], ssm_b_re[0],
                  ssm_b_im[0], ssm_c_re[0], ssm_c_im[0], ssm_d[0], ssm_w_glu[0])
    h = _moe_layer(h, moe_norm[0], moe_router[0], moe_w_gate[0], moe_w_up[0], moe_w_down[0], tm=256, f_chunk=512)
    return h.reshape(b, s, d)
```

```python
import functools
import math

import jax
import jax.numpy as jnp
from jax import lax
from jax.experimental import pallas as pl
from jax.experimental.pallas import tpu as pltpu

F32 = jnp.float32
BF16 = jnp.bfloat16

EPS = 1e-6
LANES = 128
SB_HEADS = 8
SB_HEAD_DIM = 64
SB_WIDTH = SB_HEADS * SB_HEAD_DIM
MLA_HEADS = 8
MLA_Q_RANK = 256
MLA_KV_RANK = 128
MLA_NOPE_DIM = 64
MLA_ROPE_DIM = 32
MLA_QK_DIM = MLA_NOPE_DIM + MLA_ROPE_DIM
MLA_V_DIM = 64
MLA_WIDTH = MLA_HEADS * MLA_V_DIM
ROPE_THETA = 10000.0
SSM_GROUP = 16
SSM_GROUPS = 32
SSM_STATE = 64
SSM_WIDTH = SSM_GROUP * SSM_GROUPS
SSM_NSTATE = SSM_GROUPS * SSM_STATE
EIG_RE_MAX = -1e-4
N_EXPERTS = 8
VMEM_LIMIT = 56 * 1024 * 1024


def _params(*sem):
    return pltpu.CompilerParams(dimension_semantics=sem, vmem_limit_bytes=VMEM_LIMIT)


def _rms(x, n):
    return x * lax.rsqrt(jnp.sum(x * x, axis=-1, keepdims=True) * (1.0 / n) + EPS)


def _dot(a, b):
    return jnp.dot(a, b, preferred_element_type=F32)


def _dot_nt(a, b):
    return lax.dot_general(a, b, (((1,), (1,)), ((), ())), preferred_element_type=F32)


def _const_spec(shape):
    return pl.BlockSpec(shape, lambda *_: (0,) * len(shape))


IN_SB = 3 * SB_WIDTH
IN_CQ = IN_SB
IN_CKV = IN_CQ + MLA_Q_RANK
IN_KR = IN_CKV + MLA_KV_RANK
IN_KRS = IN_KR + LANES
IN_COLS_PAD = IN_KRS + LANES
MLA_PAD = MLA_HEADS * LANES
MLA_V_ROWS = MLA_V_DIM + 16
MLA_VT_WIDTH = MLA_HEADS * MLA_V_ROWS


def _inproj_kernel(x_ref, g_ref, win_ref, qlg_ref, wq_ref, kvlg_ref, wkv_ref, ctab_ref, stab_ref,
                   qg_ref, qgs_ref, kg_ref, kgs_ref, qkv_ref, qm_ref, km_ref, vm_ref):
    x = x_ref[...]
    xn = _rms(x, x.shape[-1]) * g_ref[...]
    hh = _dot(xn.astype(BF16), win_ref[...])
    qkv_ref[:, :SB_WIDTH] = (hh[:, :SB_WIDTH] * (SB_HEAD_DIM ** -0.5)).astype(BF16)
    qkv_ref[:, SB_WIDTH:] = hh[:, SB_WIDTH:IN_SB].astype(BF16)
    cq = _rms(hh[:, IN_CQ:IN_CKV], MLA_Q_RANK) * qlg_ref[...]
    qf = _dot(cq.astype(BF16), wq_ref[...])
    ckv = _rms(hh[:, IN_CKV:IN_KR], MLA_KV_RANK) * kvlg_ref[...]
    kvf = _dot(ckv.astype(BF16), wkv_ref[...])
    v_col = lax.broadcasted_iota(jnp.int32, (1, MLA_VT_WIDTH), 1)
    v_ones = jnp.where(v_col % MLA_V_ROWS >= MLA_V_DIM, 1.0, 0.0)
    vm_ref[...] = (kvf[:, MLA_PAD:] + v_ones).T.astype(BF16)
    kr = hh[:, IN_KR:IN_KRS]
    krs = hh[:, IN_KRS:IN_COLS_PAD]
    ctab = ctab_ref[...]
    stab = stab_ref[...]
    scale = MLA_QK_DIM ** -0.5 * math.log2(math.e)
    cq_t = ctab * (qg_ref[...] * scale)
    sq_t = stab * (qgs_ref[...] * scale)
    ck_t = ctab * kg_ref[...]
    sk_t = stab * kgs_ref[...]
    for h in range(MLA_HEADS):
        lo, hi = h * LANES, (h + 1) * LANES
        qh = qf[:, lo:hi]
        qs = qf[:, MLA_PAD + lo:MLA_PAD + hi]
        q_inv = lax.rsqrt(jnp.sum(qh * qh, axis=-1, keepdims=True) * (1.0 / MLA_QK_DIM) + EPS)
        qm_ref[:, lo:hi] = ((qh * cq_t + qs * sq_t) * q_inv).astype(BF16)
        kh = kvf[:, lo:hi] + kr
        k_inv = lax.rsqrt(jnp.sum(kh * kh, axis=-1, keepdims=True) * (1.0 / MLA_QK_DIM) + EPS)
        km_ref[:, lo:hi] = ((kh * ck_t + krs * sk_t) * k_inv).astype(BF16)


def _pad_heads(w, n_heads, width):
    k = w.shape[0]
    w = w.reshape(k, n_heads, width)
    return jnp.pad(w, ((0, 0), (0, 0), (0, LANES - width))).reshape(k, n_heads * LANES)


def _swap_rope(w):
    half = MLA_ROPE_DIM // 2
    return jnp.concatenate([jnp.zeros_like(w[..., :MLA_NOPE_DIM]),
                            w[..., MLA_NOPE_DIM + half:], w[..., MLA_NOPE_DIM:MLA_NOPE_DIM + half]], axis=-1)


def _rope_lane_tables(s_len):
    half = MLA_ROPE_DIM // 2
    inv_freq = ROPE_THETA ** (-jnp.arange(0, MLA_ROPE_DIM, 2, dtype=F32) / MLA_ROPE_DIM)
    ang = jnp.arange(s_len, dtype=F32)[:, None] * inv_freq[None, :]
    cos, sin = jnp.cos(ang), jnp.sin(ang)
    ones = jnp.ones((s_len, MLA_NOPE_DIM), F32)
    zeros_n = jnp.zeros((s_len, MLA_NOPE_DIM), F32)
    zeros_t = jnp.zeros((s_len, LANES - MLA_QK_DIM), F32)
    ctab = jnp.concatenate([ones, cos, cos, zeros_t], axis=-1)
    stab = jnp.concatenate([zeros_n, -sin, sin, zeros_t], axis=-1)
    del half
    return ctab, stab


def _in_projection(x2, s_len, norm_g, w_in, q_lat_g, w_q_up, kv_lat_g, w_kv_up, q_g, k_g, tm):
    t, d = x2.shape
    w_kr = w_in[:, IN_KR - LANES + 0:][:, :0]
    del w_kr
    off_kr = 3 * SB_WIDTH + MLA_Q_RANK + MLA_KV_RANK
    w_rope = w_in[:, off_kr:]
    zeros_n = jnp.zeros((d, MLA_NOPE_DIM), F32)
    zeros_t = jnp.zeros((d, LANES - MLA_QK_DIM), F32)
    half = MLA_ROPE_DIM // 2
    w_in_pad = jnp.concatenate(
        [w_in[:, :off_kr], zeros_n, w_rope, zeros_t,
         zeros_n, w_rope[:, half:], w_rope[:, :half], zeros_t], axis=-1).astype(BF16)
    wq3 = w_q_up.reshape(MLA_Q_RANK, MLA_HEADS, MLA_QK_DIM)
    wq_pad = jnp.concatenate(
        [_pad_heads(w_q_up, MLA_HEADS, MLA_QK_DIM),
         _pad_heads(_swap_rope(wq3).reshape(MLA_Q_RANK, -1), MLA_HEADS, MLA_QK_DIM)], axis=-1).astype(BF16)
    wkv3 = w_kv_up.reshape(MLA_KV_RANK, MLA_HEADS, MLA_NOPE_DIM + MLA_V_DIM)
    wkv_pad = jnp.concatenate(
        [_pad_heads(wkv3[..., :MLA_NOPE_DIM].reshape(MLA_KV_RANK, -1), MLA_HEADS, MLA_NOPE_DIM),
         jnp.pad(wkv3[..., MLA_NOPE_DIM:], ((0, 0), (0, 0), (0, MLA_V_ROWS - MLA_V_DIM))).reshape(MLA_KV_RANK, -1)],
        axis=-1).astype(BF16)
    ctab, stab = _rope_lane_tables(s_len)
    pad_g = lambda g: jnp.pad(g, (0, LANES - MLA_QK_DIM)).reshape(1, LANES)
    swap_g = lambda g: jnp.pad(jnp.concatenate(
        [jnp.zeros((MLA_NOPE_DIM,), F32), g[MLA_NOPE_DIM + half:], g[MLA_NOPE_DIM:MLA_NOPE_DIM + half]]),
        (0, LANES - MLA_QK_DIM)).reshape(1, LANES)
    n_pos = s_len // tm
    row = lambda i: (i, 0)
    return pl.pallas_call(
        _inproj_kernel,
        grid=(t // tm,),
        in_specs=[
            pl.BlockSpec((tm, d), row),
            _const_spec((1, d)),
            _const_spec((d, IN_COLS_PAD)),
            _const_spec((1, MLA_Q_RANK)),
            _const_spec((MLA_Q_RANK, 2 * MLA_PAD)),
            _const_spec((1, MLA_KV_RANK)),
            _const_spec((MLA_KV_RANK, MLA_PAD + MLA_VT_WIDTH)),
            pl.BlockSpec((tm, LANES), lambda i: (i % n_pos, 0)),
            pl.BlockSpec((tm, LANES), lambda i: (i % n_pos, 0)),
            _const_spec((1, LANES)), _const_spec((1, LANES)),
            _const_spec((1, LANES)), _const_spec((1, LANES)),
        ],
        out_specs=[
            pl.BlockSpec((tm, IN_SB), row),
            pl.BlockSpec((tm, MLA_PAD), row),
            pl.BlockSpec((tm, MLA_PAD), row),
            pl.BlockSpec((MLA_VT_WIDTH, tm), lambda i: (0, i)),
        ],
        out_shape=[
            jax.ShapeDtypeStruct((t, IN_SB), BF16),
            jax.ShapeDtypeStruct((t, MLA_PAD), BF16),
            jax.ShapeDtypeStruct((t, MLA_PAD), BF16),
            jax.ShapeDtypeStruct((MLA_VT_WIDTH, t), BF16),
        ],
        compiler_params=_params("arbitrary"),
        name="in_projection",
    )(x2, norm_g.reshape(1, d), w_in_pad, q_lat_g.reshape(1, -1), wq_pad, kv_lat_g.reshape(1, -1), wkv_pad,
      ctab, stab, pad_g(q_g), swap_g(q_g), pad_g(k_g), swap_g(k_g))


SB_EXP_UNDERFLOW = 110.0


def _sb_kernel(q_ref, k_ref, v_ref, u_ref, o_ref, acc_ref, carry_ref, z_ref, *, tq, tk):
    qi = pl.program_id(2)
    lane = lax.broadcasted_iota(jnp.int32, (1, LANES), 1)
    q = q_ref[...]
    q_heads = [jnp.where((lane >= j * SB_HEAD_DIM) & (lane < (j + 1) * SB_HEAD_DIM), q, jnp.zeros_like(q))
               for j in range(2)]
    acc_ref[...] = jnp.zeros_like(acc_ref)
    carry_ref[...] = jnp.zeros_like(carry_ref)

    def logits(j, c, lo_row=0):
        start = pl.multiple_of(c * tk, tk)
        z_ref[j, lo_row:, :] = _dot_nt(q_heads[j][lo_row:], k_ref[pl.ds(start, tk), :])

    def absorb(j, c, diagonal, lo_row=0):
        start = pl.multiple_of(c * tk, tk)
        z = z_ref[j, lo_row:, :]
        log_fail = -(jnp.maximum(z, 0.0) + jnp.log(1.0 + jnp.exp(-jnp.abs(z))))
        if diagonal:
            key = start + lax.broadcasted_iota(jnp.int32, z.shape, 1)
            qry = qi * tq + lo_row + lax.broadcasted_iota(jnp.int32, z.shape, 0)
            earlier = key < qry
            log_fail = jnp.where(earlier, log_fail, 0.0)
        hi = log_fail.astype(BF16)
        lo = (log_fail - hi.astype(F32)).astype(BF16)
        log_stick = _dot(hi, u_ref[...]) + _dot(lo, u_ref[...])
        carry = carry_ref[j, lo_row:, :]
        w = jnp.exp(z + log_fail + log_stick + jnp.tile(carry, (1, tk // LANES)))
        if diagonal:
            w = jnp.where(earlier, w, 0.0)
        acc_ref[j, lo_row:, :] += _dot(w.astype(BF16), v_ref[pl.ds(start, tk), :])
        carry_ref[j, lo_row:, :] = carry + (log_stick[:, :1] + log_fail[:, :1])

    def sweep(c, diagonal, lo_row=0, next_lo_row=0):
        logits(1, c, lo_row)
        absorb(0, c, diagonal, lo_row)
        logits(0, jnp.maximum(c - 1, 0), next_lo_row)
        absorb(1, c, diagonal, lo_row)

    n_sub = tq // tk
    first = qi * n_sub + n_sub - 1
    logits(0, first, (n_sub - 1) * tk)
    for sub in reversed(range(n_sub)):
        sweep(qi * n_sub + sub, True, sub * tk, max(sub - 1, 0) * tk)

    def sticks_alive():
        return jnp.max(carry_ref[...]) > -SB_EXP_UNDERFLOW

    def earlier_chunk(state):
        c, _ = state
        sweep(c, False)
        return c - 1, sticks_alive()

    lax.while_loop(lambda state: (state[0] >= 0) & state[1], earlier_chunk,
                   (qi * (tq // tk) - 1, sticks_alive()))
    o_ref[...] = jnp.where(lane < SB_HEAD_DIM, acc_ref[0], acc_ref[1]).astype(o_ref.dtype)


def _sb_attention(qkv, tq, tk):
    b, s, _ = qkv.shape
    n_pairs = SB_WIDTH // LANES
    tri = (jnp.arange(tk)[:, None] > jnp.arange(tk)[None, :]).astype(BF16)
    return pl.pallas_call(
        functools.partial(_sb_kernel, tq=tq, tk=tk),
        grid=(b, n_pairs, s // tq),
        in_specs=[
            pl.BlockSpec((None, tq, LANES), lambda bi, hp, qi: (bi, qi, hp)),
            pl.BlockSpec((None, s, LANES), lambda bi, hp, qi: (bi, 0, n_pairs + hp)),
            pl.BlockSpec((None, s, LANES), lambda bi, hp, qi: (bi, 0, 2 * n_pairs + hp)),
            _const_spec((tk, tk)),
        ],
        out_specs=pl.BlockSpec((None, tq, LANES), lambda bi, hp, qi: (bi, qi, hp)),
        out_shape=jax.ShapeDtypeStruct((b, s, SB_WIDTH), BF16),
        scratch_shapes=[pltpu.VMEM((2, tq, LANES), F32), pltpu.VMEM((2, tq, LANES), F32),
                        pltpu.VMEM((2, tq, tk), F32)],
        compiler_params=_params("arbitrary", "arbitrary", "arbitrary"),
        name="stick_breaking_attention",
    )(qkv, qkv, qkv, tri)


def _mla_kernel(q_ref, k_ref, vt_ref, o_ref, acc_ref, m_ref, s_ref, *, tq):
    qi = pl.program_id(2)
    acc_ref[...] = jnp.zeros_like(acc_ref)
    m_ref[...] = jnp.full(m_ref.shape, -jnp.inf, F32)

    def scores(j, c):
        start = pl.multiple_of(c * tq, tq)
        s_ref[j] = _dot_nt(k_ref[pl.ds(start, tq), j * LANES:(j + 1) * LANES],
                           q_ref[:, j * LANES:(j + 1) * LANES])

    def update(j, c, diagonal):
        start = pl.multiple_of(c * tq, tq)
        s = s_ref[j]
        if diagonal:
            key = lax.broadcasted_iota(jnp.int32, (tq, tq), 0)
            qry = lax.broadcasted_iota(jnp.int32, (tq, tq), 1)
            s = jnp.where(key <= qry, s, -jnp.inf)
        m_prev = m_ref[j]
        m_new = jnp.maximum(m_prev, jnp.max(s, axis=0, keepdims=True))
        p = jnp.exp2(s - m_new)
        vt = vt_ref[j * MLA_V_ROWS:(j + 1) * MLA_V_ROWS, pl.ds(start, tq)]
        acc_ref[j] = jnp.exp2(m_prev - m_new) * acc_ref[j] + _dot(vt, p.astype(BF16))
        m_ref[j] = m_new

    scores(0, 0)

    def earlier_block(c, _):
        scores(1, c)
        update(0, c, False)
        scores(0, c + 1)
        update(1, c, False)
        return 0

    lax.fori_loop(0, qi, earlier_block, 0)
    scores(1, qi)
    update(0, qi, True)
    update(1, qi, True)
    out_t = jnp.concatenate([acc_ref[j, :MLA_V_DIM, :] / acc_ref[j, MLA_V_DIM:MLA_V_DIM + 1, :] for j in range(2)],
                            axis=0)
    o_ref[...] = out_t.T.astype(o_ref.dtype)


def _mla_attention(q, k, vt, tq):
    b, s, _ = q.shape
    n_pairs = MLA_WIDTH // LANES
    return pl.pallas_call(
        functools.partial(_mla_kernel, tq=tq),
        grid=(b, n_pairs, s // tq),
        in_specs=[
            pl.BlockSpec((None, tq, 2 * LANES), lambda bi, hp, qi: (bi, qi, hp)),
            pl.BlockSpec((None, s, 2 * LANES), lambda bi, hp, qi: (bi, 0, hp)),
            pl.BlockSpec((2 * MLA_V_ROWS, s), lambda bi, hp, qi: (hp, bi)),
        ],
        out_specs=pl.BlockSpec((None, tq, LANES), lambda bi, hp, qi: (bi, qi, hp)),
        out_shape=jax.ShapeDtypeStruct((b, s, MLA_WIDTH), BF16),
        scratch_shapes=[pltpu.VMEM((2, MLA_V_ROWS, tq), F32), pltpu.VMEM((2, 1, tq), F32),
                        pltpu.VMEM((2, tq, tq), F32)],
        compiler_params=_params("arbitrary", "arbitrary", "arbitrary"),
        name="latent_attention",
    )(q, k, vt)


def _swiglu(x, wg_ref, wu_ref, wd_ref, f_chunk):
    d_ff = wg_ref.shape[-1]
    acc = None
    for c in range(d_ff // f_chunk):
        cols = slice(c * f_chunk, (c + 1) * f_chunk)
        gate = _dot(x, wg_ref[:, cols])
        up = _dot(x, wu_ref[:, cols])
        act = (gate * jax.nn.sigmoid(gate) * up).astype(BF16)
        part = _dot(act, wd_ref[cols, :])
        acc = part if acc is None else acc + part
    return acc


def _outproj_ffn_kernel(x_ref, osb_ref, omla_ref, wsb_ref, wmla_ref, g_ref, wg_ref, wu_ref, wd_ref, o_ref, *, f_chunk):
    h = x_ref[...] + _dot(osb_ref[...], wsb_ref[...]) + _dot(omla_ref[...], wmla_ref[...])
    hn = (_rms(h, h.shape[-1]) * g_ref[...]).astype(BF16)
    o_ref[...] = h + _swiglu(hn, wg_ref, wu_ref, wd_ref, f_chunk)


def _outproj_dense_ffn(x2, o_sb, o_mla, w_out, norm_g, w_gate, w_up, w_down, tm, f_chunk):
    t, d = x2.shape
    d_ff = w_gate.shape[-1]
    row = lambda i: (i, 0)
    resident = lambda shape: pl.BlockSpec(shape, lambda i: (0, 0), pipeline_mode=pl.Buffered(1))
    return pl.pallas_call(
        functools.partial(_outproj_ffn_kernel, f_chunk=f_chunk),
        grid=(t // tm,),
        in_specs=[pl.BlockSpec((tm, d), row), pl.BlockSpec((tm, SB_WIDTH), row), pl.BlockSpec((tm, MLA_WIDTH), row),
                  resident((SB_WIDTH, d)), resident((MLA_WIDTH, d)), resident((1, d)),
                  resident((d, d_ff)), resident((d, d_ff)), resident((d_ff, d))],
        out_specs=pl.BlockSpec((tm, d), row),
        out_shape=jax.ShapeDtypeStruct((t, d), F32),
        compiler_params=_params("arbitrary"),
        name="out_projection_dense_swiglu",
    )(x2, o_sb, o_mla, w_out[:SB_WIDTH].astype(BF16), w_out[SB_WIDTH:].astype(BF16), norm_g.reshape(1, d),
      w_gate.astype(BF16), w_up.astype(BF16), w_down.astype(BF16))


DENSE_FFN_TILE = 512
MLA_Q_TILE = 1024


def _layer0(x2, b, s, att_norm, att_w_in, q_lat_g, w_q_up, kv_lat_g, w_kv_up, q_g, k_g, w_out,
            dffn_norm, w_gate, w_up, w_down, tm, tq, tk, f_chunk):
    qkv, qm, km, vm = _in_projection(x2, s, att_norm, att_w_in, q_lat_g, w_q_up, kv_lat_g, w_kv_up, q_g, k_g, tm)
    o_sb = _sb_attention(qkv.reshape(b, s, -1), tq, tk).reshape(b * s, -1)
    o_mla = _mla_attention(qm.reshape(b, s, -1), km.reshape(b, s, -1), vm, MLA_Q_TILE).reshape(b * s, -1)
    return _outproj_dense_ffn(x2, o_sb, o_mla, w_out, dffn_norm, w_gate, w_up, w_down, DENSE_FFN_TILE, f_chunk)


SSM_SEGS = 8
SSM_SEG_LEN = 64
SSM_TILE = SSM_SEGS * SSM_SEG_LEN
SSM_CH_CHUNK = LANES
SSM_ST_CHUNK = SSM_CH_CHUNK // SSM_GROUP * SSM_STATE
SSM_SCAN_TILES = 8


def _s5_prep_kernel(are_ref, aim_ref, logdt_ref, bre_ref, bim_ref, cre_ref, cim_ref,
                    wbu_ref, wc_ref, pow_ref):
    n = SSM_NSTATE
    a_re = jnp.minimum(are_ref[...], EIG_RE_MAX)
    a_im = aim_ref[...]
    dt = jnp.exp(logdt_ref[...])
    mag = jnp.exp(a_re * dt)
    lam_re = mag * jnp.cos(a_im * dt)
    lam_im = mag * jnp.sin(a_im * dt)
    den = a_re * a_re + a_im * a_im
    num_re = lam_re - 1.0
    coef_re = (num_re * a_re + lam_im * a_im) / den
    coef_im = (lam_im * a_re - num_re * a_im) / den
    b_re = bre_ref[...]
    b_im = bim_ref[...]
    in_group = (lax.broadcasted_iota(jnp.int32, (SSM_CH_CHUNK, n), 0) // SSM_GROUP
                == lax.broadcasted_iota(jnp.int32, (SSM_CH_CHUNK, n), 1) % SSM_ST_CHUNK // SSM_STATE)
    wbu_ref[:, :n] = jnp.where(in_group, coef_re * b_re - coef_im * b_im, 0.0).astype(BF16)
    wbu_ref[:, n:] = jnp.where(in_group, coef_re * b_im + coef_im * b_re, 0.0).astype(BF16)
    in_group_t = (lax.broadcasted_iota(jnp.int32, (n, SSM_CH_CHUNK), 0) % SSM_ST_CHUNK // SSM_STATE
                  == lax.broadcasted_iota(jnp.int32, (n, SSM_CH_CHUNK), 1) // SSM_GROUP)
    wc_ref[:n, :] = jnp.where(in_group_t, cre_ref[...], 0.0).astype(BF16)
    wc_ref[n:, :] = jnp.where(in_group_t, -cim_ref[...], 0.0).astype(BF16)
    steps = jnp.where(lax.broadcasted_iota(jnp.int32, (SSM_SEGS, n), 0) == 0, 1.0, float(SSM_SEG_LEN))
    mag_k = jnp.exp(steps * (a_re * dt))
    ang_k = steps * (a_im * dt)
    pow_ref[:, :n] = mag_k * jnp.cos(ang_k)
    pow_ref[:, n:] = mag_k * jnp.sin(ang_k)


def _s5_prepare(a_re, a_im, log_dt, b_re, b_im, c_re, c_im):
    n = SSM_NSTATE
    row = lambda a: a.reshape(1, n)
    reps = SSM_CH_CHUNK // SSM_GROUP
    b_t = lambda w: jnp.tile(w.transpose(2, 0, 1).reshape(SSM_GROUP, n), (reps, 1))
    c_t = lambda w: jnp.tile(w.transpose(0, 2, 1).reshape(n, SSM_GROUP), (1, reps))
    return pl.pallas_call(
        _s5_prep_kernel,
        out_shape=[jax.ShapeDtypeStruct((SSM_CH_CHUNK, 2 * n), BF16),
                   jax.ShapeDtypeStruct((2 * n, SSM_CH_CHUNK), BF16),
                   jax.ShapeDtypeStruct((SSM_SEGS, 2 * n), F32)],
        compiler_params=pltpu.CompilerParams(vmem_limit_bytes=VMEM_LIMIT),
        name="s5_discretize",
    )(row(a_re), row(a_im), row(jnp.repeat(log_dt, SSM_STATE)), b_t(b_re), b_t(b_im), c_t(c_re), c_t(c_im))


def _s5_kernel(h_ref, g_ref, win_ref, wbu_ref, pow_ref, wc_ref, d_ref, wglu_ref, o_ref,
               ut_ref, up_ref, st_ref, carry_ref, yp_ref, yt_ref):
    n = SSM_NSTATE
    nt = n // LANES
    n_ch = SSM_WIDTH // LANES
    per = SSM_ST_CHUNK // LANES
    d_model = h_ref.shape[-1]

    @pl.when(pl.program_id(1) == 0)
    def _():
        carry_ref[...] = jnp.zeros_like(carry_ref)

    x = h_ref[...]
    xn = _rms(x, d_model) * g_ref[...]
    u = _dot(xn.astype(BF16), win_ref[...])
    for k in range(n_ch):
        ut_ref[k] = u[:, k * LANES:(k + 1) * LANES]

    def to_step_major(i, _):
        dst = pl.ds(pl.multiple_of(i * SSM_SEGS, SSM_SEGS), SSM_SEGS)
        for k in range(n_ch):
            up_ref[k, dst, :] = ut_ref[k, pl.ds(i, SSM_SEGS, stride=SSM_SEG_LEN), :]
        return 0

    lax.fori_loop(0, SSM_SEG_LEN, to_step_major, 0)

    for c in range(n_ch):
        u16 = up_ref[c].astype(BF16)
        for part in range(2):
            cols = slice(part * n + c * SSM_ST_CHUNK, part * n + (c + 1) * SSM_ST_CHUNK)
            bu = _dot(u16, wbu_ref[:, cols])
            for k in range(per):
                st_ref[part * nt + c * per + k] = bu[:, k * LANES:(k + 1) * LANES]

    for grp in range(nt // SSM_SCAN_TILES):
        tiles = range(grp * SSM_SCAN_TILES, (grp + 1) * SSM_SCAN_TILES)
        re_l = [slice(k * LANES, (k + 1) * LANES) for k in tiles]
        im_l = [slice(n + k * LANES, n + (k + 1) * LANES) for k in tiles]
        lam_re = [jnp.broadcast_to(pow_ref[0:1, c], (SSM_SEGS, LANES)) for c in re_l]
        lam_im = [jnp.broadcast_to(pow_ref[0:1, c], (SSM_SEGS, LANES)) for c in im_l]

        def scan(state, store):
            def step(i, state):
                rows = pl.ds(pl.multiple_of(i * SSM_SEGS, SSM_SEGS), SSM_SEGS)
                new_state = []
                for idx, k in enumerate(tiles):
                    s_re, s_im = state[idx]
                    n_re = lam_re[idx] * s_re - lam_im[idx] * s_im + st_ref[k, rows, :]
                    n_im = lam_re[idx] * s_im + lam_im[idx] * s_re + st_ref[nt + k, rows, :]
                    if store:
                        st_ref[k, rows, :] = n_re
                        st_ref[nt + k, rows, :] = n_im
                    new_state.append((n_re, n_im))
                return tuple(new_state)

            return lax.fori_loop(0, SSM_SEG_LEN, step, state)

        zero = jnp.zeros((SSM_SEGS, LANES), F32)
        ends = scan(tuple((zero, zero) for _ in tiles), False)
        starts = []
        for idx in range(len(tiles)):
            e_re, e_im = ends[idx]
            p_re = pow_ref[1:2, re_l[idx]]
            p_im = pow_ref[1:2, im_l[idx]]
            c_re = carry_ref[:, re_l[idx]]
            c_im = carry_ref[:, im_l[idx]]
            rows_re, rows_im = [], []
            for j in range(SSM_SEGS):
                rows_re.append(c_re)
                rows_im.append(c_im)
                c_re, c_im = (e_re[j:j + 1] + (p_re * c_re - p_im * c_im),
                              e_im[j:j + 1] + (p_re * c_im + p_im * c_re))
            carry_ref[:, re_l[idx]] = c_re
            carry_ref[:, im_l[idx]] = c_im
            starts.append((jnp.concatenate(rows_re, axis=0), jnp.concatenate(rows_im, axis=0)))
        scan(tuple(starts), True)

    for c in range(n_ch):
        h_re = jnp.concatenate([st_ref[c * per + k] for k in range(per)], axis=-1).astype(BF16)
        h_im = jnp.concatenate([st_ref[nt + c * per + k] for k in range(per)], axis=-1).astype(BF16)
        rows_re = slice(c * SSM_ST_CHUNK, (c + 1) * SSM_ST_CHUNK)
        rows_im = slice(n + c * SSM_ST_CHUNK, n + (c + 1) * SSM_ST_CHUNK)
        yp_ref[c] = _dot(h_re, wc_ref[rows_re, :]) + _dot(h_im, wc_ref[rows_im, :])

    def to_token_major(i, _):
        src = pl.ds(pl.multiple_of(i * SSM_SEGS, SSM_SEGS), SSM_SEGS)
        for k in range(n_ch):
            yt_ref[k, pl.ds(i, SSM_SEGS, stride=SSM_SEG_LEN), :] = yp_ref[k, src, :]
        return 0

    lax.fori_loop(0, SSM_SEG_LEN, to_token_major, 0)
    y = jnp.concatenate([yt_ref[k] for k in range(n_ch)], axis=-1)
    y = jax.nn.gelu(y + d_ref[...] * u)
    z = _dot(y.astype(BF16), wglu_ref[...])
    o_ref[...] = x + z[:, :d_model] * jax.nn.sigmoid(z[:, d_model:])


def _s5_layer(h, b, s, norm_g, w_in, a_re, a_im, log_dt, b_re, b_im, c_re, c_im, d_skip, w_glu):
    t, d = h.shape
    n = SSM_NSTATE
    w_bu, w_c, powers = _s5_prepare(a_re, a_im, log_dt, b_re, b_im, c_re, c_im)
    tiles = s // SSM_TILE
    resident = lambda shape: pl.BlockSpec(shape, lambda bi, ti: (0, 0), pipeline_mode=pl.Buffered(1))
    row = lambda bi, ti: (bi * tiles + ti, 0)
    ch_tiles = pltpu.VMEM((SSM_WIDTH // LANES, SSM_TILE, LANES), F32)
    return pl.pallas_call(
        _s5_kernel,
        grid=(b, tiles),
        in_specs=[pl.BlockSpec((SSM_TILE, d), row), resident((1, d)), resident((d, SSM_WIDTH)),
                  resident((SSM_CH_CHUNK, 2 * n)), resident((SSM_SEGS, 2 * n)), resident((2 * n, SSM_CH_CHUNK)),
                  resident((1, SSM_WIDTH)), resident((SSM_WIDTH, 2 * d))],
        out_specs=pl.BlockSpec((SSM_TILE, d), row),
        out_shape=jax.ShapeDtypeStruct((t, d), F32),
        scratch_shapes=[ch_tiles, ch_tiles, pltpu.VMEM((2 * n // LANES, SSM_TILE, LANES), F32),
                        pltpu.VMEM((1, 2 * n), F32), ch_tiles, ch_tiles],
        compiler_params=_params("arbitrary", "arbitrary"),
        name="s5_mixer",
    )(h, norm_g.reshape(1, d), w_in.astype(BF16), w_bu, powers, w_c, d_skip.reshape(1, -1), w_glu.astype(BF16))


MOE_BLOCK = 256
META_E1, META_E2, META_G1, META_G2, META_R1, META_R2 = range(6)


def _router_kernel(h_ref, g_ref, wr_ref, tri_ref, xn_ref, meta_ref, counts_ref, run_ref):
    @pl.when(pl.program_id(0) == 0)
    def _():
        run_ref[...] = jnp.zeros_like(run_ref)

    x = h_ref[...]
    xn = _rms(x, x.shape[-1]) * g_ref[...]
    xn_ref[...] = xn
    logits = jnp.dot(xn, wr_ref[...], preferred_element_type=F32, precision=lax.Precision.HIGHEST)
    tm = logits.shape[0]
    e = lax.broadcasted_iota(jnp.int32, (tm, N_EXPERTS), 1)
    v1 = jnp.max(logits, axis=-1, keepdims=True)
    i1 = jnp.min(jnp.where(logits == v1, e, N_EXPERTS), axis=-1, keepdims=True)
    rest = jnp.where(e == i1, -jnp.inf, logits)
    v2 = jnp.max(rest, axis=-1, keepdims=True)
    i2 = jnp.min(jnp.where(rest == v2, e, N_EXPERTS), axis=-1, keepdims=True)
    ex = jnp.exp(v2 - v1)
    g1 = 1.0 / (1.0 + ex)
    g2 = ex / (1.0 + ex)
    oh1 = (e == i1).astype(F32)
    oh2 = (e == i2).astype(F32)
    both = oh1 + oh2
    before = _dot(tri_ref[...], both.astype(BF16)) + run_ref[...]
    r1 = jnp.sum(oh1 * before, axis=-1, keepdims=True)
    r2 = jnp.sum(oh2 * before, axis=-1, keepdims=True)
    run_ref[...] += jnp.sum(both, axis=0, keepdims=True)
    counts_ref[...] = run_ref[...]
    meta = jnp.zeros((tm, N_EXPERTS), F32)
    for lane, val in ((META_E1, i1.astype(F32)), (META_E2, i2.astype(F32)), (META_G1, g1), (META_G2, g2),
                      (META_R1, r1), (META_R2, r2)):
        meta = jnp.where(e == lane, val, meta)
    meta_ref[...] = meta


def _router(h, norm_g, w_router, tm):
    t, d = h.shape
    row = lambda i: (i, 0)
    tri = (jnp.arange(tm)[:, None] > jnp.arange(tm)[None, :]).astype(BF16)
    return pl.pallas_call(
        _router_kernel,
        grid=(t // tm,),
        in_specs=[pl.BlockSpec((tm, d), row), _const_spec((1, d)), _const_spec((d, N_EXPERTS)),
                  _const_spec((tm, tm))],
        out_specs=[pl.BlockSpec((tm, d), row), pl.BlockSpec((tm, N_EXPERTS), row), _const_spec((1, N_EXPERTS))],
        out_shape=[jax.ShapeDtypeStruct((t, d), F32), jax.ShapeDtypeStruct((t, N_EXPERTS), F32),
                   jax.ShapeDtypeStruct((1, N_EXPERTS), F32)],
        scratch_shapes=[pltpu.VMEM((1, N_EXPERTS), F32)],
        compiler_params=_params("arbitrary"),
        name="moe_router",
    )(h, norm_g.reshape(1, d), w_router, tri)


def _slot_map_kernel(d1_ref, d2_ref, tok_ref):
    def clear(slot, _):
        tok_ref[slot] = 0
        return 0

    lax.fori_loop(0, tok_ref.shape[0], clear, 0, unroll=16)

    def place(tok, _):
        tok_ref[d1_ref[tok]] = tok
        tok_ref[d2_ref[tok]] = tok
        return 0

    lax.fori_loop(0, d1_ref.shape[0], place, 0, unroll=8)


def _slot_map(dest1, dest2, n_slots):
    smem = pl.BlockSpec(memory_space=pltpu.SMEM)
    return pl.pallas_call(
        _slot_map_kernel,
        in_specs=[smem, smem], out_specs=smem,
        out_shape=jax.ShapeDtypeStruct((n_slots,), jnp.int32),
        name="moe_slot_map",
    )(dest1, dest2)


def _row_copy(src_ref, src_row, dst_ref, dst_row, sem):
    return pltpu.make_async_copy(src_ref.at[pl.ds(src_row, 1)], dst_ref.at[pl.ds(dst_row, 1)], sem)


def _expert_ffn_kernel(be_ref, bv_ref, tok_ref, xn_ref, wg_ref, wu_ref, wd_ref, ys_ref, buf_ref, sem, *, f_chunk):
    del be_ref
    i = pl.program_id(0)
    last = pl.num_programs(0) - 1
    cur = i % 2

    def gather(block, buf):
        base = block * MOE_BLOCK
        for r in range(MOE_BLOCK):
            _row_copy(xn_ref, tok_ref[base + r], buf_ref.at[buf], r, sem.at[buf]).start()

    def wait_gather(buf):
        pltpu.make_async_copy(xn_ref.at[pl.ds(0, MOE_BLOCK)], buf_ref.at[buf], sem.at[buf]).wait()

    @pl.when(i == 0)
    def _():
        gather(0, 0)

    gather(jnp.minimum(i + 1, last), 1 - cur)
    wait_gather(cur)
    valid = bv_ref[i] != 0

    @pl.when(valid)
    def _():
        ys_ref[...] = _swiglu(buf_ref[cur].astype(BF16), wg_ref, wu_ref, wd_ref, f_chunk)

    @pl.when(jnp.logical_not(valid))
    def _():
        ys_ref[...] = jnp.zeros_like(ys_ref)

    @pl.when(i == last)
    def _():
        wait_gather(1 - cur)


def _expert_ffn(xn, slot_tok, blk_expert, blk_valid, w_gate, w_up, w_down, f_chunk):
    n_slots = slot_tok.shape[0]
    d = xn.shape[-1]
    d_ff = w_gate.shape[-1]
    return pl.pallas_call(
        functools.partial(_expert_ffn_kernel, f_chunk=f_chunk),
        grid_spec=pltpu.PrefetchScalarGridSpec(
            num_scalar_prefetch=3, grid=(n_slots // MOE_BLOCK,),
            in_specs=[pl.BlockSpec(memory_space=pl.ANY),
                      pl.BlockSpec((None, d, d_ff), lambda i, be, bv, st: (be[i], 0, 0)),
                      pl.BlockSpec((None, d, d_ff), lambda i, be, bv, st: (be[i], 0, 0)),
                      pl.BlockSpec((None, d_ff, d), lambda i, be, bv, st: (be[i], 0, 0))],
            out_specs=pl.BlockSpec((MOE_BLOCK, d), lambda i, be, bv, st: (i, 0)),
            scratch_shapes=[pltpu.VMEM((2, MOE_BLOCK, d), F32), pltpu.SemaphoreType.DMA((2,))]),
        out_shape=jax.ShapeDtypeStruct((n_slots, d), F32),
        compiler_params=_params("arbitrary"),
        name="expert_swiglu",
    )(blk_expert, blk_valid, slot_tok, xn, w_gate.astype(BF16), w_up.astype(BF16), w_down.astype(BF16))


def _combine_kernel(d1_ref, d2_ref, h_ref, meta_ref, ys_ref, o_ref, a_ref, b_ref, sem, *, tm):
    base = pl.program_id(0) * tm

    def issue(r, _):
        tok = base + r
        _row_copy(ys_ref, d1_ref[tok], a_ref, r, sem).start()
        _row_copy(ys_ref, d2_ref[tok], b_ref, r, sem).start()
        return 0

    lax.fori_loop(0, tm, issue, 0, unroll=8)

    pltpu.make_async_copy(ys_ref.at[pl.ds(0, tm)], a_ref, sem).wait()
    pltpu.make_async_copy(ys_ref.at[pl.ds(0, tm)], b_ref, sem).wait()
    meta = meta_ref[...]
    g1 = meta[:, META_G1:META_G1 + 1]
    g2 = meta[:, META_G2:META_G2 + 1]
    o_ref[...] = h_ref[...] + (g1 * a_ref[...] + g2 * b_ref[...])


def _combine(h, meta, ys, dest1, dest2, tm):
    t, d = h.shape
    row = lambda i, d1, d2: (i, 0)
    return pl.pallas_call(
        functools.partial(_combine_kernel, tm=tm),
        grid_spec=pltpu.PrefetchScalarGridSpec(
            num_scalar_prefetch=2, grid=(t // tm,),
            in_specs=[pl.BlockSpec((tm, d), row), pl.BlockSpec((tm, N_EXPERTS), row),
                      pl.BlockSpec(memory_space=pl.ANY)],
            out_specs=pl.BlockSpec((tm, d), row),
            scratch_shapes=[pltpu.VMEM((tm, d), F32), pltpu.VMEM((tm, d), F32), pltpu.SemaphoreType.DMA(())]),
        out_shape=jax.ShapeDtypeStruct((t, d), F32),
        compiler_params=_params("arbitrary"),
        name="moe_combine",
    )(dest1, dest2, h, meta, ys)


def _moe_layer(h, norm_g, w_router, w_gate, w_up, w_down, tm, f_chunk):
    t, d = h.shape
    xn, meta, counts = _router(h, norm_g, w_router, tm)
    counts = counts.reshape(N_EXPERTS).astype(jnp.int32)
    padded = (counts + MOE_BLOCK - 1) // MOE_BLOCK * MOE_BLOCK
    pad_end = jnp.cumsum(padded)
    pad_start = pad_end - padded
    e1 = meta[:, META_E1].astype(jnp.int32)
    e2 = meta[:, META_E2].astype(jnp.int32)
    dest1 = pad_start[e1] + meta[:, META_R1].astype(jnp.int32)
    dest2 = pad_start[e2] + meta[:, META_R2].astype(jnp.int32)
    n_slots = 2 * t + N_EXPERTS * MOE_BLOCK
    blk_start = jnp.arange(n_slots // MOE_BLOCK, dtype=jnp.int32) * MOE_BLOCK
    blk_expert = jnp.minimum(jnp.sum(blk_start[:, None] >= pad_end[None, :], axis=1), N_EXPERTS - 1).astype(jnp.int32)
    blk_valid = (blk_start < pad_end[-1]).astype(jnp.int32)
    slot_tok = _slot_map(dest1, dest2, n_slots)
    ys = _expert_ffn(xn, slot_tok, blk_expert, blk_valid, w_gate, w_up, w_down, f_chunk)
    return _combine(h, meta, ys, dest1, dest2, tm)


def kernel(x, att_norm, att_w_in, att_q_latent_norm, att_w_q_up, att_kv_latent_norm, att_w_kv_up, att_q_norm, att_k_norm, att_w_out, dffn_norm, dffn_w_gate, dffn_w_up, dffn_w_down, ssm_norm, ssm_w_in, ssm_a_re, ssm_a_im, ssm_log_dt, ssm_b_re, ssm_b_im, ssm_c_re, ssm_c_im, ssm_d, ssm_w_glu, moe_norm, moe_router, moe_w_gate, moe_w_up, moe_w_down):
    b, s, d = x.shape
    x2 = x.reshape(b * s, d)
    h = _layer0(x2, b, s, att_norm[0], att_w_in[0], att_q_latent_norm[0], att_w_q_up[0], att_kv_latent_norm[0],
                att_w_kv_up[0], att_q_norm[0], att_k_norm[0], att_w_out[0], dffn_norm[0], dffn_w_gate[0],
                dffn_w_up[0], dffn_w_down[0], tm=512, tq=512, tk=256, f_chunk=512)
    h = _s5_layer(h, b, s, ssm_norm[0], ssm_w_in[0], ssm_a_re[0], ssm_a_im[0], ssm_log_dt[0], ssm_b_re[0],
                  ssm_b_im[0], ssm_c_re[0], ssm_c_im[0], ssm_d[0], ssm_w_glu[0])
    h = _moe_layer(h, moe_norm[0], moe_router[0], moe_w_gate[0], moe_w_up[0], moe_w_down[0], tm=256, f_chunk=512)
    return h.reshape(b, s, d)
```

```python
import functools
import math

import jax
import jax.numpy as jnp
from jax import lax
from jax.experimental import pallas as pl
from jax.experimental.pallas import tpu as pltpu

F32 = jnp.float32
BF16 = jnp.bfloat16

EPS = 1e-6
LANES = 128
SB_HEADS = 8
SB_HEAD_DIM = 64
SB_WIDTH = SB_HEADS * SB_HEAD_DIM
MLA_HEADS = 8
MLA_Q_RANK = 256
MLA_KV_RANK = 128
MLA_NOPE_DIM = 64
MLA_ROPE_DIM = 32
MLA_QK_DIM = MLA_NOPE_DIM + MLA_ROPE_DIM
MLA_V_DIM = 64
MLA_WIDTH = MLA_HEADS * MLA_V_DIM
ROPE_THETA = 10000.0
SSM_GROUP = 16
SSM_GROUPS = 32
SSM_STATE = 64
SSM_WIDTH = SSM_GROUP * SSM_GROUPS
SSM_NSTATE = SSM_GROUPS * SSM_STATE
EIG_RE_MAX = -1e-4
N_EXPERTS = 8
VMEM_LIMIT = 56 * 1024 * 1024


def _params(*sem):
    return pltpu.CompilerParams(dimension_semantics=sem, vmem_limit_bytes=VMEM_LIMIT)


def _rms(x, n):
    return x * lax.rsqrt(jnp.sum(x * x, axis=-1, keepdims=True) * (1.0 / n) + EPS)


def _dot(a, b):
    return jnp.dot(a, b, preferred_element_type=F32)


def _dot_nt(a, b):
    return lax.dot_general(a, b, (((1,), (1,)), ((), ())), preferred_element_type=F32)


def _const_spec(shape):
    return pl.BlockSpec(shape, lambda *_: (0,) * len(shape))


IN_SB = 3 * SB_WIDTH
IN_CQ = IN_SB
IN_CKV = IN_CQ + MLA_Q_RANK
IN_KR = IN_CKV + MLA_KV_RANK
IN_KRS = IN_KR + LANES
IN_COLS_PAD = IN_KRS + LANES
MLA_PAD = MLA_HEADS * LANES
MLA_V_ROWS = MLA_V_DIM + 16
MLA_VT_WIDTH = MLA_HEADS * MLA_V_ROWS


def _inproj_kernel(x_ref, g_ref, win_ref, qlg_ref, wq_ref, kvlg_ref, wkv_ref, ctab_ref, stab_ref,
                   qg_ref, qgs_ref, kg_ref, kgs_ref, qkv_ref, qm_ref, km_ref, vm_ref):
    x = x_ref[...]
    xn = _rms(x, x.shape[-1]) * g_ref[...]
    hh = _dot(xn.astype(BF16), win_ref[...])
    qkv_ref[:, :SB_WIDTH] = (hh[:, :SB_WIDTH] * (SB_HEAD_DIM ** -0.5)).astype(BF16)
    qkv_ref[:, SB_WIDTH:] = hh[:, SB_WIDTH:IN_SB].astype(BF16)
    cq = _rms(hh[:, IN_CQ:IN_CKV], MLA_Q_RANK) * qlg_ref[...]
    qf = _dot(cq.astype(BF16), wq_ref[...])
    ckv = _rms(hh[:, IN_CKV:IN_KR], MLA_KV_RANK) * kvlg_ref[...]
    kvf = _dot(ckv.astype(BF16), wkv_ref[...])
    v_col = lax.broadcasted_iota(jnp.int32, (1, MLA_VT_WIDTH), 1)
    v_ones = jnp.where(v_col % MLA_V_ROWS >= MLA_V_DIM, 1.0, 0.0)
    vm_ref[...] = (kvf[:, MLA_PAD:] + v_ones).T.astype(BF16)
    kr = hh[:, IN_KR:IN_KRS]
    krs = hh[:, IN_KRS:IN_COLS_PAD]
    ctab = ctab_ref[...]
    stab = stab_ref[...]
    scale = MLA_QK_DIM ** -0.5 * math.log2(math.e)
    cq_t = ctab * (qg_ref[...] * scale)
    sq_t = stab * (qgs_ref[...] * scale)
    ck_t = ctab * kg_ref[...]
    sk_t = stab * kgs_ref[...]
    for h in range(MLA_HEADS):
        lo, hi = h * LANES, (h + 1) * LANES
        qh = qf[:, lo:hi]
        qs = qf[:, MLA_PAD + lo:MLA_PAD + hi]
        q_inv = lax.rsqrt(jnp.sum(qh * qh, axis=-1, keepdims=True) * (1.0 / MLA_QK_DIM) + EPS)
        qm_ref[:, lo:hi] = ((qh * cq_t + qs * sq_t) * q_inv).astype(BF16)
        kh = kvf[:, lo:hi] + kr
        k_inv = lax.rsqrt(jnp.sum(kh * kh, axis=-1, keepdims=True) * (1.0 / MLA_QK_DIM) + EPS)
        km_ref[:, lo:hi] = ((kh * ck_t + krs * sk_t) * k_inv).astype(BF16)


def _pad_heads(w, n_heads, width):
    k = w.shape[0]
    w = w.reshape(k, n_heads, width)
    return jnp.pad(w, ((0, 0), (0, 0), (0, LANES - width))).reshape(k, n_heads * LANES)


def _swap_rope(w):
    half = MLA_ROPE_DIM // 2
    return jnp.concatenate([jnp.zeros_like(w[..., :MLA_NOPE_DIM]),
                            w[..., MLA_NOPE_DIM + half:], w[..., MLA_NOPE_DIM:MLA_NOPE_DIM + half]], axis=-1)


def _rope_lane_tables(s_len):
    half = MLA_ROPE_DIM // 2
    inv_freq = ROPE_THETA ** (-jnp.arange(0, MLA_ROPE_DIM, 2, dtype=F32) / MLA_ROPE_DIM)
    ang = jnp.arange(s_len, dtype=F32)[:, None] * inv_freq[None, :]
    cos, sin = jnp.cos(ang), jnp.sin(ang)
    ones = jnp.ones((s_len, MLA_NOPE_DIM), F32)
    zeros_n = jnp.zeros((s_len, MLA_NOPE_DIM), F32)
    zeros_t = jnp.zeros((s_len, LANES - MLA_QK_DIM), F32)
    ctab = jnp.concatenate([ones, cos, cos, zeros_t], axis=-1)
    stab = jnp.concatenate([zeros_n, -sin, sin, zeros_t], axis=-1)
    del half
    return ctab, stab


def _in_projection(x2, s_len, norm_g, w_in, q_lat_g, w_q_up, kv_lat_g, w_kv_up, q_g, k_g, tm):
    t, d = x2.shape
    w_kr = w_in[:, IN_KR - LANES + 0:][:, :0]
    del w_kr
    off_kr = 3 * SB_WIDTH + MLA_Q_RANK + MLA_KV_RANK
    w_rope = w_in[:, off_kr:]
    zeros_n = jnp.zeros((d, MLA_NOPE_DIM), F32)
    zeros_t = jnp.zeros((d, LANES - MLA_QK_DIM), F32)
    half = MLA_ROPE_DIM // 2
    w_in_pad = jnp.concatenate(
        [w_in[:, :off_kr], zeros_n, w_rope, zeros_t,
         zeros_n, w_rope[:, half:], w_rope[:, :half], zeros_t], axis=-1).astype(BF16)
    wq3 = w_q_up.reshape(MLA_Q_RANK, MLA_HEADS, MLA_QK_DIM)
    wq_pad = jnp.concatenate(
        [_pad_heads(w_q_up, MLA_HEADS, MLA_QK_DIM),
         _pad_heads(_swap_rope(wq3).reshape(MLA_Q_RANK, -1), MLA_HEADS, MLA_QK_DIM)], axis=-1).astype(BF16)
    wkv3 = w_kv_up.reshape(MLA_KV_RANK, MLA_HEADS, MLA_NOPE_DIM + MLA_V_DIM)
    wkv_pad = jnp.concatenate(
        [_pad_heads(wkv3[..., :MLA_NOPE_DIM].reshape(MLA_KV_RANK, -1), MLA_HEADS, MLA_NOPE_DIM),
         jnp.pad(wkv3[..., MLA_NOPE_DIM:], ((0, 0), (0, 0), (0, MLA_V_ROWS - MLA_V_DIM))).reshape(MLA_KV_RANK, -1)],
        axis=-1).astype(BF16)
    ctab, stab = _rope_lane_tables(s_len)
    pad_g = lambda g: jnp.pad(g, (0, LANES - MLA_QK_DIM)).reshape(1, LANES)
    swap_g = lambda g: jnp.pad(jnp.concatenate(
        [jnp.zeros((MLA_NOPE_DIM,), F32), g[MLA_NOPE_DIM + half:], g[MLA_NOPE_DIM:MLA_NOPE_DIM + half]]),
        (0, LANES - MLA_QK_DIM)).reshape(1, LANES)
    n_pos = s_len // tm
    row = lambda i: (i, 0)
    return pl.pallas_call(
        _inproj_kernel,
        grid=(t // tm,),
        in_specs=[
            pl.BlockSpec((tm, d), row),
            _const_spec((1, d)),
            _const_spec((d, IN_COLS_PAD)),
            _const_spec((1, MLA_Q_RANK)),
            _const_spec((MLA_Q_RANK, 2 * MLA_PAD)),
            _const_spec((1, MLA_KV_RANK)),
            _const_spec((MLA_KV_RANK, MLA_PAD + MLA_VT_WIDTH)),
            pl.BlockSpec((tm, LANES), lambda i: (i % n_pos, 0)),
            pl.BlockSpec((tm, LANES), lambda i: (i % n_pos, 0)),
            _const_spec((1, LANES)), _const_spec((1, LANES)),
            _const_spec((1, LANES)), _const_spec((1, LANES)),
        ],
        out_specs=[
            pl.BlockSpec((tm, IN_SB), row),
            pl.BlockSpec((tm, MLA_PAD), row),
            pl.BlockSpec((tm, MLA_PAD), row),
            pl.BlockSpec((MLA_VT_WIDTH, tm), lambda i: (0, i)),
        ],
        out_shape=[
            jax.ShapeDtypeStruct((t, IN_SB), BF16),
            jax.ShapeDtypeStruct((t, MLA_PAD), BF16),
            jax.ShapeDtypeStruct((t, MLA_PAD), BF16),
            jax.ShapeDtypeStruct((MLA_VT_WIDTH, t), BF16),
        ],
        compiler_params=_params("arbitrary"),
        name="in_projection",
    )(x2, norm_g.reshape(1, d), w_in_pad, q_lat_g.reshape(1, -1), wq_pad, kv_lat_g.reshape(1, -1), wkv_pad,
      ctab, stab, pad_g(q_g), swap_g(q_g), pad_g(k_g), swap_g(k_g))


SB_EXP_UNDERFLOW = 110.0


def _sb_kernel(q_ref, k_ref, v_ref, u_ref, o_ref, acc_ref, carry_ref, z_ref, *, tq, tk):
    qi = pl.program_id(2)
    lane = lax.broadcasted_iota(jnp.int32, (1, LANES), 1)
    q = q_ref[...]
    q_heads = [jnp.where((lane >= j * SB_HEAD_DIM) & (lane < (j + 1) * SB_HEAD_DIM), q, jnp.zeros_like(q))
               for j in range(2)]
    acc_ref[...] = jnp.zeros_like(acc_ref)
    carry_ref[...] = jnp.zeros_like(carry_ref)

    def logits(j, c, lo_row=0):
        start = pl.multiple_of(c * tk, tk)
        z_ref[j, lo_row:, :] = _dot_nt(q_heads[j][lo_row:], k_ref[pl.ds(start, tk), :])

    def absorb(j, c, diagonal, lo_row=0):
        start = pl.multiple_of(c * tk, tk)
        z = z_ref[j, lo_row:, :]
        log_fail = -(jnp.maximum(z, 0.0) + jnp.log(1.0 + jnp.exp(-jnp.abs(z))))
        if diagonal:
            key = start + lax.broadcasted_iota(jnp.int32, z.shape, 1)
            qry = qi * tq + lo_row + lax.broadcasted_iota(jnp.int32, z.shape, 0)
            earlier = key < qry
            log_fail = jnp.where(earlier, log_fail, 0.0)
        hi = log_fail.astype(BF16)
        lo = (log_fail - hi.astype(F32)).astype(BF16)
        log_stick = _dot(hi, u_ref[...]) + _dot(lo, u_ref[...])
        carry = carry_ref[j, lo_row:, :]
        w = jnp.exp(z + log_fail + log_stick + jnp.tile(carry, (1, tk // LANES)))
        if diagonal:
            w = jnp.where(earlier, w, 0.0)
        acc_ref[j, lo_row:, :] += _dot(w.astype(BF16), v_ref[pl.ds(start, tk), :])
        carry_ref[j, lo_row:, :] = carry + (log_stick[:, :1] + log_fail[:, :1])

    def sweep(c, diagonal, lo_row=0, next_lo_row=0):
        logits(1, c, lo_row)
        absorb(0, c, diagonal, lo_row)
        logits(0, jnp.maximum(c - 1, 0), next_lo_row)
        absorb(1, c, diagonal, lo_row)

    n_sub = tq // tk
    first = qi * n_sub + n_sub - 1
    logits(0, first, (n_sub - 1) * tk)
    for sub in reversed(range(n_sub)):
        sweep(qi * n_sub + sub, True, sub * tk, max(sub - 1, 0) * tk)

    def sticks_alive():
        return jnp.max(carry_ref[...]) > -SB_EXP_UNDERFLOW

    def earlier_chunk(state):
        c, _ = state
        sweep(c, False)
        return c - 1, sticks_alive()

    lax.while_loop(lambda state: (state[0] >= 0) & state[1], earlier_chunk,
                   (qi * (tq // tk) - 1, sticks_alive()))
    o_ref[...] = jnp.where(lane < SB_HEAD_DIM, acc_ref[0], acc_ref[1]).astype(o_ref.dtype)


def _sb_attention(qkv, tq, tk):
    b, s, _ = qkv.shape
    n_pairs = SB_WIDTH // LANES
    tri = (jnp.arange(tk)[:, None] > jnp.arange(tk)[None, :]).astype(BF16)
    return pl.pallas_call(
        functools.partial(_sb_kernel, tq=tq, tk=tk),
        grid=(b, n_pairs, s // tq),
        in_specs=[
            pl.BlockSpec((None, tq, LANES), lambda bi, hp, qi: (bi, qi, hp)),
            pl.BlockSpec((None, s, LANES), lambda bi, hp, qi: (bi, 0, n_pairs + hp)),
            pl.BlockSpec((None, s, LANES), lambda bi, hp, qi: (bi, 0, 2 * n_pairs + hp)),
            _const_spec((tk, tk)),
        ],
        out_specs=pl.BlockSpec((None, tq, LANES), lambda bi, hp, qi: (bi, qi, hp)),
        out_shape=jax.ShapeDtypeStruct((b, s, SB_WIDTH), BF16),
        scratch_shapes=[pltpu.VMEM((2, tq, LANES), F32), pltpu.VMEM((2, tq, LANES), F32),
                        pltpu.VMEM((2, tq, tk), F32)],
        compiler_params=_params("arbitrary", "arbitrary", "arbitrary"),
        name="stick_breaking_attention",
    )(qkv, qkv, qkv, tri)


def _mla_kernel(q_ref, k_ref, vt_ref, o_ref, acc_ref, m_ref, s_ref, *, tq):
    qi = pl.program_id(2)
    acc_ref[...] = jnp.zeros_like(acc_ref)
    m_ref[...] = jnp.full(m_ref.shape, -jnp.inf, F32)

    def scores(j, c):
        start = pl.multiple_of(c * tq, tq)
        s_ref[j] = _dot_nt(k_ref[pl.ds(start, tq), j * LANES:(j + 1) * LANES],
                           q_ref[:, j * LANES:(j + 1) * LANES])

    def update(j, c, diagonal):
        start = pl.multiple_of(c * tq, tq)
        s = s_ref[j]
        if diagonal:
            key = lax.broadcasted_iota(jnp.int32, (tq, tq), 0)
            qry = lax.broadcasted_iota(jnp.int32, (tq, tq), 1)
            s = jnp.where(key <= qry, s, -jnp.inf)
        m_prev = m_ref[j]
        m_new = jnp.maximum(m_prev, jnp.max(s, axis=0, keepdims=True))
        p = jnp.exp2(s - m_new)
        vt = vt_ref[j * MLA_V_ROWS:(j + 1) * MLA_V_ROWS, pl.ds(start, tq)]
        acc_ref[j] = jnp.exp2(m_prev - m_new) * acc_ref[j] + _dot(vt, p.astype(BF16))
        m_ref[j] = m_new

    scores(0, 0)

    def earlier_block(c, _):
        scores(1, c)
        update(0, c, False)
        scores(0, c + 1)
        update(1, c, False)
        return 0

    lax.fori_loop(0, qi, earlier_block, 0)
    scores(1, qi)
    update(0, qi, True)
    update(1, qi, True)
    out_t = jnp.concatenate([acc_ref[j, :MLA_V_DIM, :] / acc_ref[j, MLA_V_DIM:MLA_V_DIM + 1, :] for j in range(2)],
                            axis=0)
    o_ref[...] = out_t.T.astype(o_ref.dtype)


def _mla_attention(q, k, vt, tq):
    b, s, _ = q.shape
    n_pairs = MLA_WIDTH // LANES
    return pl.pallas_call(
        functools.partial(_mla_kernel, tq=tq),
        grid=(b, n_pairs, s // tq),
        in_specs=[
            pl.BlockSpec((None, tq, 2 * LANES), lambda bi, hp, qi: (bi, qi, hp)),
            pl.BlockSpec((None, s, 2 * LANES), lambda bi, hp, qi: (bi, 0, hp)),
            pl.BlockSpec((2 * MLA_V_ROWS, s), lambda bi, hp, qi: (hp, bi)),
        ],
        out_specs=pl.BlockSpec((None, tq, LANES), lambda bi, hp, qi: (bi, qi, hp)),
        out_shape=jax.ShapeDtypeStruct((b, s, MLA_WIDTH), BF16),
        scratch_shapes=[pltpu.VMEM((2, MLA_V_ROWS, tq), F32), pltpu.VMEM((2, 1, tq), F32),
                        pltpu.VMEM((2, tq, tq), F32)],
        compiler_params=_params("arbitrary", "arbitrary", "arbitrary"),
        name="latent_attention",
    )(q, k, vt)


def _swiglu(x, wg_ref, wu_ref, wd_ref, f_chunk):
    d_ff = wg_ref.shape[-1]
    acc = None
    for c in range(d_ff // f_chunk):
        cols = slice(c * f_chunk, (c + 1) * f_chunk)
        gate = _dot(x, wg_ref[:, cols])
        up = _dot(x, wu_ref[:, cols])
        act = (gate * jax.nn.sigmoid(gate) * up).astype(BF16)
        part = _dot(act, wd_ref[cols, :])
        acc = part if acc is None else acc + part
    return acc


def _outproj_ffn_kernel(x_ref, osb_ref, omla_ref, wsb_ref, wmla_ref, g_ref, wg_ref, wu_ref, wd_ref, o_ref, *, f_chunk):
    h = x_ref[...] + _dot(osb_ref[...], wsb_ref[...]) + _dot(omla_ref[...], wmla_ref[...])
    hn = (_rms(h, h.shape[-1]) * g_ref[...]).astype(BF16)
    o_ref[...] = h + _swiglu(hn, wg_ref, wu_ref, wd_ref, f_chunk)


def _outproj_dense_ffn(x2, o_sb, o_mla, w_out, norm_g, w_gate, w_up, w_down, tm, f_chunk):
    t, d = x2.shape
    d_ff = w_gate.shape[-1]
    row = lambda i: (i, 0)
    resident = lambda shape: pl.BlockSpec(shape, lambda i: (0, 0), pipeline_mode=pl.Buffered(1))
    return pl.pallas_call(
        functools.partial(_outproj_ffn_kernel, f_chunk=f_chunk),
        grid=(t // tm,),
        in_specs=[pl.BlockSpec((tm, d), row), pl.BlockSpec((tm, SB_WIDTH), row), pl.BlockSpec((tm, MLA_WIDTH), row),
                  resident((SB_WIDTH, d)), resident((MLA_WIDTH, d)), resident((1, d)),
                  resident((d, d_ff)), resident((d, d_ff)), resident((d_ff, d))],
        out_specs=pl.BlockSpec((tm, d), row),
        out_shape=jax.ShapeDtypeStruct((t, d), F32),
        compiler_params=_params("arbitrary"),
        name="out_projection_dense_swiglu",
    )(x2, o_sb, o_mla, w_out[:SB_WIDTH].astype(BF16), w_out[SB_WIDTH:].astype(BF16), norm_g.reshape(1, d),
      w_gate.astype(BF16), w_up.astype(BF16), w_down.astype(BF16))


DENSE_FFN_TILE = 512
MLA_Q_TILE = 1024


def _layer0(x2, b, s, att_norm, att_w_in, q_lat_g, w_q_up, kv_lat_g, w_kv_up, q_g, k_g, w_out,
            dffn_norm, w_gate, w_up, w_down, tm, tq, tk, f_chunk):
    qkv, qm, km, vm = _in_projection(x2, s, att_norm, att_w_in, q_lat_g, w_q_up, kv_lat_g, w_kv_up, q_g, k_g, tm)
    o_sb = _sb_attention(qkv.reshape(b, s, -1), tq, tk).reshape(b * s, -1)
    o_mla = _mla_attention(qm.reshape(b, s, -1), km.reshape(b, s, -1), vm, MLA_Q_TILE).reshape(b * s, -1)
    return _outproj_dense_ffn(x2, o_sb, o_mla, w_out, dffn_norm, w_gate, w_up, w_down, DENSE_FFN_TILE, f_chunk)


SSM_SEGS = 8
SSM_SEG_LEN = 64
SSM_TILE = SSM_SEGS * SSM_SEG_LEN
SSM_CH_CHUNK = LANES
SSM_ST_CHUNK = SSM_CH_CHUNK // SSM_GROUP * SSM_STATE
SSM_SCAN_TILES = 8


def _s5_prep_kernel(are_ref, aim_ref, logdt_ref, bre_ref, bim_ref, cre_ref, cim_ref,
                    wbu_ref, wc_ref, pow_ref):
    n = SSM_NSTATE
    a_re = jnp.minimum(are_ref[...], EIG_RE_MAX)
    a_im = aim_ref[...]
    dt = jnp.exp(logdt_ref[...])
    mag = jnp.exp(a_re * dt)
    lam_re = mag * jnp.cos(a_im * dt)
    lam_im = mag * jnp.sin(a_im * dt)
    den = a_re * a_re + a_im * a_im
    num_re = lam_re - 1.0
    coef_re = (num_re * a_re + lam_im * a_im) / den
    coef_im = (lam_im * a_re - num_re * a_im) / den
    b_re = bre_ref[...]
    b_im = bim_ref[...]
    in_group = (lax.broadcasted_iota(jnp.int32, (SSM_CH_CHUNK, n), 0) // SSM_GROUP
                == lax.broadcasted_iota(jnp.int32, (SSM_CH_CHUNK, n), 1) % SSM_ST_CHUNK // SSM_STATE)
    wbu_ref[:, :n] = jnp.where(in_group, coef_re * b_re - coef_im * b_im, 0.0).astype(BF16)
    wbu_ref[:, n:] = jnp.where(in_group, coef_re * b_im + coef_im * b_re, 0.0).astype(BF16)
    in_group_t = (lax.broadcasted_iota(jnp.int32, (n, SSM_CH_CHUNK), 0) % SSM_ST_CHUNK // SSM_STATE
                  == lax.broadcasted_iota(jnp.int32, (n, SSM_CH_CHUNK), 1) // SSM_GROUP)
    wc_ref[:n, :] = jnp.where(in_group_t, cre_ref[...], 0.0).astype(BF16)
    wc_ref[n:, :] = jnp.where(in_group_t, -cim_ref[...], 0.0).astype(BF16)
    steps = jnp.where(lax.broadcasted_iota(jnp.int32, (SSM_SEGS, n), 0) == 0, 1.0, float(SSM_SEG_LEN))
    mag_k = jnp.exp(steps * (a_re * dt))
    ang_k = steps * (a_im * dt)
    pow_ref[:, :n] = mag_k * jnp.cos(ang_k)
    pow_ref[:, n:] = mag_k * jnp.sin(ang_k)


def _s5_prepare(a_re, a_im, log_dt, b_re, b_im, c_re, c_im):
    n = SSM_NSTATE
    row = lambda a: a.reshape(1, n)
    reps = SSM_CH_CHUNK // SSM_GROUP
    b_t = lambda w: jnp.tile(w.transpose(2, 0, 1).reshape(SSM_GROUP, n), (reps, 1))
    c_t = lambda w: jnp.tile(w.transpose(0, 2, 1).reshape(n, SSM_GROUP), (1, reps))
    return pl.pallas_call(
        _s5_prep_kernel,
        out_shape=[jax.ShapeDtypeStruct((SSM_CH_CHUNK, 2 * n), BF16),
                   jax.ShapeDtypeStruct((2 * n, SSM_CH_CHUNK), BF16),
                   jax.ShapeDtypeStruct((SSM_SEGS, 2 * n), F32)],
        compiler_params=pltpu.CompilerParams(vmem_limit_bytes=VMEM_LIMIT),
        name="s5_discretize",
    )(row(a_re), row(a_im), row(jnp.repeat(log_dt, SSM_STATE)), b_t(b_re), b_t(b_im), c_t(c_re), c_t(c_im))


def _s5_kernel(h_ref, g_ref, win_ref, wbu_ref, pow_ref, wc_ref, d_ref, wglu_ref, o_ref,
               ut_ref, up_ref, st_ref, carry_ref, yp_ref, yt_ref):
    n = SSM_NSTATE
    nt = n // LANES
    n_ch = SSM_WIDTH // LANES
    per = SSM_ST_CHUNK // LANES
    d_model = h_ref.shape[-1]

    @pl.when(pl.program_id(1) == 0)
    def _():
        carry_ref[...] = jnp.zeros_like(carry_ref)

    x = h_ref[...]
    xn = _rms(x, d_model) * g_ref[...]
    u = _dot(xn.astype(BF16), win_ref[...])
    for k in range(n_ch):
        ut_ref[k] = u[:, k * LANES:(k + 1) * LANES]

    def to_step_major(i, _):
        dst = pl.ds(pl.multiple_of(i * SSM_SEGS, SSM_SEGS), SSM_SEGS)
        for k in range(n_ch):
            up_ref[k, dst, :] = ut_ref[k, pl.ds(i, SSM_SEGS, stride=SSM_SEG_LEN), :]
        return 0

    lax.fori_loop(0, SSM_SEG_LEN, to_step_major, 0)

    for c in range(n_ch):
        u16 = up_ref[c].astype(BF16)
        for part in range(2):
            cols = slice(part * n + c * SSM_ST_CHUNK, part * n + (c + 1) * SSM_ST_CHUNK)
            bu = _dot(u16, wbu_ref[:, cols])
            for k in range(per):
                st_ref[part * nt + c * per + k] = bu[:, k * LANES:(k + 1) * LANES]

    for grp in range(nt // SSM_SCAN_TILES):
        tiles = range(grp * SSM_SCAN_TILES, (grp + 1) * SSM_SCAN_TILES)
        re_l = [slice(k * LANES, (k + 1) * LANES) for k in tiles]
        im_l = [slice(n + k * LANES, n + (k + 1) * LANES) for k in tiles]
        lam_re = [jnp.broadcast_to(pow_ref[0:1, c], (SSM_SEGS, LANES)) for c in re_l]
        lam_im = [jnp.broadcast_to(pow_ref[0:1, c], (SSM_SEGS, LANES)) for c in im_l]

        def scan(state, store):
            def step(i, state):
                rows = pl.ds(pl.multiple_of(i * SSM_SEGS, SSM_SEGS), SSM_SEGS)
                new_state = []
                for idx, k in enumerate(tiles):
                    s_re, s_im = state[idx]
                    n_re = lam_re[idx] * s_re - lam_im[idx] * s_im + st_ref[k, rows, :]
                    n_im = lam_re[idx] * s_im + lam_im[idx] * s_re + st_ref[nt + k, rows, :]
                    if store:
                        st_ref[k, rows, :] = n_re
                        st_ref[nt + k, rows, :] = n_im
                    new_state.append((n_re, n_im))
                return tuple(new_state)

            return lax.fori_loop(0, SSM_SEG_LEN, step, state)

        zero = jnp.zeros((SSM_SEGS, LANES), F32)
        ends = scan(tuple((zero, zero) for _ in tiles), False)
        starts = []
        for idx in range(len(tiles)):
            e_re, e_im = ends[idx]
            p_re = pow_ref[1:2, re_l[idx]]
            p_im = pow_ref[1:2, im_l[idx]]
            c_re = carry_ref[:, re_l[idx]]
            c_im = carry_ref[:, im_l[idx]]
            rows_re, rows_im = [], []
            for j in range(SSM_SEGS):
                rows_re.append(c_re)
                rows_im.append(c_im)
                c_re, c_im = (e_re[j:j + 1] + (p_re * c_re - p_im * c_im),
                              e_im[j:j + 1] + (p_re * c_im + p_im * c_re))
            carry_ref[:, re_l[idx]] = c_re
            carry_ref[:, im_l[idx]] = c_im
            starts.append((jnp.concatenate(rows_re, axis=0), jnp.concatenate(rows_im, axis=0)))
        scan(tuple(starts), True)

    for c in range(n_ch):
        h_re = jnp.concatenate([st_ref[c * per + k] for k in range(per)], axis=-1).astype(BF16)
        h_im = jnp.concatenate([st_ref[nt + c * per + k] for k in range(per)], axis=-1).astype(BF16)
        rows_re = slice(c * SSM_ST_CHUNK, (c + 1) * SSM_ST_CHUNK)
        rows_im = slice(n + c * SSM_ST_CHUNK, n + (c + 1) * SSM_ST_CHUNK)
        yp_ref[c] = _dot(h_re, wc_ref[rows_re, :]) + _dot(h_im, wc_ref[rows_im, :])

    def to_token_major(i, _):
        src = pl.ds(pl.multiple_of(i * SSM_SEGS, SSM_SEGS), SSM_SEGS)
        for k in range(n_ch):
            yt_ref[k, pl.ds(i, SSM_SEGS, stride=SSM_SEG_LEN), :] = yp_ref[k, src, :]
        return 0

    lax.fori_loop(0, SSM_SEG_LEN, to_token_major, 0)
    y = jnp.concatenate([yt_ref[k] for k in range(n_ch)], axis=-1)
    y = jax.nn.gelu(y + d_ref[...] * u)
    z = _dot(y.astype(BF16), wglu_ref[...])
    o_ref[...] = x + z[:, :d_model] * jax.nn.sigmoid(z[:, d_model:])


def _s5_layer(h, b, s, norm_g, w_in, a_re, a_im, log_dt, b_re, b_im, c_re, c_im, d_skip, w_glu):
    t, d = h.shape
    n = SSM_NSTATE
    w_bu, w_c, powers = _s5_prepare(a_re, a_im, log_dt, b_re, b_im, c_re, c_im)
    tiles = s // SSM_TILE
    resident = lambda shape: pl.BlockSpec(shape, lambda bi, ti: (0, 0), pipeline_mode=pl.Buffered(1))
    row = lambda bi, ti: (bi * tiles + ti, 0)
    ch_tiles = pltpu.VMEM((SSM_WIDTH // LANES, SSM_TILE, LANES), F32)
    return pl.pallas_call(
        _s5_kernel,
        grid=(b, tiles),
        in_specs=[pl.BlockSpec((SSM_TILE, d), row), resident((1, d)), resident((d, SSM_WIDTH)),
                  resident((SSM_CH_CHUNK, 2 * n)), resident((SSM_SEGS, 2 * n)), resident((2 * n, SSM_CH_CHUNK)),
                  resident((1, SSM_WIDTH)), resident((SSM_WIDTH, 2 * d))],
        out_specs=pl.BlockSpec((SSM_TILE, d), row),
        out_shape=jax.ShapeDtypeStruct((t, d), F32),
        scratch_shapes=[ch_tiles, ch_tiles, pltpu.VMEM((2 * n // LANES, SSM_TILE, LANES), F32),
                        pltpu.VMEM((1, 2 * n), F32), ch_tiles, ch_tiles],
        compiler_params=_params("arbitrary", "arbitrary"),
        name="s5_mixer",
    )(h, norm_g.reshape(1, d), w_in.astype(BF16), w_bu, powers, w_c, d_skip.reshape(1, -1), w_glu.astype(BF16))


MOE_BLOCK = 256
META_E1, META_E2, META_G1, META_G2, META_R1, META_R2 = range(6)


def _router_kernel(h_ref, g_ref, wr_ref, tri_ref, meta_ref, counts_ref, run_ref):
    @pl.when(pl.program_id(0) == 0)
    def _():
        run_ref[...] = jnp.zeros_like(run_ref)

    x = h_ref[...]
    xn = _rms(x, x.shape[-1]) * g_ref[...]
    logits = jnp.dot(xn, wr_ref[...], preferred_element_type=F32, precision=lax.Precision.HIGHEST)
    tm = logits.shape[0]
    e = lax.broadcasted_iota(jnp.int32, (tm, N_EXPERTS), 1)
    v1 = jnp.max(logits, axis=-1, keepdims=True)
    i1 = jnp.min(jnp.where(logits == v1, e, N_EXPERTS), axis=-1, keepdims=True)
    rest = jnp.where(e == i1, -jnp.inf, logits)
    v2 = jnp.max(rest, axis=-1, keepdims=True)
    i2 = jnp.min(jnp.where(rest == v2, e, N_EXPERTS), axis=-1, keepdims=True)
    ex = jnp.exp(v2 - v1)
    g1 = 1.0 / (1.0 + ex)
    g2 = ex / (1.0 + ex)
    oh1 = (e == i1).astype(F32)
    oh2 = (e == i2).astype(F32)
    both = oh1 + oh2
    before = _dot(tri_ref[...], both.astype(BF16)) + run_ref[...]
    r1 = jnp.sum(oh1 * before, axis=-1, keepdims=True)
    r2 = jnp.sum(oh2 * before, axis=-1, keepdims=True)
    run_ref[...] += jnp.sum(both, axis=0, keepdims=True)
    counts_ref[...] = run_ref[...]
    meta = jnp.zeros((tm, N_EXPERTS), F32)
    for lane, val in ((META_E1, i1.astype(F32)), (META_E2, i2.astype(F32)), (META_G1, g1), (META_G2, g2),
                      (META_R1, r1), (META_R2, r2)):
        meta = jnp.where(e == lane, val, meta)
    meta_ref[...] = meta


def _router(h, norm_g, w_router, tm):
    t, d = h.shape
    row = lambda i: (i, 0)
    tri = (jnp.arange(tm)[:, None] > jnp.arange(tm)[None, :]).astype(BF16)
    return pl.pallas_call(
        _router_kernel,
        grid=(t // tm,),
        in_specs=[pl.BlockSpec((tm, d), row), _const_spec((1, d)), _const_spec((d, N_EXPERTS)),
                  _const_spec((tm, tm))],
        out_specs=[pl.BlockSpec((tm, N_EXPERTS), row), _const_spec((1, N_EXPERTS))],
        out_shape=[jax.ShapeDtypeStruct((t, N_EXPERTS), F32), jax.ShapeDtypeStruct((1, N_EXPERTS), F32)],
        scratch_shapes=[pltpu.VMEM((1, N_EXPERTS), F32)],
        compiler_params=_params("arbitrary"),
        name="moe_router",
    )(h, norm_g.reshape(1, d), w_router, tri)


def _slot_map_kernel(d1_ref, d2_ref, tok_ref):
    def clear(slot, _):
        tok_ref[slot] = 0
        return 0

    lax.fori_loop(0, tok_ref.shape[0], clear, 0, unroll=16)

    def place(tok, _):
        tok_ref[d1_ref[tok]] = tok
        tok_ref[d2_ref[tok]] = tok
        return 0

    lax.fori_loop(0, d1_ref.shape[0], place, 0, unroll=8)


def _slot_map(dest1, dest2, n_slots):
    smem = pl.BlockSpec(memory_space=pltpu.SMEM)
    return pl.pallas_call(
        _slot_map_kernel,
        in_specs=[smem, smem], out_specs=smem,
        out_shape=jax.ShapeDtypeStruct((n_slots,), jnp.int32),
        name="moe_slot_map",
    )(dest1, dest2)


def _row_copy(src_ref, src_row, dst_ref, dst_row, sem):
    return pltpu.make_async_copy(src_ref.at[pl.ds(src_row, 1)], dst_ref.at[pl.ds(dst_row, 1)], sem)


def _expert_ffn_kernel(be_ref, bv_ref, tok_ref, h_ref, g_ref, wg_ref, wu_ref, wd_ref, ys_ref, buf_ref, sem, *, f_chunk):
    del be_ref
    i = pl.program_id(0)
    last = pl.num_programs(0) - 1
    cur = i % 2

    def gather(block, buf):
        base = block * MOE_BLOCK
        for r in range(MOE_BLOCK):
            _row_copy(h_ref, tok_ref[base + r], buf_ref.at[buf], r, sem.at[buf]).start(priority=r % 2)

    def wait_gather(buf):
        pltpu.make_async_copy(h_ref.at[pl.ds(0, MOE_BLOCK)], buf_ref.at[buf], sem.at[buf]).wait()

    @pl.when(i == 0)
    def _():
        gather(0, 0)

    gather(jnp.minimum(i + 1, last), 1 - cur)
    wait_gather(cur)
    valid = bv_ref[i] != 0

    @pl.when(valid)
    def _():
        x = buf_ref[cur]
        xn = (_rms(x, x.shape[-1]) * g_ref[...]).astype(BF16)
        ys_ref[...] = _swiglu(xn, wg_ref, wu_ref, wd_ref, f_chunk)

    @pl.when(jnp.logical_not(valid))
    def _():
        ys_ref[...] = jnp.zeros_like(ys_ref)

    @pl.when(i == last)
    def _():
        wait_gather(1 - cur)


def _expert_ffn(h, norm_g, slot_tok, blk_expert, blk_valid, w_gate, w_up, w_down, f_chunk):
    n_slots = slot_tok.shape[0]
    d = h.shape[-1]
    d_ff = w_gate.shape[-1]
    return pl.pallas_call(
        functools.partial(_expert_ffn_kernel, f_chunk=f_chunk),
        grid_spec=pltpu.PrefetchScalarGridSpec(
            num_scalar_prefetch=3, grid=(n_slots // MOE_BLOCK,),
            in_specs=[pl.BlockSpec(memory_space=pl.ANY),
                      pl.BlockSpec((1, d), lambda i, be, bv, st: (0, 0)),
                      pl.BlockSpec((None, d, d_ff), lambda i, be, bv, st: (be[i], 0, 0)),
                      pl.BlockSpec((None, d, d_ff), lambda i, be, bv, st: (be[i], 0, 0)),
                      pl.BlockSpec((None, d_ff, d), lambda i, be, bv, st: (be[i], 0, 0))],
            out_specs=pl.BlockSpec((MOE_BLOCK, d), lambda i, be, bv, st: (i, 0)),
            scratch_shapes=[pltpu.VMEM((2, MOE_BLOCK, d), F32), pltpu.SemaphoreType.DMA((2,))]),
        out_shape=jax.ShapeDtypeStruct((n_slots, d), F32),
        compiler_params=_params("arbitrary"),
        name="expert_swiglu",
    )(blk_expert, blk_valid, slot_tok, h, norm_g.reshape(1, d), w_gate.astype(BF16), w_up.astype(BF16),
      w_down.astype(BF16))


def _combine_kernel(d1_ref, d2_ref, h_ref, meta_ref, ys_ref, o_ref, a_ref, b_ref, sem, *, tm):
    base = pl.program_id(0) * tm

    def issue(r, _):
        tok = base + r
        _row_copy(ys_ref, d1_ref[tok], a_ref, r, sem).start(priority=0)
        _row_copy(ys_ref, d2_ref[tok], b_ref, r, sem).start(priority=1)
        return 0

    lax.fori_loop(0, tm, issue, 0, unroll=8)

    pltpu.make_async_copy(ys_ref.at[pl.ds(0, tm)], a_ref, sem).wait()
    pltpu.make_async_copy(ys_ref.at[pl.ds(0, tm)], b_ref, sem).wait()
    meta = meta_ref[...]
    g1 = meta[:, META_G1:META_G1 + 1]
    g2 = meta[:, META_G2:META_G2 + 1]
    o_ref[...] = h_ref[...] + (g1 * a_ref[...] + g2 * b_ref[...])


def _combine(h, meta, ys, dest1, dest2, tm):
    t, d = h.shape
    row = lambda i, d1, d2: (i, 0)
    return pl.pallas_call(
        functools.partial(_combine_kernel, tm=tm),
        grid_spec=pltpu.PrefetchScalarGridSpec(
            num_scalar_prefetch=2, grid=(t // tm,),
            in_specs=[pl.BlockSpec((tm, d), row), pl.BlockSpec((tm, N_EXPERTS), row),
                      pl.BlockSpec(memory_space=pl.ANY)],
            out_specs=pl.BlockSpec((tm, d), row),
            scratch_shapes=[pltpu.VMEM((tm, d), F32), pltpu.VMEM((tm, d), F32), pltpu.SemaphoreType.DMA(())]),
        out_shape=jax.ShapeDtypeStruct((t, d), F32),
        compiler_params=_params("arbitrary"),
        name="moe_combine",
    )(dest1, dest2, h, meta, ys)


def _moe_layer(h, norm_g, w_router, w_gate, w_up, w_down, tm, f_chunk):
    t, d = h.shape
    meta, counts = _router(h, norm_g, w_router, tm)
    counts = counts.reshape(N_EXPERTS).astype(jnp.int32)
    padded = (counts + MOE_BLOCK - 1) // MOE_BLOCK * MOE_BLOCK
    pad_end = jnp.cumsum(padded)
    pad_start = pad_end - padded
    e1 = meta[:, META_E1].astype(jnp.int32)
    e2 = meta[:, META_E2].astype(jnp.int32)
    dest1 = pad_start[e1] + meta[:, META_R1].astype(jnp.int32)
    dest2 = pad_start[e2] + meta[:, META_R2].astype(jnp.int32)
    n_slots = 2 * t + N_EXPERTS * MOE_BLOCK
    blk_start = jnp.arange(n_slots // MOE_BLOCK, dtype=jnp.int32) * MOE_BLOCK
    blk_expert = jnp.minimum(jnp.sum(blk_start[:, None] >= pad_end[None, :], axis=1), N_EXPERTS - 1).astype(jnp.int32)
    blk_valid = (blk_start < pad_end[-1]).astype(jnp.int32)
    slot_tok = _slot_map(dest1, dest2, n_slots)
    ys = _expert_ffn(h, norm_g, slot_tok, blk_expert, blk_valid, w_gate, w_up, w_down, f_chunk)
    return _combine(h, meta, ys, dest1, dest2, tm)


def kernel(x, att_norm, att_w_in, att_q_latent_norm, att_w_q_up, att_kv_latent_norm, att_w_kv_up, att_q_norm, att_k_norm, att_w_out, dffn_norm, dffn_w_gate, dffn_w_up, dffn_w_down, ssm_norm, ssm_w_in, ssm_a_re, ssm_a_im, ssm_log_dt, ssm_b_re, ssm_b_im, ssm_c_re, ssm_c_im, ssm_d, ssm_w_glu, moe_norm, moe_router, moe_w_gate, moe_w_up, moe_w_down):
    b, s, d = x.shape
    x2 = x.reshape(b * s, d)
    h = _layer0(x2, b, s, att_norm[0], att_w_in[0], att_q_latent_norm[0], att_w_q_up[0], att_kv_latent_norm[0],
                att_w_kv_up[0], att_q_norm[0], att_k_norm[0], att_w_out[0], dffn_norm[0], dffn_w_gate[0],
                dffn_w_up[0], dffn_w_down[0], tm=512, tq=512, tk=256, f_chunk=512)
    h = _s5_layer(h, b, s, ssm_norm[0], ssm_w_in[0], ssm_a_re[0], ssm_a_im[0], ssm_log_dt[0], ssm_b_re[0],
                  ssm_b_im[0], ssm_c_re[0], ssm_c_im[0], ssm_d[0], ssm_w_glu[0])
    h = _moe_layer(h, moe_norm[0], moe_router[0], moe_w_gate[0], moe_w_up[0], moe_w_down[0], tm=256, f_chunk=512)
    return h.reshape(b, s, d)
```

```python
import functools
import math

import jax
import jax.numpy as jnp
from jax import lax
from jax.experimental import pallas as pl
from jax.experimental.pallas import tpu as pltpu

F32 = jnp.float32
BF16 = jnp.bfloat16

EPS = 1e-6
LANES = 128
SB_HEADS = 8
SB_HEAD_DIM = 64
SB_WIDTH = SB_HEADS * SB_HEAD_DIM
MLA_HEADS = 8
MLA_Q_RANK = 256
MLA_KV_RANK = 128
MLA_NOPE_DIM = 64
MLA_ROPE_DIM = 32
MLA_QK_DIM = MLA_NOPE_DIM + MLA_ROPE_DIM
MLA_V_DIM = 64
MLA_WIDTH = MLA_HEADS * MLA_V_DIM
ROPE_THETA = 10000.0
SSM_GROUP = 16
SSM_GROUPS = 32
SSM_STATE = 64
SSM_WIDTH = SSM_GROUP * SSM_GROUPS
SSM_NSTATE = SSM_GROUPS * SSM_STATE
EIG_RE_MAX = -1e-4
N_EXPERTS = 8
VMEM_LIMIT = 56 * 1024 * 1024


def _params(*sem):
    return pltpu.CompilerParams(dimension_semantics=sem, vmem_limit_bytes=VMEM_LIMIT)


def _rms(x, n):
    return x * lax.rsqrt(jnp.sum(x * x, axis=-1, keepdims=True) * (1.0 / n) + EPS)


def _dot(a, b):
    return jnp.dot(a, b, preferred_element_type=F32)


def _dot_nt(a, b):
    return lax.dot_general(a, b, (((1,), (1,)), ((), ())), preferred_element_type=F32)


def _const_spec(shape):
    return pl.BlockSpec(shape, lambda *_: (0,) * len(shape))


IN_SB = 3 * SB_WIDTH
IN_CQ = IN_SB
IN_CKV = IN_CQ + MLA_Q_RANK
IN_KR = IN_CKV + MLA_KV_RANK
IN_KRS = IN_KR + LANES
IN_COLS_PAD = IN_KRS + LANES
MLA_PAD = MLA_HEADS * LANES
MLA_V_ROWS = MLA_V_DIM + 16
MLA_VT_WIDTH = MLA_HEADS * MLA_V_ROWS


def _inproj_kernel(x_ref, g_ref, win_ref, qlg_ref, wq_ref, kvlg_ref, wkv_ref, ctab_ref, stab_ref,
                   qg_ref, qgs_ref, kg_ref, kgs_ref, qkv_ref, qm_ref, km_ref, vm_ref):
    x = x_ref[...]
    xn = _rms(x, x.shape[-1]) * g_ref[...]
    hh = _dot(xn.astype(BF16), win_ref[...])
    qkv_ref[:, :SB_WIDTH] = (hh[:, :SB_WIDTH] * (SB_HEAD_DIM ** -0.5)).astype(BF16)
    qkv_ref[:, SB_WIDTH:] = hh[:, SB_WIDTH:IN_SB].astype(BF16)
    cq = _rms(hh[:, IN_CQ:IN_CKV], MLA_Q_RANK) * qlg_ref[...]
    qf = _dot(cq.astype(BF16), wq_ref[...])
    ckv = _rms(hh[:, IN_CKV:IN_KR], MLA_KV_RANK) * kvlg_ref[...]
    kvf = _dot(ckv.astype(BF16), wkv_ref[...])
    v_col = lax.broadcasted_iota(jnp.int32, (1, MLA_VT_WIDTH), 1)
    v_ones = jnp.where(v_col % MLA_V_ROWS >= MLA_V_DIM, 1.0, 0.0)
    vm_ref[...] = (kvf[:, MLA_PAD:] + v_ones).T.astype(BF16)
    kr = hh[:, IN_KR:IN_KRS]
    krs = hh[:, IN_KRS:IN_COLS_PAD]
    ctab = ctab_ref[...]
    stab = stab_ref[...]
    scale = MLA_QK_DIM ** -0.5 * math.log2(math.e)
    cq_t = ctab * (qg_ref[...] * scale)
    sq_t = stab * (qgs_ref[...] * scale)
    ck_t = ctab * kg_ref[...]
    sk_t = stab * kgs_ref[...]
    for h in range(MLA_HEADS):
        lo, hi = h * LANES, (h + 1) * LANES
        qh = qf[:, lo:hi]
        qs = qf[:, MLA_PAD + lo:MLA_PAD + hi]
        q_inv = lax.rsqrt(jnp.sum(qh * qh, axis=-1, keepdims=True) * (1.0 / MLA_QK_DIM) + EPS)
        qm_ref[:, lo:hi] = ((qh * cq_t + qs * sq_t) * q_inv).astype(BF16)
        kh = kvf[:, lo:hi] + kr
        k_inv = lax.rsqrt(jnp.sum(kh * kh, axis=-1, keepdims=True) * (1.0 / MLA_QK_DIM) + EPS)
        km_ref[:, lo:hi] = ((kh * ck_t + krs * sk_t) * k_inv).astype(BF16)


def _pad_heads(w, n_heads, width):
    k = w.shape[0]
    w = w.reshape(k, n_heads, width)
    return jnp.pad(w, ((0, 0), (0, 0), (0, LANES - width))).reshape(k, n_heads * LANES)


def _swap_rope(w):
    half = MLA_ROPE_DIM // 2
    return jnp.concatenate([jnp.zeros_like(w[..., :MLA_NOPE_DIM]),
                            w[..., MLA_NOPE_DIM + half:], w[..., MLA_NOPE_DIM:MLA_NOPE_DIM + half]], axis=-1)


def _rope_lane_tables(s_len):
    inv_freq = ROPE_THETA ** (-jnp.arange(0, MLA_ROPE_DIM, 2, dtype=F32) / MLA_ROPE_DIM)
    ang = jnp.arange(s_len, dtype=F32)[:, None] * inv_freq[None, :]
    cos, sin = jnp.cos(ang), jnp.sin(ang)
    ones = jnp.ones((s_len, MLA_NOPE_DIM), F32)
    zeros_n = jnp.zeros((s_len, MLA_NOPE_DIM), F32)
    zeros_t = jnp.zeros((s_len, LANES - MLA_QK_DIM), F32)
    ctab = jnp.concatenate([ones, cos, cos, zeros_t], axis=-1)
    stab = jnp.concatenate([zeros_n, -sin, sin, zeros_t], axis=-1)
    return ctab, stab


def _in_projection(x2, s_len, norm_g, w_in, q_lat_g, w_q_up, kv_lat_g, w_kv_up, q_g, k_g, tm):
    t, d = x2.shape
    off_kr = 3 * SB_WIDTH + MLA_Q_RANK + MLA_KV_RANK
    w_rope = w_in[:, off_kr:]
    zeros_n = jnp.zeros((d, MLA_NOPE_DIM), F32)
    zeros_t = jnp.zeros((d, LANES - MLA_QK_DIM), F32)
    half = MLA_ROPE_DIM // 2
    w_in_pad = jnp.concatenate(
        [w_in[:, :off_kr], zeros_n, w_rope, zeros_t,
         zeros_n, w_rope[:, half:], w_rope[:, :half], zeros_t], axis=-1).astype(BF16)
    wq3 = w_q_up.reshape(MLA_Q_RANK, MLA_HEADS, MLA_QK_DIM)
    wq_pad = jnp.concatenate(
        [_pad_heads(w_q_up, MLA_HEADS, MLA_QK_DIM),
         _pad_heads(_swap_rope(wq3).reshape(MLA_Q_RANK, -1), MLA_HEADS, MLA_QK_DIM)], axis=-1).astype(BF16)
    wkv3 = w_kv_up.reshape(MLA_KV_RANK, MLA_HEADS, MLA_NOPE_DIM + MLA_V_DIM)
    wkv_pad = jnp.concatenate(
        [_pad_heads(wkv3[..., :MLA_NOPE_DIM].reshape(MLA_KV_RANK, -1), MLA_HEADS, MLA_NOPE_DIM),
         jnp.pad(wkv3[..., MLA_NOPE_DIM:], ((0, 0), (0, 0), (0, MLA_V_ROWS - MLA_V_DIM))).reshape(MLA_KV_RANK, -1)],
        axis=-1).astype(BF16)
    ctab, stab = _rope_lane_tables(s_len)
    pad_g = lambda g: jnp.pad(g, (0, LANES - MLA_QK_DIM)).reshape(1, LANES)
    swap_g = lambda g: jnp.pad(jnp.concatenate(
        [jnp.zeros((MLA_NOPE_DIM,), F32), g[MLA_NOPE_DIM + half:], g[MLA_NOPE_DIM:MLA_NOPE_DIM + half]]),
        (0, LANES - MLA_QK_DIM)).reshape(1, LANES)
    n_pos = s_len // tm
    row = lambda i: (i, 0)
    return pl.pallas_call(
        _inproj_kernel,
        grid=(t // tm,),
        in_specs=[
            pl.BlockSpec((tm, d), row),
            _const_spec((1, d)),
            _const_spec((d, IN_COLS_PAD)),
            _const_spec((1, MLA_Q_RANK)),
            _const_spec((MLA_Q_RANK, 2 * MLA_PAD)),
            _const_spec((1, MLA_KV_RANK)),
            _const_spec((MLA_KV_RANK, MLA_PAD + MLA_VT_WIDTH)),
            pl.BlockSpec((tm, LANES), lambda i: (i % n_pos, 0)),
            pl.BlockSpec((tm, LANES), lambda i: (i % n_pos, 0)),
            _const_spec((1, LANES)), _const_spec((1, LANES)),
            _const_spec((1, LANES)), _const_spec((1, LANES)),
        ],
        out_specs=[
            pl.BlockSpec((tm, IN_SB), row),
            pl.BlockSpec((tm, MLA_PAD), row),
            pl.BlockSpec((tm, MLA_PAD), row),
            pl.BlockSpec((MLA_VT_WIDTH, tm), lambda i: (0, i)),
        ],
        out_shape=[
            jax.ShapeDtypeStruct((t, IN_SB), BF16),
            jax.ShapeDtypeStruct((t, MLA_PAD), BF16),
            jax.ShapeDtypeStruct((t, MLA_PAD), BF16),
            jax.ShapeDtypeStruct((MLA_VT_WIDTH, t), BF16),
        ],
        compiler_params=_params("arbitrary"),
        name="in_projection",
    )(x2, norm_g.reshape(1, d), w_in_pad, q_lat_g.reshape(1, -1), wq_pad, kv_lat_g.reshape(1, -1), wkv_pad,
      ctab, stab, pad_g(q_g), swap_g(q_g), pad_g(k_g), swap_g(k_g))


SB_EXP_UNDERFLOW = 110.0


def _sb_kernel(q_ref, k_ref, v_ref, u_ref, o_ref, acc_ref, carry_ref, z_ref, *, tq, tk):
    qi = pl.program_id(2)
    lane = lax.broadcasted_iota(jnp.int32, (1, LANES), 1)
    q = q_ref[...]
    q_heads = [jnp.where((lane >= j * SB_HEAD_DIM) & (lane < (j + 1) * SB_HEAD_DIM), q, jnp.zeros_like(q))
               for j in range(2)]
    acc_ref[...] = jnp.zeros_like(acc_ref)
    carry_ref[...] = jnp.zeros_like(carry_ref)

    def logits(j, c, lo_row=0, hi_row=tq):
        start = pl.multiple_of(c * tk, tk)
        z_ref[j, lo_row:hi_row, :] = _dot_nt(q_heads[j][lo_row:hi_row], k_ref[pl.ds(start, tk), :])

    def absorb(j, c, diagonal, lo_row=0, hi_row=tq):
        start = pl.multiple_of(c * tk, tk)
        z = z_ref[j, lo_row:hi_row, :]
        log_fail = -(jnp.maximum(z, 0.0) + jnp.log(1.0 + jnp.exp(-jnp.abs(z))))
        if diagonal:
            key = start + lax.broadcasted_iota(jnp.int32, z.shape, 1)
            qry = qi * tq + lo_row + lax.broadcasted_iota(jnp.int32, z.shape, 0)
            earlier = key < qry
            log_fail = jnp.where(earlier, log_fail, 0.0)
        hi = log_fail.astype(BF16)
        lo = (log_fail - hi.astype(F32)).astype(BF16)
        log_stick = _dot(hi, u_ref[...]) + _dot(lo, u_ref[...])
        carry = carry_ref[j, lo_row:hi_row, :]
        w = jnp.exp(z + log_fail + log_stick + jnp.tile(carry, (1, tk // LANES)))
        if diagonal:
            w = jnp.where(earlier, w, 0.0)
        acc_ref[j, lo_row:hi_row, :] += _dot(w.astype(BF16), v_ref[pl.ds(start, tk), :])
        carry_ref[j, lo_row:hi_row, :] = carry + (log_stick[:, :1] + log_fail[:, :1])

    def sweep(c, diagonal, lo_row=0, next_lo_row=0, hi_row=tq):
        logits(1, c, lo_row, hi_row)
        absorb(0, c, diagonal, lo_row, hi_row)
        logits(0, jnp.maximum(c - 1, 0), next_lo_row)
        absorb(1, c, diagonal, lo_row, hi_row)

    n_sub = tq // tk
    first = qi * n_sub + n_sub - 1
    logits(0, first, (n_sub - 1) * tk)
    for sub in reversed(range(n_sub)):
        sweep(qi * n_sub + sub, True, sub * tk, max(sub - 1, 0) * tk)

    def sticks_alive(lo_row, hi_row):
        return jnp.max(carry_ref[:, lo_row:hi_row, :]) > -SB_EXP_UNDERFLOW

    def earlier_chunk(state):
        c, _, rest_alive = state

        @pl.when(rest_alive)
        def _():
            sweep(c, False)

        @pl.when(jnp.logical_not(rest_alive))
        def _():
            sweep(c, False, hi_row=tk)

        return c - 1, sticks_alive(0, tk), sticks_alive(tk, tq)

    lax.while_loop(lambda state: (state[0] >= 0) & (state[1] | state[2]), earlier_chunk,
                   (qi * n_sub - 1, sticks_alive(0, tk), sticks_alive(tk, tq)))
    o_ref[...] = jnp.where(lane < SB_HEAD_DIM, acc_ref[0], acc_ref[1]).astype(o_ref.dtype)


def _sb_attention(qkv, tq, tk):
    b, s, _ = qkv.shape
    assert tq % tk == 0 and tq > tk and s % tq == 0
    n_pairs = SB_WIDTH // LANES
    tri = (jnp.arange(tk)[:, None] > jnp.arange(tk)[None, :]).astype(BF16)
    return pl.pallas_call(
        functools.partial(_sb_kernel, tq=tq, tk=tk),
        grid=(b, n_pairs, s // tq),
        in_specs=[
            pl.BlockSpec((None, tq, LANES), lambda bi, hp, qi: (bi, qi, hp)),
            pl.BlockSpec((None, s, LANES), lambda bi, hp, qi: (bi, 0, n_pairs + hp)),
            pl.BlockSpec((None, s, LANES), lambda bi, hp, qi: (bi, 0, 2 * n_pairs + hp)),
            _const_spec((tk, tk)),
        ],
        out_specs=pl.BlockSpec((None, tq, LANES), lambda bi, hp, qi: (bi, qi, hp)),
        out_shape=jax.ShapeDtypeStruct((b, s, SB_WIDTH), BF16),
        scratch_shapes=[pltpu.VMEM((2, tq, LANES), F32), pltpu.VMEM((2, tq, LANES), F32),
                        pltpu.VMEM((2, tq, tk), F32)],
        compiler_params=_params("arbitrary", "arbitrary", "arbitrary"),
        name="stick_breaking_attention",
    )(qkv, qkv, qkv, tri)


def _mla_kernel(q_ref, k_ref, vt_ref, o_ref, acc_ref, m_ref, s_ref, *, tq):
    qi = pl.program_id(2)
    acc_ref[...] = jnp.zeros_like(acc_ref)
    m_ref[...] = jnp.full(m_ref.shape, -jnp.inf, F32)

    def scores(j, c):
        start = pl.multiple_of(c * tq, tq)
        s_ref[j] = _dot_nt(k_ref[pl.ds(start, tq), j * LANES:(j + 1) * LANES],
                           q_ref[:, j * LANES:(j + 1) * LANES])

    def update(j, c, diagonal, k0=0, k1=tq, q0=0):
        start = pl.multiple_of(c * tq + k0, LANES)
        s = s_ref[j, k0:k1, q0:]
        if diagonal:
            key = k0 + lax.broadcasted_iota(jnp.int32, s.shape, 0)
            qry = q0 + lax.broadcasted_iota(jnp.int32, s.shape, 1)
            s = jnp.where(key <= qry, s, -jnp.inf)
        m_prev = m_ref[j, :, q0:]
        m_new = jnp.maximum(m_prev, jnp.max(s, axis=0, keepdims=True))
        p = jnp.exp2(s - m_new)
        vt = vt_ref[j * MLA_V_ROWS:(j + 1) * MLA_V_ROWS, pl.ds(start, k1 - k0)]
        acc_ref[j, :, q0:] = jnp.exp2(m_prev - m_new) * acc_ref[j, :, q0:] + _dot(vt, p.astype(BF16))
        m_ref[j, :, q0:] = m_new

    def update_diagonal(j):
        half = tq // 2
        update(j, qi, True, 0, half, 0)
        update(j, qi, True, half, tq, half)


    scores(0, 0)

    def earlier_block(c, _):
        scores(1, c)
        update(0, c, False)
        scores(0, c + 1)
        update(1, c, False)
        return 0

    lax.fori_loop(0, qi, earlier_block, 0)
    scores(1, qi)
    update_diagonal(0)
    update_diagonal(1)
    out_t = jnp.concatenate([acc_ref[j, :MLA_V_DIM, :] / acc_ref[j, MLA_V_DIM:MLA_V_DIM + 1, :] for j in range(2)],
                            axis=0)
    o_ref[...] = out_t.T.astype(o_ref.dtype)


def _mla_attention(q, k, vt, tq):
    b, s, _ = q.shape
    n_pairs = MLA_WIDTH // LANES
    return pl.pallas_call(
        functools.partial(_mla_kernel, tq=tq),
        grid=(b, n_pairs, s // tq),
        in_specs=[
            pl.BlockSpec((None, tq, 2 * LANES), lambda bi, hp, qi: (bi, qi, hp)),
            pl.BlockSpec((None, s, 2 * LANES), lambda bi, hp, qi: (bi, 0, hp)),
            pl.BlockSpec((2 * MLA_V_ROWS, s), lambda bi, hp, qi: (hp, bi)),
        ],
        out_specs=pl.BlockSpec((None, tq, LANES), lambda bi, hp, qi: (bi, qi, hp)),
        out_shape=jax.ShapeDtypeStruct((b, s, MLA_WIDTH), BF16),
        scratch_shapes=[pltpu.VMEM((2, MLA_V_ROWS, tq), F32), pltpu.VMEM((2, 1, tq), F32),
                        pltpu.VMEM((2, tq, tq), F32)],
        compiler_params=_params("arbitrary", "arbitrary", "arbitrary"),
        name="latent_attention",
    )(q, k, vt)


def _swiglu(x, wg_ref, wu_ref, wd_ref, f_chunk):
    d_ff = wg_ref.shape[-1]
    acc = None
    for c in range(d_ff // f_chunk):
        cols = slice(c * f_chunk, (c + 1) * f_chunk)
        gate = _dot(x, wg_ref[:, cols])
        up = _dot(x, wu_ref[:, cols])
        act = (gate * jax.nn.sigmoid(gate) * up).astype(BF16)
        part = _dot(act, wd_ref[cols, :])
        acc = part if acc is None else acc + part
    return acc


def _outproj_ffn_kernel(x_ref, osb_ref, omla_ref, wsb_ref, wmla_ref, g_ref, wg_ref, wu_ref, wd_ref, o_ref, *, f_chunk):
    h = x_ref[...] + _dot(osb_ref[...], wsb_ref[...]) + _dot(omla_ref[...], wmla_ref[...])
    hn = (_rms(h, h.shape[-1]) * g_ref[...]).astype(BF16)
    o_ref[...] = h + _swiglu(hn, wg_ref, wu_ref, wd_ref, f_chunk)


def _outproj_dense_ffn(x2, o_sb, o_mla, w_out, norm_g, w_gate, w_up, w_down, tm, f_chunk):
    t, d = x2.shape
    d_ff = w_gate.shape[-1]
    row = lambda i: (i, 0)
    resident = lambda shape: pl.BlockSpec(shape, lambda i: (0, 0), pipeline_mode=pl.Buffered(1))
    return pl.pallas_call(
        functools.partial(_outproj_ffn_kernel, f_chunk=f_chunk),
        grid=(t // tm,),
        in_specs=[pl.BlockSpec((tm, d), row), pl.BlockSpec((tm, SB_WIDTH), row), pl.BlockSpec((tm, MLA_WIDTH), row),
                  resident((SB_WIDTH, d)), resident((MLA_WIDTH, d)), resident((1, d)),
                  resident((d, d_ff)), resident((d, d_ff)), resident((d_ff, d))],
        out_specs=pl.BlockSpec((tm, d), row),
        out_shape=jax.ShapeDtypeStruct((t, d), F32),
        compiler_params=_params("arbitrary"),
        name="out_projection_dense_swiglu",
    )(x2, o_sb, o_mla, w_out[:SB_WIDTH].astype(BF16), w_out[SB_WIDTH:].astype(BF16), norm_g.reshape(1, d),
      w_gate.astype(BF16), w_up.astype(BF16), w_down.astype(BF16))


DENSE_FFN_TILE = 512
MLA_Q_TILE = 1024


def _layer0(x2, b, s, att_norm, att_w_in, q_lat_g, w_q_up, kv_lat_g, w_kv_up, q_g, k_g, w_out,
            dffn_norm, w_gate, w_up, w_down, tm, tq, tk, f_chunk):
    qkv, qm, km, vm = _in_projection(x2, s, att_norm, att_w_in, q_lat_g, w_q_up, kv_lat_g, w_kv_up, q_g, k_g, tm)
    o_sb = _sb_attention(qkv.reshape(b, s, -1), tq, tk).reshape(b * s, -1)
    o_mla = _mla_attention(qm.reshape(b, s, -1), km.reshape(b, s, -1), vm, MLA_Q_TILE).reshape(b * s, -1)
    return _outproj_dense_ffn(x2, o_sb, o_mla, w_out, dffn_norm, w_gate, w_up, w_down, DENSE_FFN_TILE, f_chunk)


SSM_SEGS = 8
SSM_SEG_LEN = 64
SSM_TILE = SSM_SEGS * SSM_SEG_LEN
SSM_CH_CHUNK = LANES
SSM_ST_CHUNK = SSM_CH_CHUNK // SSM_GROUP * SSM_STATE
SSM_SCAN_TILES = 8


def _s5_prep_kernel(are_ref, aim_ref, logdt_ref, bre_ref, bim_ref, cre_ref, cim_ref,
                    wbu_ref, wc_ref, pow_ref):
    n = SSM_NSTATE
    a_re = jnp.minimum(are_ref[...], EIG_RE_MAX)
    a_im = aim_ref[...]
    dt = jnp.exp(logdt_ref[...])
    mag = jnp.exp(a_re * dt)
    lam_re = mag * jnp.cos(a_im * dt)
    lam_im = mag * jnp.sin(a_im * dt)
    den = a_re * a_re + a_im * a_im
    num_re = lam_re - 1.0
    coef_re = (num_re * a_re + lam_im * a_im) / den
    coef_im = (lam_im * a_re - num_re * a_im) / den
    b_re = bre_ref[...]
    b_im = bim_ref[...]
    in_group = (lax.broadcasted_iota(jnp.int32, (SSM_CH_CHUNK, n), 0) // SSM_GROUP
                == lax.broadcasted_iota(jnp.int32, (SSM_CH_CHUNK, n), 1) % SSM_ST_CHUNK // SSM_STATE)
    wbu_ref[:, :n] = jnp.where(in_group, coef_re * b_re - coef_im * b_im, 0.0).astype(BF16)
    wbu_ref[:, n:] = jnp.where(in_group, coef_re * b_im + coef_im * b_re, 0.0).astype(BF16)
    in_group_t = (lax.broadcasted_iota(jnp.int32, (n, SSM_CH_CHUNK), 0) % SSM_ST_CHUNK // SSM_STATE
                  == lax.broadcasted_iota(jnp.int32, (n, SSM_CH_CHUNK), 1) // SSM_GROUP)
    wc_ref[:n, :] = jnp.where(in_group_t, cre_ref[...], 0.0).astype(BF16)
    wc_ref[n:, :] = jnp.where(in_group_t, -cim_ref[...], 0.0).astype(BF16)
    steps = jnp.where(lax.broadcasted_iota(jnp.int32, (SSM_SEGS, n), 0) == 0, 1.0, float(SSM_SEG_LEN))
    mag_k = jnp.exp(steps * (a_re * dt))
    ang_k = steps * (a_im * dt)
    pow_ref[:, :n] = mag_k * jnp.cos(ang_k)
    pow_ref[:, n:] = mag_k * jnp.sin(ang_k)


def _s5_prepare(a_re, a_im, log_dt, b_re, b_im, c_re, c_im):
    n = SSM_NSTATE
    row = lambda a: a.reshape(1, n)
    reps = SSM_CH_CHUNK // SSM_GROUP
    b_t = lambda w: jnp.tile(w.transpose(2, 0, 1).reshape(SSM_GROUP, n), (reps, 1))
    c_t = lambda w: jnp.tile(w.transpose(0, 2, 1).reshape(n, SSM_GROUP), (1, reps))
    return pl.pallas_call(
        _s5_prep_kernel,
        out_shape=[jax.ShapeDtypeStruct((SSM_CH_CHUNK, 2 * n), BF16),
                   jax.ShapeDtypeStruct((2 * n, SSM_CH_CHUNK), BF16),
                   jax.ShapeDtypeStruct((SSM_SEGS, 2 * n), F32)],
        compiler_params=pltpu.CompilerParams(vmem_limit_bytes=VMEM_LIMIT),
        name="s5_discretize",
    )(row(a_re), row(a_im), row(jnp.repeat(log_dt, SSM_STATE)), b_t(b_re), b_t(b_im), c_t(c_re), c_t(c_im))


def _s5_kernel(h_ref, g_ref, win_ref, wbu_ref, pow_ref, wc_ref, d_ref, wglu_ref, o_ref,
               ut_ref, up_ref, st_ref, carry_ref, yp_ref, yt_ref):
    n = SSM_NSTATE
    nt = n // LANES
    n_ch = SSM_WIDTH // LANES
    per = SSM_ST_CHUNK // LANES
    d_model = h_ref.shape[-1]

    @pl.when(pl.program_id(1) == 0)
    def _():
        carry_ref[...] = jnp.zeros_like(carry_ref)

    x = h_ref[...]
    xn = _rms(x, d_model) * g_ref[...]
    u = _dot(xn.astype(BF16), win_ref[...])
    for k in range(n_ch):
        ut_ref[k] = u[:, k * LANES:(k + 1) * LANES]

    def to_step_major(i, _):
        dst = pl.ds(pl.multiple_of(i * SSM_SEGS, SSM_SEGS), SSM_SEGS)
        for k in range(n_ch):
            up_ref[k, dst, :] = ut_ref[k, pl.ds(i, SSM_SEGS, stride=SSM_SEG_LEN), :]
        return 0

    lax.fori_loop(0, SSM_SEG_LEN, to_step_major, 0)

    for c in range(n_ch):
        u16 = up_ref[c].astype(BF16)
        for part in range(2):
            cols = slice(part * n + c * SSM_ST_CHUNK, part * n + (c + 1) * SSM_ST_CHUNK)
            bu = _dot(u16, wbu_ref[:, cols])
            for k in range(per):
                st_ref[part * nt + c * per + k] = bu[:, k * LANES:(k + 1) * LANES]

    for grp in range(nt // SSM_SCAN_TILES):
        tiles = range(grp * SSM_SCAN_TILES, (grp + 1) * SSM_SCAN_TILES)
        re_l = [slice(k * LANES, (k + 1) * LANES) for k in tiles]
        im_l = [slice(n + k * LANES, n + (k + 1) * LANES) for k in tiles]
        lam_re = [jnp.broadcast_to(pow_ref[0:1, c], (SSM_SEGS, LANES)) for c in re_l]
        lam_im = [jnp.broadcast_to(pow_ref[0:1, c], (SSM_SEGS, LANES)) for c in im_l]

        def scan(state, store):
            def step(i, state):
                rows = pl.ds(pl.multiple_of(i * SSM_SEGS, SSM_SEGS), SSM_SEGS)
                new_state = []
                for idx, k in enumerate(tiles):
                    s_re, s_im = state[idx]
                    n_re = lam_re[idx] * s_re - lam_im[idx] * s_im + st_ref[k, rows, :]
                    n_im = lam_re[idx] * s_im + lam_im[idx] * s_re + st_ref[nt + k, rows, :]
                    if store:
                        st_ref[k, rows, :] = n_re
                        st_ref[nt + k, rows, :] = n_im
                    new_state.append((n_re, n_im))
                return tuple(new_state)

            return lax.fori_loop(0, SSM_SEG_LEN, step, state)

        zero = jnp.zeros((SSM_SEGS, LANES), F32)
        ends = scan(tuple((zero, zero) for _ in tiles), False)
        starts = []
        for idx in range(len(tiles)):
            e_re, e_im = ends[idx]
            p_re = pow_ref[1:2, re_l[idx]]
            p_im = pow_ref[1:2, im_l[idx]]
            c_re = carry_ref[:, re_l[idx]]
            c_im = carry_ref[:, im_l[idx]]
            rows_re, rows_im = [], []
            for j in range(SSM_SEGS):
                rows_re.append(c_re)
                rows_im.append(c_im)
                c_re, c_im = (e_re[j:j + 1] + (p_re * c_re - p_im * c_im),
                              e_im[j:j + 1] + (p_re * c_im + p_im * c_re))
            carry_ref[:, re_l[idx]] = c_re
            carry_ref[:, im_l[idx]] = c_im
            starts.append((jnp.concatenate(rows_re, axis=0), jnp.concatenate(rows_im, axis=0)))
        scan(tuple(starts), True)

    for c in range(n_ch):
        h_re = jnp.concatenate([st_ref[c * per + k] for k in range(per)], axis=-1).astype(BF16)
        h_im = jnp.concatenate([st_ref[nt + c * per + k] for k in range(per)], axis=-1).astype(BF16)
        rows_re = slice(c * SSM_ST_CHUNK, (c + 1) * SSM_ST_CHUNK)
        rows_im = slice(n + c * SSM_ST_CHUNK, n + (c + 1) * SSM_ST_CHUNK)
        yp_ref[c] = _dot(h_re, wc_ref[rows_re, :]) + _dot(h_im, wc_ref[rows_im, :])

    def to_token_major(i, _):
        src = pl.ds(pl.multiple_of(i * SSM_SEGS, SSM_SEGS), SSM_SEGS)
        for k in range(n_ch):
            yt_ref[k, pl.ds(i, SSM_SEGS, stride=SSM_SEG_LEN), :] = yp_ref[k, src, :]
        return 0

    lax.fori_loop(0, SSM_SEG_LEN, to_token_major, 0)
    y = jnp.concatenate([yt_ref[k] for k in range(n_ch)], axis=-1)
    y = jax.nn.gelu(y + d_ref[...] * u)
    z = _dot(y.astype(BF16), wglu_ref[...])
    o_ref[...] = x + z[:, :d_model] * jax.nn.sigmoid(z[:, d_model:])


def _s5_layer(h, b, s, norm_g, w_in, a_re, a_im, log_dt, b_re, b_im, c_re, c_im, d_skip, w_glu):
    t, d = h.shape
    n = SSM_NSTATE
    w_bu, w_c, powers = _s5_prepare(a_re, a_im, log_dt, b_re, b_im, c_re, c_im)
    tiles = s // SSM_TILE
    resident = lambda shape: pl.BlockSpec(shape, lambda bi, ti: (0, 0), pipeline_mode=pl.Buffered(1))
    row = lambda bi, ti: (bi * tiles + ti, 0)
    ch_tiles = pltpu.VMEM((SSM_WIDTH // LANES, SSM_TILE, LANES), F32)
    return pl.pallas_call(
        _s5_kernel,
        grid=(b, tiles),
        in_specs=[pl.BlockSpec((SSM_TILE, d), row), resident((1, d)), resident((d, SSM_WIDTH)),
                  resident((SSM_CH_CHUNK, 2 * n)), resident((SSM_SEGS, 2 * n)), resident((2 * n, SSM_CH_CHUNK)),
                  resident((1, SSM_WIDTH)), resident((SSM_WIDTH, 2 * d))],
        out_specs=pl.BlockSpec((SSM_TILE, d), row),
        out_shape=jax.ShapeDtypeStruct((t, d), F32),
        scratch_shapes=[ch_tiles, ch_tiles, pltpu.VMEM((2 * n // LANES, SSM_TILE, LANES), F32),
                        pltpu.VMEM((1, 2 * n), F32), ch_tiles, ch_tiles],
        compiler_params=_params("arbitrary", "arbitrary"),
        name="s5_mixer",
    )(h, norm_g.reshape(1, d), w_in.astype(BF16), w_bu, powers, w_c, d_skip.reshape(1, -1), w_glu.astype(BF16))


MOE_BLOCK = 256
META_E1, META_E2, META_G1, META_G2, META_R1, META_R2 = range(6)


def _router_kernel(h_ref, g_ref, wr_ref, tri_ref, meta_ref, counts_ref, run_ref):
    @pl.when(pl.program_id(0) == 0)
    def _():
        run_ref[...] = jnp.zeros_like(run_ref)

    x = h_ref[...]
    xn = _rms(x, x.shape[-1]) * g_ref[...]
    logits = jnp.dot(xn, wr_ref[...], preferred_element_type=F32, precision=lax.Precision.HIGHEST)
    tm = logits.shape[0]
    e = lax.broadcasted_iota(jnp.int32, (tm, N_EXPERTS), 1)
    v1 = jnp.max(logits, axis=-1, keepdims=True)
    i1 = jnp.min(jnp.where(logits == v1, e, N_EXPERTS), axis=-1, keepdims=True)
    rest = jnp.where(e == i1, -jnp.inf, logits)
    v2 = jnp.max(rest, axis=-1, keepdims=True)
    i2 = jnp.min(jnp.where(rest == v2, e, N_EXPERTS), axis=-1, keepdims=True)
    ex = jnp.exp(v2 - v1)
    g1 = 1.0 / (1.0 + ex)
    g2 = ex / (1.0 + ex)
    oh1 = (e == i1).astype(F32)
    oh2 = (e == i2).astype(F32)
    both = oh1 + oh2
    before = _dot(tri_ref[...], both.astype(BF16)) + run_ref[...]
    r1 = jnp.sum(oh1 * before, axis=-1, keepdims=True)
    r2 = jnp.sum(oh2 * before, axis=-1, keepdims=True)
    run_ref[...] += jnp.sum(both, axis=0, keepdims=True)
    counts_ref[...] = run_ref[...]
    meta = jnp.zeros((tm, N_EXPERTS), F32)
    for lane, val in ((META_E1, i1.astype(F32)), (META_E2, i2.astype(F32)), (META_G1, g1), (META_G2, g2),
                      (META_R1, r1), (META_R2, r2)):
        meta = jnp.where(e == lane, val, meta)
    meta_ref[...] = meta


def _router(h, norm_g, w_router, tm):
    t, d = h.shape
    row = lambda i: (i, 0)
    tri = (jnp.arange(tm)[:, None] > jnp.arange(tm)[None, :]).astype(BF16)
    return pl.pallas_call(
        _router_kernel,
        grid=(t // tm,),
        in_specs=[pl.BlockSpec((tm, d), row), _const_spec((1, d)), _const_spec((d, N_EXPERTS)),
                  _const_spec((tm, tm))],
        out_specs=[pl.BlockSpec((tm, N_EXPERTS), row), _const_spec((1, N_EXPERTS))],
        out_shape=[jax.ShapeDtypeStruct((t, N_EXPERTS), F32), jax.ShapeDtypeStruct((1, N_EXPERTS), F32)],
        scratch_shapes=[pltpu.VMEM((1, N_EXPERTS), F32)],
        compiler_params=_params("arbitrary"),
        name="moe_router",
    )(h, norm_g.reshape(1, d), w_router, tri)


def _slot_map_kernel(d1_ref, d2_ref, tok_ref):
    def clear(slot, _):
        tok_ref[slot] = 0
        return 0

    lax.fori_loop(0, tok_ref.shape[0], clear, 0, unroll=16)

    def place(tok, _):
        tok_ref[d1_ref[tok]] = tok
        tok_ref[d2_ref[tok]] = tok
        return 0

    lax.fori_loop(0, d1_ref.shape[0], place, 0, unroll=8)


def _slot_map(dest1, dest2, n_slots):
    smem = pl.BlockSpec(memory_space=pltpu.SMEM)
    return pl.pallas_call(
        _slot_map_kernel,
        in_specs=[smem, smem], out_specs=smem,
        out_shape=jax.ShapeDtypeStruct((n_slots,), jnp.int32),
        name="moe_slot_map",
    )(dest1, dest2)


def _row_copy(src_ref, src_row, dst_ref, dst_row, sem):
    return pltpu.make_async_copy(src_ref.at[pl.ds(src_row, 1)], dst_ref.at[pl.ds(dst_row, 1)], sem)


def _expert_ffn_kernel(be_ref, bv_ref, tok_ref, h_ref, g_ref, wg_ref, wu_ref, wd_ref, ys_ref, buf_ref, sem, *, f_chunk):
    del be_ref
    i = pl.program_id(0)
    last = pl.num_programs(0) - 1
    cur = i % 2

    def gather(block, buf):
        base = block * MOE_BLOCK
        for r in range(MOE_BLOCK):
            _row_copy(h_ref, tok_ref[base + r], buf_ref.at[buf], r, sem.at[buf]).start(priority=r % 2)

    def wait_gather(buf):
        pltpu.make_async_copy(h_ref.at[pl.ds(0, MOE_BLOCK)], buf_ref.at[buf], sem.at[buf]).wait()

    @pl.when(i == 0)
    def _():
        gather(0, 0)

    gather(jnp.minimum(i + 1, last), 1 - cur)
    wait_gather(cur)
    valid = bv_ref[i] != 0

    @pl.when(valid)
    def _():
        x = buf_ref[cur]
        xn = (_rms(x, x.shape[-1]) * g_ref[...]).astype(BF16)
        ys_ref[...] = _swiglu(xn, wg_ref, wu_ref, wd_ref, f_chunk)

    @pl.when(jnp.logical_not(valid))
    def _():
        ys_ref[...] = jnp.zeros_like(ys_ref)

    @pl.when(i == last)
    def _():
        wait_gather(1 - cur)


def _expert_ffn(h, norm_g, slot_tok, blk_expert, blk_valid, w_gate, w_up, w_down, f_chunk):
    n_slots = slot_tok.shape[0]
    d = h.shape[-1]
    d_ff = w_gate.shape[-1]
    return pl.pallas_call(
        functools.partial(_expert_ffn_kernel, f_chunk=f_chunk),
        grid_spec=pltpu.PrefetchScalarGridSpec(
            num_scalar_prefetch=3, grid=(n_slots // MOE_BLOCK,),
            in_specs=[pl.BlockSpec(memory_space=pl.ANY),
                      pl.BlockSpec((1, d), lambda i, be, bv, st: (0, 0)),
                      pl.BlockSpec((None, d, d_ff), lambda i, be, bv, st: (be[i], 0, 0)),
                      pl.BlockSpec((None, d, d_ff), lambda i, be, bv, st: (be[i], 0, 0)),
                      pl.BlockSpec((None, d_ff, d), lambda i, be, bv, st: (be[i], 0, 0))],
            out_specs=pl.BlockSpec((MOE_BLOCK, d), lambda i, be, bv, st: (i, 0)),
            scratch_shapes=[pltpu.VMEM((2, MOE_BLOCK, d), F32), pltpu.SemaphoreType.DMA((2,))]),
        out_shape=jax.ShapeDtypeStruct((n_slots, d), F32),
        compiler_params=_params("arbitrary"),
        name="expert_swiglu",
    )(blk_expert, blk_valid, slot_tok, h, norm_g.reshape(1, d), w_gate.astype(BF16), w_up.astype(BF16),
      w_down.astype(BF16))


def _combine_kernel(d1_ref, d2_ref, h_ref, meta_ref, ys_ref, o_ref, a_ref, b_ref, sem, *, tm):
    base = pl.program_id(0) * tm

    def issue(r, _):
        tok = base + r
        _row_copy(ys_ref, d1_ref[tok], a_ref, r, sem).start(priority=0)
        _row_copy(ys_ref, d2_ref[tok], b_ref, r, sem).start(priority=1)
        return 0

    lax.fori_loop(0, tm, issue, 0, unroll=8)

    pltpu.make_async_copy(ys_ref.at[pl.ds(0, tm)], a_ref, sem).wait()
    pltpu.make_async_copy(ys_ref.at[pl.ds(0, tm)], b_ref, sem).wait()
    meta = meta_ref[...]
    g1 = meta[:, META_G1:META_G1 + 1]
    g2 = meta[:, META_G2:META_G2 + 1]
    o_ref[...] = h_ref[...] + (g1 * a_ref[...] + g2 * b_ref[...])


def _combine(h, meta, ys, dest1, dest2, tm):
    t, d = h.shape
    row = lambda i, d1, d2: (i, 0)
    return pl.pallas_call(
        functools.partial(_combine_kernel, tm=tm),
        grid_spec=pltpu.PrefetchScalarGridSpec(
            num_scalar_prefetch=2, grid=(t // tm,),
            in_specs=[pl.BlockSpec((tm, d), row), pl.BlockSpec((tm, N_EXPERTS), row),
                      pl.BlockSpec(memory_space=pl.ANY)],
            out_specs=pl.BlockSpec((tm, d), row),
            scratch_shapes=[pltpu.VMEM((tm, d), F32), pltpu.VMEM((tm, d), F32), pltpu.SemaphoreType.DMA(())]),
        out_shape=jax.ShapeDtypeStruct((t, d), F32),
        compiler_params=_params("arbitrary"),
        name="moe_combine",
    )(dest1, dest2, h, meta, ys)


def _moe_layer(h, norm_g, w_router, w_gate, w_up, w_down, tm, f_chunk):
    t, d = h.shape
    meta, counts = _router(h, norm_g, w_router, tm)
    counts = counts.reshape(N_EXPERTS).astype(jnp.int32)
    padded = (counts + MOE_BLOCK - 1) // MOE_BLOCK * MOE_BLOCK
    pad_end = jnp.cumsum(padded)
    pad_start = pad_end - padded
    e1 = meta[:, META_E1].astype(jnp.int32)
    e2 = meta[:, META_E2].astype(jnp.int32)
    dest1 = pad_start[e1] + meta[:, META_R1].astype(jnp.int32)
    dest2 = pad_start[e2] + meta[:, META_R2].astype(jnp.int32)
    n_slots = 2 * t + N_EXPERTS * MOE_BLOCK
    blk_start = jnp.arange(n_slots // MOE_BLOCK, dtype=jnp.int32) * MOE_BLOCK
    blk_expert = jnp.minimum(jnp.sum(blk_start[:, None] >= pad_end[None, :], axis=1), N_EXPERTS - 1).astype(jnp.int32)
    blk_valid = (blk_start < pad_end[-1]).astype(jnp.int32)
    slot_tok = _slot_map(dest1, dest2, n_slots)
    ys = _expert_ffn(h, norm_g, slot_tok, blk_expert, blk_valid, w_gate, w_up, w_down, f_chunk)
    return _combine(h, meta, ys, dest1, dest2, tm)


def kernel(x, att_norm, att_w_in, att_q_latent_norm, att_w_q_up, att_kv_latent_norm, att_w_kv_up, att_q_norm, att_k_norm, att_w_out, dffn_norm, dffn_w_gate, dffn_w_up, dffn_w_down, ssm_norm, ssm_w_in, ssm_a_re, ssm_a_im, ssm_log_dt, ssm_b_re, ssm_b_im, ssm_c_re, ssm_c_im, ssm_d, ssm_w_glu, moe_norm, moe_router, moe_w_gate, moe_w_up, moe_w_down):
    b, s, d = x.shape
    x2 = x.reshape(b * s, d)
    h = _layer0(x2, b, s, att_norm[0], att_w_in[0], att_q_latent_norm[0], att_w_q_up[0], att_kv_latent_norm[0],
                att_w_kv_up[0], att_q_norm[0], att_k_norm[0], att_w_out[0], dffn_norm[0], dffn_w_gate[0],
                dffn_w_up[0], dffn_w_down[0], tm=512, tq=512, tk=256, f_chunk=512)
    h = _s5_layer(h, b, s, ssm_norm[0], ssm_w_in[0], ssm_a_re[0], ssm_a_im[0], ssm_log_dt[0], ssm_b_re[0],
                  ssm_b_im[0], ssm_c_re[0], ssm_c_im[0], ssm_d[0], ssm_w_glu[0])
    h = _moe_layer(h, moe_norm[0], moe_router[0], moe_w_gate[0], moe_w_up[0], moe_w_down[0], tm=256, f_chunk=512)
    return h.reshape(b, s, d)
```

```python
import functools
import math

import jax
import jax.numpy as jnp
from jax import lax
from jax.experimental import pallas as pl
from jax.experimental.pallas import tpu as pltpu

F32 = jnp.float32
BF16 = jnp.bfloat16

EPS = 1e-6
LANES = 128
SB_HEADS = 8
SB_HEAD_DIM = 64
SB_WIDTH = SB_HEADS * SB_HEAD_DIM
MLA_HEADS = 8
MLA_Q_RANK = 256
MLA_KV_RANK = 128
MLA_NOPE_DIM = 64
MLA_ROPE_DIM = 32
MLA_QK_DIM = MLA_NOPE_DIM + MLA_ROPE_DIM
MLA_V_DIM = 64
MLA_WIDTH = MLA_HEADS * MLA_V_DIM
ROPE_THETA = 10000.0
SSM_GROUP = 16
SSM_GROUPS = 32
SSM_STATE = 64
SSM_WIDTH = SSM_GROUP * SSM_GROUPS
SSM_NSTATE = SSM_GROUPS * SSM_STATE
EIG_RE_MAX = -1e-4
N_EXPERTS = 8
VMEM_LIMIT = 56 * 1024 * 1024

TOKEN_TILE = 512
FFN_CHUNK = 512
SB_Q_TILE = 512
SB_K_TILE = 256
MLA_Q_TILE = 1024
ROUTER_TILE = 256


def _params(*sem):
    return pltpu.CompilerParams(dimension_semantics=sem, vmem_limit_bytes=VMEM_LIMIT)


def _rms(x, n):
    return x * lax.rsqrt(jnp.sum(x * x, axis=-1, keepdims=True) * (1.0 / n) + EPS)


def _dot(a, b):
    return jnp.dot(a, b, preferred_element_type=F32)


def _dot_nt(a, b):
    return lax.dot_general(a, b, (((1,), (1,)), ((), ())), preferred_element_type=F32)


def _const_spec(shape):
    return pl.BlockSpec(shape, lambda *_: (0,) * len(shape))


IN_SB = 3 * SB_WIDTH
IN_CQ = IN_SB
IN_CKV = IN_CQ + MLA_Q_RANK
IN_KR = IN_CKV + MLA_KV_RANK
IN_KRS = IN_KR + LANES
IN_COLS_PAD = IN_KRS + LANES
MLA_PAD = MLA_HEADS * LANES
MLA_V_ROWS = MLA_V_DIM + 16
MLA_VT_WIDTH = MLA_HEADS * MLA_V_ROWS


def _inproj_kernel(x_ref, g_ref, win_ref, qlg_ref, wq_ref, kvlg_ref, wkv_ref, ctab_ref, stab_ref,
                   qg_ref, qgs_ref, kg_ref, kgs_ref, qkv_ref, qm_ref, km_ref, vm_ref):
    x = x_ref[...]
    xn = _rms(x, x.shape[-1]) * g_ref[...]
    hh = _dot(xn.astype(BF16), win_ref[...])
    qkv_ref[:, :SB_WIDTH] = (hh[:, :SB_WIDTH] * (SB_HEAD_DIM ** -0.5)).astype(BF16)
    qkv_ref[:, SB_WIDTH:] = hh[:, SB_WIDTH:IN_SB].astype(BF16)
    cq = _rms(hh[:, IN_CQ:IN_CKV], MLA_Q_RANK) * qlg_ref[...]
    qf = _dot(cq.astype(BF16), wq_ref[...])
    ckv = _rms(hh[:, IN_CKV:IN_KR], MLA_KV_RANK) * kvlg_ref[...]
    kvf = _dot(ckv.astype(BF16), wkv_ref[...])
    v_col = lax.broadcasted_iota(jnp.int32, (1, MLA_VT_WIDTH), 1)
    v_ones = jnp.where(v_col % MLA_V_ROWS >= MLA_V_DIM, 1.0, 0.0)
    vm_ref[...] = (kvf[:, MLA_PAD:] + v_ones).T.astype(BF16)
    kr = hh[:, IN_KR:IN_KRS]
    krs = hh[:, IN_KRS:IN_COLS_PAD]
    ctab = ctab_ref[...]
    stab = stab_ref[...]
    scale = MLA_QK_DIM ** -0.5 * math.log2(math.e)
    cq_t = ctab * (qg_ref[...] * scale)
    sq_t = stab * (qgs_ref[...] * scale)
    ck_t = ctab * kg_ref[...]
    sk_t = stab * kgs_ref[...]
    for h in range(MLA_HEADS):
        lo, hi = h * LANES, (h + 1) * LANES
        qh = qf[:, lo:hi]
        qs = qf[:, MLA_PAD + lo:MLA_PAD + hi]
        q_inv = lax.rsqrt(jnp.sum(qh * qh, axis=-1, keepdims=True) * (1.0 / MLA_QK_DIM) + EPS)
        qm_ref[:, lo:hi] = ((qh * cq_t + qs * sq_t) * q_inv).astype(BF16)
        kh = kvf[:, lo:hi] + kr
        k_inv = lax.rsqrt(jnp.sum(kh * kh, axis=-1, keepdims=True) * (1.0 / MLA_QK_DIM) + EPS)
        km_ref[:, lo:hi] = ((kh * ck_t + krs * sk_t) * k_inv).astype(BF16)


def _pad_heads(w, n_heads, width):
    k = w.shape[0]
    w = w.reshape(k, n_heads, width)
    return jnp.pad(w, ((0, 0), (0, 0), (0, LANES - width))).reshape(k, n_heads * LANES)


def _swap_rope(w):
    half = MLA_ROPE_DIM // 2
    return jnp.concatenate([jnp.zeros_like(w[..., :MLA_NOPE_DIM]),
                            w[..., MLA_NOPE_DIM + half:], w[..., MLA_NOPE_DIM:MLA_NOPE_DIM + half]], axis=-1)


def _rope_lane_tables(s_len):
    inv_freq = ROPE_THETA ** (-jnp.arange(0, MLA_ROPE_DIM, 2, dtype=F32) / MLA_ROPE_DIM)
    ang = jnp.arange(s_len, dtype=F32)[:, None] * inv_freq[None, :]
    cos, sin = jnp.cos(ang), jnp.sin(ang)
    ones = jnp.ones((s_len, MLA_NOPE_DIM), F32)
    zeros_n = jnp.zeros((s_len, MLA_NOPE_DIM), F32)
    zeros_t = jnp.zeros((s_len, LANES - MLA_QK_DIM), F32)
    ctab = jnp.concatenate([ones, cos, cos, zeros_t], axis=-1)
    stab = jnp.concatenate([zeros_n, -sin, sin, zeros_t], axis=-1)
    return ctab, stab


def _in_projection(x2, s_len, norm_g, w_in, q_lat_g, w_q_up, kv_lat_g, w_kv_up, q_g, k_g, tm):
    t, d = x2.shape
    off_kr = 3 * SB_WIDTH + MLA_Q_RANK + MLA_KV_RANK
    w_rope = w_in[:, off_kr:]
    zeros_n = jnp.zeros((d, MLA_NOPE_DIM), F32)
    zeros_t = jnp.zeros((d, LANES - MLA_QK_DIM), F32)
    half = MLA_ROPE_DIM // 2
    w_in_pad = jnp.concatenate(
        [w_in[:, :off_kr], zeros_n, w_rope, zeros_t,
         zeros_n, w_rope[:, half:], w_rope[:, :half], zeros_t], axis=-1).astype(BF16)
    wq3 = w_q_up.reshape(MLA_Q_RANK, MLA_HEADS, MLA_QK_DIM)
    wq_pad = jnp.concatenate(
        [_pad_heads(w_q_up, MLA_HEADS, MLA_QK_DIM),
         _pad_heads(_swap_rope(wq3).reshape(MLA_Q_RANK, -1), MLA_HEADS, MLA_QK_DIM)], axis=-1).astype(BF16)
    wkv3 = w_kv_up.reshape(MLA_KV_RANK, MLA_HEADS, MLA_NOPE_DIM + MLA_V_DIM)
    wkv_pad = jnp.concatenate(
        [_pad_heads(wkv3[..., :MLA_NOPE_DIM].reshape(MLA_KV_RANK, -1), MLA_HEADS, MLA_NOPE_DIM),
         jnp.pad(wkv3[..., MLA_NOPE_DIM:], ((0, 0), (0, 0), (0, MLA_V_ROWS - MLA_V_DIM))).reshape(MLA_KV_RANK, -1)],
        axis=-1).astype(BF16)
    ctab, stab = _rope_lane_tables(s_len)
    pad_g = lambda g: jnp.pad(g, (0, LANES - MLA_QK_DIM)).reshape(1, LANES)
    swap_g = lambda g: jnp.pad(jnp.concatenate(
        [jnp.zeros((MLA_NOPE_DIM,), F32), g[MLA_NOPE_DIM + half:], g[MLA_NOPE_DIM:MLA_NOPE_DIM + half]]),
        (0, LANES - MLA_QK_DIM)).reshape(1, LANES)
    n_pos = s_len // tm
    row = lambda i: (i, 0)
    return pl.pallas_call(
        _inproj_kernel,
        grid=(t // tm,),
        in_specs=[
            pl.BlockSpec((tm, d), row),
            _const_spec((1, d)),
            _const_spec((d, IN_COLS_PAD)),
            _const_spec((1, MLA_Q_RANK)),
            _const_spec((MLA_Q_RANK, 2 * MLA_PAD)),
            _const_spec((1, MLA_KV_RANK)),
            _const_spec((MLA_KV_RANK, MLA_PAD + MLA_VT_WIDTH)),
            pl.BlockSpec((tm, LANES), lambda i: (i % n_pos, 0)),
            pl.BlockSpec((tm, LANES), lambda i: (i % n_pos, 0)),
            _const_spec((1, LANES)), _const_spec((1, LANES)),
            _const_spec((1, LANES)), _const_spec((1, LANES)),
        ],
        out_specs=[
            pl.BlockSpec((tm, IN_SB), row),
            pl.BlockSpec((tm, MLA_PAD), row),
            pl.BlockSpec((tm, MLA_PAD), row),
            pl.BlockSpec((MLA_VT_WIDTH, tm), lambda i: (0, i)),
        ],
        out_shape=[
            jax.ShapeDtypeStruct((t, IN_SB), BF16),
            jax.ShapeDtypeStruct((t, MLA_PAD), BF16),
            jax.ShapeDtypeStruct((t, MLA_PAD), BF16),
            jax.ShapeDtypeStruct((MLA_VT_WIDTH, t), BF16),
        ],
        compiler_params=_params("arbitrary"),
        name="in_projection",
    )(x2, norm_g.reshape(1, d), w_in_pad, q_lat_g.reshape(1, -1), wq_pad, kv_lat_g.reshape(1, -1), wkv_pad,
      ctab, stab, pad_g(q_g), swap_g(q_g), pad_g(k_g), swap_g(k_g))


SB_EXP_UNDERFLOW = 110.0


def _sb_kernel(q_ref, k_ref, v_ref, u_ref, o_ref, acc_ref, carry_ref, z_ref, *, tq, tk):
    qi = pl.program_id(2)
    lane = lax.broadcasted_iota(jnp.int32, (1, LANES), 1)
    q = q_ref[...]
    q_heads = [jnp.where((lane >= j * SB_HEAD_DIM) & (lane < (j + 1) * SB_HEAD_DIM), q, jnp.zeros_like(q))
               for j in range(2)]
    acc_ref[...] = jnp.zeros_like(acc_ref)
    carry_ref[...] = jnp.zeros_like(carry_ref)

    def logits(j, c, lo_row=0, hi_row=tq):
        start = pl.multiple_of(c * tk, tk)
        z_ref[j, lo_row:hi_row, :] = _dot_nt(q_heads[j][lo_row:hi_row], k_ref[pl.ds(start, tk), :])

    def absorb(j, c, diagonal, lo_row=0, hi_row=tq):
        start = pl.multiple_of(c * tk, tk)
        z = z_ref[j, lo_row:hi_row, :]
        log_fail = -(jnp.maximum(z, 0.0) + jnp.log(1.0 + jnp.exp(-jnp.abs(z))))
        if diagonal:
            key = start + lax.broadcasted_iota(jnp.int32, z.shape, 1)
            qry = qi * tq + lo_row + lax.broadcasted_iota(jnp.int32, z.shape, 0)
            earlier = key < qry
            log_fail = jnp.where(earlier, log_fail, 0.0)
        hi = log_fail.astype(BF16)
        lo = (log_fail - hi.astype(F32)).astype(BF16)
        log_stick = _dot(hi, u_ref[...]) + _dot(lo, u_ref[...])
        carry = carry_ref[j, lo_row:hi_row, :]
        w = jnp.exp(z + log_fail + log_stick + jnp.tile(carry, (1, tk // LANES)))
        if diagonal:
            w = jnp.where(earlier, w, 0.0)
        acc_ref[j, lo_row:hi_row, :] += _dot(w.astype(BF16), v_ref[pl.ds(start, tk), :])
        carry_ref[j, lo_row:hi_row, :] = carry + (log_stick[:, :1] + log_fail[:, :1])

    def sweep(c, diagonal, lo_row=0, next_lo_row=0, hi_row=tq):
        logits(1, c, lo_row, hi_row)
        absorb(0, c, diagonal, lo_row, hi_row)
        logits(0, jnp.maximum(c - 1, 0), next_lo_row)
        absorb(1, c, diagonal, lo_row, hi_row)

    n_sub = tq // tk
    first = qi * n_sub + n_sub - 1
    logits(0, first, (n_sub - 1) * tk)
    for sub in reversed(range(n_sub)):
        sweep(qi * n_sub + sub, True, sub * tk, max(sub - 1, 0) * tk)

    def sticks_alive(lo_row, hi_row):
        return jnp.max(carry_ref[:, lo_row:hi_row, :]) > -SB_EXP_UNDERFLOW

    def earlier_chunk(state):
        c, _, rest_alive = state

        @pl.when(rest_alive)
        def _():
            sweep(c, False)

        @pl.when(jnp.logical_not(rest_alive))
        def _():
            sweep(c, False, hi_row=tk)

        return c - 1, sticks_alive(0, tk), sticks_alive(tk, tq)

    lax.while_loop(lambda state: (state[0] >= 0) & (state[1] | state[2]), earlier_chunk,
                   (qi * n_sub - 1, sticks_alive(0, tk), sticks_alive(tk, tq)))
    o_ref[...] = jnp.where(lane < SB_HEAD_DIM, acc_ref[0], acc_ref[1]).astype(o_ref.dtype)


def _sb_attention(qkv, tq, tk):
    b, s, _ = qkv.shape
    assert tq % tk == 0 and tq > tk and s % tq == 0
    n_pairs = SB_WIDTH // LANES
    tri = (jnp.arange(tk)[:, None] > jnp.arange(tk)[None, :]).astype(BF16)
    return pl.pallas_call(
        functools.partial(_sb_kernel, tq=tq, tk=tk),
        grid=(b, n_pairs, s // tq),
        in_specs=[
            pl.BlockSpec((None, tq, LANES), lambda bi, hp, qi: (bi, qi, hp)),
            pl.BlockSpec((None, s, LANES), lambda bi, hp, qi: (bi, 0, n_pairs + hp)),
            pl.BlockSpec((None, s, LANES), lambda bi, hp, qi: (bi, 0, 2 * n_pairs + hp)),
            _const_spec((tk, tk)),
        ],
        out_specs=pl.BlockSpec((None, tq, LANES), lambda bi, hp, qi: (bi, qi, hp)),
        out_shape=jax.ShapeDtypeStruct((b, s, SB_WIDTH), BF16),
        scratch_shapes=[pltpu.VMEM((2, tq, LANES), F32), pltpu.VMEM((2, tq, LANES), F32),
                        pltpu.VMEM((2, tq, tk), F32)],
        compiler_params=_params("arbitrary", "arbitrary", "arbitrary"),
        name="stick_breaking_attention",
    )(qkv, qkv, qkv, tri)


def _mla_kernel(q_ref, k_ref, vt_ref, o_ref, acc_ref, m_ref, s_ref, *, tq):
    qi = pl.program_id(2)
    acc_ref[...] = jnp.zeros_like(acc_ref)
    m_ref[...] = jnp.full(m_ref.shape, -jnp.inf, F32)

    def scores(j, c):
        start = pl.multiple_of(c * tq, tq)
        s_ref[j] = _dot_nt(k_ref[pl.ds(start, tq), j * LANES:(j + 1) * LANES],
                           q_ref[:, j * LANES:(j + 1) * LANES])

    def update(j, c, diagonal, k0=0, k1=tq, q0=0):
        start = pl.multiple_of(c * tq + k0, LANES)
        s = s_ref[j, k0:k1, q0:]
        if diagonal:
            key = k0 + lax.broadcasted_iota(jnp.int32, s.shape, 0)
            qry = q0 + lax.broadcasted_iota(jnp.int32, s.shape, 1)
            s = jnp.where(key <= qry, s, -jnp.inf)
        m_prev = m_ref[j, :, q0:]
        m_new = jnp.maximum(m_prev, jnp.max(s, axis=0, keepdims=True))
        p = jnp.exp2(s - m_new)
        vt = vt_ref[j * MLA_V_ROWS:(j + 1) * MLA_V_ROWS, pl.ds(start, k1 - k0)]
        acc_ref[j, :, q0:] = jnp.exp2(m_prev - m_new) * acc_ref[j, :, q0:] + _dot(vt, p.astype(BF16))
        m_ref[j, :, q0:] = m_new

    def update_diagonal(j):
        half = tq // 2
        update(j, qi, True, 0, half, 0)
        update(j, qi, True, half, tq, half)


    scores(0, 0)

    def earlier_block(c, _):
        scores(1, c)
        update(0, c, False)
        scores(0, c + 1)
        update(1, c, False)
        return 0

    lax.fori_loop(0, qi, earlier_block, 0)
    scores(1, qi)
    update_diagonal(0)
    update_diagonal(1)
    out_t = jnp.concatenate([acc_ref[j, :MLA_V_DIM, :] / acc_ref[j, MLA_V_DIM:MLA_V_DIM + 1, :] for j in range(2)],
                            axis=0)
    o_ref[...] = out_t.T.astype(o_ref.dtype)


def _mla_attention(q, k, vt, tq):
    b, s, _ = q.shape
    n_pairs = MLA_WIDTH // LANES
    return pl.pallas_call(
        functools.partial(_mla_kernel, tq=tq),
        grid=(b, n_pairs, s // tq),
        in_specs=[
            pl.BlockSpec((None, tq, 2 * LANES), lambda bi, hp, qi: (bi, qi, hp)),
            pl.BlockSpec((None, s, 2 * LANES), lambda bi, hp, qi: (bi, 0, hp)),
            pl.BlockSpec((2 * MLA_V_ROWS, s), lambda bi, hp, qi: (hp, bi)),
        ],
        out_specs=pl.BlockSpec((None, tq, LANES), lambda bi, hp, qi: (bi, qi, hp)),
        out_shape=jax.ShapeDtypeStruct((b, s, MLA_WIDTH), BF16),
        scratch_shapes=[pltpu.VMEM((2, MLA_V_ROWS, tq), F32), pltpu.VMEM((2, 1, tq), F32),
                        pltpu.VMEM((2, tq, tq), F32)],
        compiler_params=_params("arbitrary", "arbitrary", "arbitrary"),
        name="latent_attention",
    )(q, k, vt)


def _swiglu(x, wg_ref, wu_ref, wd_ref, f_chunk):
    d_ff = wg_ref.shape[-1]
    acc = None
    for c in range(d_ff // f_chunk):
        cols = slice(c * f_chunk, (c + 1) * f_chunk)
        gate = _dot(x, wg_ref[:, cols])
        up = _dot(x, wu_ref[:, cols])
        act = (gate * jax.nn.sigmoid(gate) * up).astype(BF16)
        part = _dot(act, wd_ref[cols, :])
        acc = part if acc is None else acc + part
    return acc


def _outproj_ffn_kernel(x_ref, osb_ref, omla_ref, wsb_ref, wmla_ref, g_ref, wg_ref, wu_ref, wd_ref, o_ref, *, f_chunk):
    h = x_ref[...] + _dot(osb_ref[...], wsb_ref[...]) + _dot(omla_ref[...], wmla_ref[...])
    hn = (_rms(h, h.shape[-1]) * g_ref[...]).astype(BF16)
    o_ref[...] = h + _swiglu(hn, wg_ref, wu_ref, wd_ref, f_chunk)


def _outproj_dense_ffn(x2, o_sb, o_mla, w_out, norm_g, w_gate, w_up, w_down, tm, f_chunk):
    t, d = x2.shape
    d_ff = w_gate.shape[-1]
    row = lambda i: (i, 0)
    resident = lambda shape: pl.BlockSpec(shape, lambda i: (0, 0), pipeline_mode=pl.Buffered(1))
    return pl.pallas_call(
        functools.partial(_outproj_ffn_kernel, f_chunk=f_chunk),
        grid=(t // tm,),
        in_specs=[pl.BlockSpec((tm, d), row), pl.BlockSpec((tm, SB_WIDTH), row), pl.BlockSpec((tm, MLA_WIDTH), row),
                  resident((SB_WIDTH, d)), resident((MLA_WIDTH, d)), resident((1, d)),
                  resident((d, d_ff)), resident((d, d_ff)), resident((d_ff, d))],
        out_specs=pl.BlockSpec((tm, d), row),
        out_shape=jax.ShapeDtypeStruct((t, d), F32),
        compiler_params=_params("arbitrary"),
        name="out_projection_dense_swiglu",
    )(x2, o_sb, o_mla, w_out[:SB_WIDTH].astype(BF16), w_out[SB_WIDTH:].astype(BF16), norm_g.reshape(1, d),
      w_gate.astype(BF16), w_up.astype(BF16), w_down.astype(BF16))


def _layer0(x2, b, s, att_norm, att_w_in, q_lat_g, w_q_up, kv_lat_g, w_kv_up, q_g, k_g, w_out,
            dffn_norm, w_gate, w_up, w_down):
    qkv, qm, km, vm = _in_projection(x2, s, att_norm, att_w_in, q_lat_g, w_q_up, kv_lat_g, w_kv_up, q_g, k_g,
                                     TOKEN_TILE)
    o_sb = _sb_attention(qkv.reshape(b, s, -1), SB_Q_TILE, SB_K_TILE).reshape(b * s, -1)
    o_mla = _mla_attention(qm.reshape(b, s, -1), km.reshape(b, s, -1), vm, MLA_Q_TILE).reshape(b * s, -1)
    return _outproj_dense_ffn(x2, o_sb, o_mla, w_out, dffn_norm, w_gate, w_up, w_down, TOKEN_TILE, FFN_CHUNK)


SSM_SEGS = 8
SSM_SEG_LEN = 128
SSM_TILE = SSM_SEGS * SSM_SEG_LEN
SSM_CH_CHUNK = LANES
SSM_ST_CHUNK = SSM_CH_CHUNK // SSM_GROUP * SSM_STATE
SSM_SCAN_TILES = 8


def _s5_prep_kernel(are_ref, aim_ref, logdt_ref, bre_ref, bim_ref, cre_ref, cim_ref,
                    wbu_ref, wc_ref, pow_ref):
    n = SSM_NSTATE
    a_re = jnp.minimum(are_ref[...], EIG_RE_MAX)
    a_im = aim_ref[...]
    dt = jnp.exp(logdt_ref[...])
    mag = jnp.exp(a_re * dt)
    lam_re = mag * jnp.cos(a_im * dt)
    lam_im = mag * jnp.sin(a_im * dt)
    den = a_re * a_re + a_im * a_im
    num_re = lam_re - 1.0
    coef_re = (num_re * a_re + lam_im * a_im) / den
    coef_im = (lam_im * a_re - num_re * a_im) / den
    b_re = bre_ref[...]
    b_im = bim_ref[...]
    in_group = (lax.broadcasted_iota(jnp.int32, (SSM_CH_CHUNK, n), 0) // SSM_GROUP
                == lax.broadcasted_iota(jnp.int32, (SSM_CH_CHUNK, n), 1) % SSM_ST_CHUNK // SSM_STATE)
    wbu_ref[:, :n] = jnp.where(in_group, coef_re * b_re - coef_im * b_im, 0.0).astype(BF16)
    wbu_ref[:, n:] = jnp.where(in_group, coef_re * b_im + coef_im * b_re, 0.0).astype(BF16)
    in_group_t = (lax.broadcasted_iota(jnp.int32, (n, SSM_CH_CHUNK), 0) % SSM_ST_CHUNK // SSM_STATE
                  == lax.broadcasted_iota(jnp.int32, (n, SSM_CH_CHUNK), 1) // SSM_GROUP)
    wc_ref[:n, :] = jnp.where(in_group_t, cre_ref[...], 0.0).astype(BF16)
    wc_ref[n:, :] = jnp.where(in_group_t, -cim_ref[...], 0.0).astype(BF16)
    steps = jnp.where(lax.broadcasted_iota(jnp.int32, (SSM_SEGS, n), 0) == 0, 1.0, float(SSM_SEG_LEN))
    mag_k = jnp.exp(steps * (a_re * dt))
    ang_k = steps * (a_im * dt)
    pow_ref[:, :n] = mag_k * jnp.cos(ang_k)
    pow_ref[:, n:] = mag_k * jnp.sin(ang_k)


def _s5_prepare(a_re, a_im, log_dt, b_re, b_im, c_re, c_im):
    n = SSM_NSTATE
    row = lambda a: a.reshape(1, n)
    reps = SSM_CH_CHUNK // SSM_GROUP
    b_t = lambda w: jnp.tile(w.transpose(2, 0, 1).reshape(SSM_GROUP, n), (reps, 1))
    c_t = lambda w: jnp.tile(w.transpose(0, 2, 1).reshape(n, SSM_GROUP), (1, reps))
    return pl.pallas_call(
        _s5_prep_kernel,
        out_shape=[jax.ShapeDtypeStruct((SSM_CH_CHUNK, 2 * n), BF16),
                   jax.ShapeDtypeStruct((2 * n, SSM_CH_CHUNK), BF16),
                   jax.ShapeDtypeStruct((SSM_SEGS, 2 * n), F32)],
        compiler_params=pltpu.CompilerParams(vmem_limit_bytes=VMEM_LIMIT),
        name="s5_discretize",
    )(row(a_re), row(a_im), row(jnp.repeat(log_dt, SSM_STATE)), b_t(b_re), b_t(b_im), c_t(c_re), c_t(c_im))


def _s5_kernel(h_ref, g_ref, win_ref, wbu_ref, pow_ref, wc_ref, d_ref, wglu_ref, o_ref,
               ut_ref, up_ref, st_ref, carry_ref, yp_ref, yt_ref):
    n = SSM_NSTATE
    nt = n // LANES
    n_ch = SSM_WIDTH // LANES
    per = SSM_ST_CHUNK // LANES
    d_model = h_ref.shape[-1]

    @pl.when(pl.program_id(1) == 0)
    def _():
        carry_ref[...] = jnp.zeros_like(carry_ref)

    x = h_ref[...]
    xn = _rms(x, d_model) * g_ref[...]
    u = _dot(xn.astype(BF16), win_ref[...])
    for k in range(n_ch):
        ut_ref[k] = u[:, k * LANES:(k + 1) * LANES]

    def to_step_major(i, _):
        dst = pl.ds(pl.multiple_of(i * SSM_SEGS, SSM_SEGS), SSM_SEGS)
        for k in range(n_ch):
            up_ref[k, dst, :] = ut_ref[k, pl.ds(i, SSM_SEGS, stride=SSM_SEG_LEN), :]
        return 0

    lax.fori_loop(0, SSM_SEG_LEN, to_step_major, 0)

    for c in range(n_ch):
        u16 = up_ref[c].astype(BF16)
        for part in range(2):
            cols = slice(part * n + c * SSM_ST_CHUNK, part * n + (c + 1) * SSM_ST_CHUNK)
            bu = _dot(u16, wbu_ref[:, cols])
            for k in range(per):
                st_ref[part * nt + c * per + k] = bu[:, k * LANES:(k + 1) * LANES]

    for grp in range(nt // SSM_SCAN_TILES):
        tiles = range(grp * SSM_SCAN_TILES, (grp + 1) * SSM_SCAN_TILES)
        re_l = [slice(k * LANES, (k + 1) * LANES) for k in tiles]
        im_l = [slice(n + k * LANES, n + (k + 1) * LANES) for k in tiles]
        lam_re = [jnp.broadcast_to(pow_ref[0:1, c], (SSM_SEGS, LANES)) for c in re_l]
        lam_im = [jnp.broadcast_to(pow_ref[0:1, c], (SSM_SEGS, LANES)) for c in im_l]

        def scan(state, store):
            def step(i, state):
                rows = pl.ds(pl.multiple_of(i * SSM_SEGS, SSM_SEGS), SSM_SEGS)
                new_state = []
                for idx, k in enumerate(tiles):
                    s_re, s_im = state[idx]
                    n_re = lam_re[idx] * s_re - lam_im[idx] * s_im + st_ref[k, rows, :]
                    n_im = lam_re[idx] * s_im + lam_im[idx] * s_re + st_ref[nt + k, rows, :]
                    if store:
                        st_ref[k, rows, :] = n_re
                        st_ref[nt + k, rows, :] = n_im
                    new_state.append((n_re, n_im))
                return tuple(new_state)

            return lax.fori_loop(0, SSM_SEG_LEN, step, state)

        zero = jnp.zeros((SSM_SEGS, LANES), F32)
        ends = scan(tuple((zero, zero) for _ in tiles), False)
        starts = []
        for idx in range(len(tiles)):
            e_re, e_im = ends[idx]
            p_re = pow_ref[1:2, re_l[idx]]
            p_im = pow_ref[1:2, im_l[idx]]
            c_re = carry_ref[:, re_l[idx]]
            c_im = carry_ref[:, im_l[idx]]
            rows_re, rows_im = [], []
            for j in range(SSM_SEGS):
                rows_re.append(c_re)
                rows_im.append(c_im)
                c_re, c_im = (e_re[j:j + 1] + (p_re * c_re - p_im * c_im),
                              e_im[j:j + 1] + (p_re * c_im + p_im * c_re))
            carry_ref[:, re_l[idx]] = c_re
            carry_ref[:, im_l[idx]] = c_im
            starts.append((jnp.concatenate(rows_re, axis=0), jnp.concatenate(rows_im, axis=0)))
        scan(tuple(starts), True)

    for c in range(n_ch):
        h_re = jnp.concatenate([st_ref[c * per + k] for k in range(per)], axis=-1).astype(BF16)
        h_im = jnp.concatenate([st_ref[nt + c * per + k] for k in range(per)], axis=-1).astype(BF16)
        rows_re = slice(c * SSM_ST_CHUNK, (c + 1) * SSM_ST_CHUNK)
        rows_im = slice(n + c * SSM_ST_CHUNK, n + (c + 1) * SSM_ST_CHUNK)
        yp_ref[c] = _dot(h_re, wc_ref[rows_re, :]) + _dot(h_im, wc_ref[rows_im, :])

    def to_token_major(i, _):
        src = pl.ds(pl.multiple_of(i * SSM_SEGS, SSM_SEGS), SSM_SEGS)
        for k in range(n_ch):
            yt_ref[k, pl.ds(i, SSM_SEGS, stride=SSM_SEG_LEN), :] = yp_ref[k, src, :]
        return 0

    lax.fori_loop(0, SSM_SEG_LEN, to_token_major, 0)
    y = jnp.concatenate([yt_ref[k] for k in range(n_ch)], axis=-1)
    y = jax.nn.gelu(y + d_ref[...] * u)
    z = _dot(y.astype(BF16), wglu_ref[...])
    o_ref[...] = x + z[:, :d_model] * jax.nn.sigmoid(z[:, d_model:])


def _s5_layer(h, b, s, norm_g, w_in, a_re, a_im, log_dt, b_re, b_im, c_re, c_im, d_skip, w_glu):
    t, d = h.shape
    n = SSM_NSTATE
    w_bu, w_c, powers = _s5_prepare(a_re, a_im, log_dt, b_re, b_im, c_re, c_im)
    tiles = s // SSM_TILE
    resident = lambda shape: pl.BlockSpec(shape, lambda bi, ti: (0, 0), pipeline_mode=pl.Buffered(1))
    row = lambda bi, ti: (bi * tiles + ti, 0)
    ch_tiles = pltpu.VMEM((SSM_WIDTH // LANES, SSM_TILE, LANES), F32)
    return pl.pallas_call(
        _s5_kernel,
        grid=(b, tiles),
        in_specs=[pl.BlockSpec((SSM_TILE, d), row), resident((1, d)), resident((d, SSM_WIDTH)),
                  resident((SSM_CH_CHUNK, 2 * n)), resident((SSM_SEGS, 2 * n)), resident((2 * n, SSM_CH_CHUNK)),
                  resident((1, SSM_WIDTH)), resident((SSM_WIDTH, 2 * d))],
        out_specs=pl.BlockSpec((SSM_TILE, d), row),
        out_shape=jax.ShapeDtypeStruct((t, d), F32),
        scratch_shapes=[ch_tiles, ch_tiles, pltpu.VMEM((2 * n // LANES, SSM_TILE, LANES), F32),
                        pltpu.VMEM((1, 2 * n), F32), ch_tiles, ch_tiles],
        compiler_params=_params("arbitrary", "arbitrary"),
        name="s5_mixer",
    )(h, norm_g.reshape(1, d), w_in.astype(BF16), w_bu, powers, w_c, d_skip.reshape(1, -1), w_glu.astype(BF16))


MOE_BLOCK = 256
META_E1, META_E2, META_G1, META_G2, META_R1, META_R2 = range(6)


def _router_kernel(h_ref, g_ref, wr_ref, tri_ref, meta_ref, counts_ref, run_ref):
    @pl.when(pl.program_id(0) == 0)
    def _():
        run_ref[...] = jnp.zeros_like(run_ref)

    x = h_ref[...]
    xn = _rms(x, x.shape[-1]) * g_ref[...]
    logits = jnp.dot(xn, wr_ref[...], preferred_element_type=F32, precision=lax.Precision.HIGHEST)
    tm = logits.shape[0]
    e = lax.broadcasted_iota(jnp.int32, (tm, N_EXPERTS), 1)
    v1 = jnp.max(logits, axis=-1, keepdims=True)
    i1 = jnp.min(jnp.where(logits == v1, e, N_EXPERTS), axis=-1, keepdims=True)
    rest = jnp.where(e == i1, -jnp.inf, logits)
    v2 = jnp.max(rest, axis=-1, keepdims=True)
    i2 = jnp.min(jnp.where(rest == v2, e, N_EXPERTS), axis=-1, keepdims=True)
    ex = jnp.exp(v2 - v1)
    g1 = 1.0 / (1.0 + ex)
    g2 = ex / (1.0 + ex)
    oh1 = (e == i1).astype(F32)
    oh2 = (e == i2).astype(F32)
    both = oh1 + oh2
    before = _dot(tri_ref[...], both.astype(BF16)) + run_ref[...]
    r1 = jnp.sum(oh1 * before, axis=-1, keepdims=True)
    r2 = jnp.sum(oh2 * before, axis=-1, keepdims=True)
    run_ref[...] += jnp.sum(both, axis=0, keepdims=True)
    counts_ref[...] = run_ref[...]
    meta = jnp.zeros((tm, N_EXPERTS), F32)
    for lane, val in ((META_E1, i1.astype(F32)), (META_E2, i2.astype(F32)), (META_G1, g1), (META_G2, g2),
                      (META_R1, r1), (META_R2, r2)):
        meta = jnp.where(e == lane, val, meta)
    meta_ref[...] = meta


def _router(h, norm_g, w_router, tm):
    t, d = h.shape
    row = lambda i: (i, 0)
    tri = (jnp.arange(tm)[:, None] > jnp.arange(tm)[None, :]).astype(BF16)
    return pl.pallas_call(
        _router_kernel,
        grid=(t // tm,),
        in_specs=[pl.BlockSpec((tm, d), row), _const_spec((1, d)), _const_spec((d, N_EXPERTS)),
                  _const_spec((tm, tm))],
        out_specs=[pl.BlockSpec((tm, N_EXPERTS), row), _const_spec((1, N_EXPERTS))],
        out_shape=[jax.ShapeDtypeStruct((t, N_EXPERTS), F32), jax.ShapeDtypeStruct((1, N_EXPERTS), F32)],
        scratch_shapes=[pltpu.VMEM((1, N_EXPERTS), F32)],
        compiler_params=_params("arbitrary"),
        name="moe_router",
    )(h, norm_g.reshape(1, d), w_router, tri)


def _slot_map_kernel(d1_ref, d2_ref, tok_ref):
    def clear(slot, _):
        tok_ref[slot] = 0
        return 0

    lax.fori_loop(0, tok_ref.shape[0], clear, 0, unroll=16)

    def place(tok, _):
        tok_ref[d1_ref[tok]] = tok
        tok_ref[d2_ref[tok]] = tok
        return 0

    lax.fori_loop(0, d1_ref.shape[0], place, 0, unroll=8)


def _slot_map(dest1, dest2, n_slots):
    smem = pl.BlockSpec(memory_space=pltpu.SMEM)
    return pl.pallas_call(
        _slot_map_kernel,
        in_specs=[smem, smem], out_specs=smem,
        out_shape=jax.ShapeDtypeStruct((n_slots,), jnp.int32),
        name="moe_slot_map",
    )(dest1, dest2)


def _row_copy(src_ref, src_row, dst_ref, dst_row, sem):
    return pltpu.make_async_copy(src_ref.at[pl.ds(src_row, 1)], dst_ref.at[pl.ds(dst_row, 1)], sem)


def _expert_ffn_kernel(be_ref, bv_ref, tok_ref, h_ref, g_ref, wg_ref, wu_ref, wd_ref, ys_ref, buf_ref, sem, *, f_chunk):
    del be_ref
    i = pl.program_id(0)
    last = pl.num_programs(0) - 1
    cur = i % 2

    def gather(block, buf):
        base = block * MOE_BLOCK
        for r in range(MOE_BLOCK):
            _row_copy(h_ref, tok_ref[base + r], buf_ref.at[buf], r, sem.at[buf]).start(priority=r % 2)

    def wait_gather(buf):
        pltpu.make_async_copy(h_ref.at[pl.ds(0, MOE_BLOCK)], buf_ref.at[buf], sem.at[buf]).wait()

    @pl.when(i == 0)
    def _():
        gather(0, 0)

    gather(jnp.minimum(i + 1, last), 1 - cur)
    wait_gather(cur)
    valid = bv_ref[i] != 0

    @pl.when(valid)
    def _():
        x = buf_ref[cur]
        xn = (_rms(x, x.shape[-1]) * g_ref[...]).astype(BF16)
        ys_ref[...] = _swiglu(xn, wg_ref, wu_ref, wd_ref, f_chunk)

    @pl.when(jnp.logical_not(valid))
    def _():
        ys_ref[...] = jnp.zeros_like(ys_ref)

    @pl.when(i == last)
    def _():
        wait_gather(1 - cur)


def _expert_ffn(h, norm_g, slot_tok, blk_expert, blk_valid, w_gate, w_up, w_down, f_chunk):
    n_slots = slot_tok.shape[0]
    d = h.shape[-1]
    d_ff = w_gate.shape[-1]
    return pl.pallas_call(
        functools.partial(_expert_ffn_kernel, f_chunk=f_chunk),
        grid_spec=pltpu.PrefetchScalarGridSpec(
            num_scalar_prefetch=3, grid=(n_slots // MOE_BLOCK,),
            in_specs=[pl.BlockSpec(memory_space=pl.ANY),
                      pl.BlockSpec((1, d), lambda i, be, bv, st: (0, 0)),
                      pl.BlockSpec((None, d, d_ff), lambda i, be, bv, st: (be[i], 0, 0)),
                      pl.BlockSpec((None, d, d_ff), lambda i, be, bv, st: (be[i], 0, 0)),
                      pl.BlockSpec((None, d_ff, d), lambda i, be, bv, st: (be[i], 0, 0))],
            out_specs=pl.BlockSpec((MOE_BLOCK, d), lambda i, be, bv, st: (i, 0)),
            scratch_shapes=[pltpu.VMEM((2, MOE_BLOCK, d), F32), pltpu.SemaphoreType.DMA((2,))]),
        out_shape=jax.ShapeDtypeStruct((n_slots, d), F32),
        compiler_params=_params("arbitrary"),
        name="expert_swiglu",
    )(blk_expert, blk_valid, slot_tok, h, norm_g.reshape(1, d), w_gate.astype(BF16), w_up.astype(BF16),
      w_down.astype(BF16))


def _combine_kernel(d1_ref, d2_ref, h_ref, meta_ref, ys_ref, o_ref, a_ref, b_ref, sem, *, tm):
    base = pl.program_id(0) * tm

    def issue(r, _):
        tok = base + r
        _row_copy(ys_ref, d1_ref[tok], a_ref, r, sem).start(priority=0)
        _row_copy(ys_ref, d2_ref[tok], b_ref, r, sem).start(priority=1)
        return 0

    lax.fori_loop(0, tm, issue, 0, unroll=8)

    pltpu.make_async_copy(ys_ref.at[pl.ds(0, tm)], a_ref, sem).wait()
    pltpu.make_async_copy(ys_ref.at[pl.ds(0, tm)], b_ref, sem).wait()
    meta = meta_ref[...]
    g1 = meta[:, META_G1:META_G1 + 1]
    g2 = meta[:, META_G2:META_G2 + 1]
    o_ref[...] = h_ref[...] + (g1 * a_ref[...] + g2 * b_ref[...])


def _combine(h, meta, ys, dest1, dest2, tm):
    t, d = h.shape
    row = lambda i, d1, d2: (i, 0)
    return pl.pallas_call(
        functools.partial(_combine_kernel, tm=tm),
        grid_spec=pltpu.PrefetchScalarGridSpec(
            num_scalar_prefetch=2, grid=(t // tm,),
            in_specs=[pl.BlockSpec((tm, d), row), pl.BlockSpec((tm, N_EXPERTS), row),
                      pl.BlockSpec(memory_space=pl.ANY)],
            out_specs=pl.BlockSpec((tm, d), row),
            scratch_shapes=[pltpu.VMEM((tm, d), F32), pltpu.VMEM((tm, d), F32), pltpu.SemaphoreType.DMA(())]),
        out_shape=jax.ShapeDtypeStruct((t, d), F32),
        compiler_params=_params("arbitrary"),
        name="moe_combine",
    )(dest1, dest2, h, meta, ys)


def _moe_layer(h, norm_g, w_router, w_gate, w_up, w_down):
    t, d = h.shape
    meta, counts = _router(h, norm_g, w_router, ROUTER_TILE)
    counts = counts.reshape(N_EXPERTS).astype(jnp.int32)
    padded = (counts + MOE_BLOCK - 1) // MOE_BLOCK * MOE_BLOCK
    pad_end = jnp.cumsum(padded)
    pad_start = pad_end - padded
    e1 = meta[:, META_E1].astype(jnp.int32)
    e2 = meta[:, META_E2].astype(jnp.int32)
    dest1 = pad_start[e1] + meta[:, META_R1].astype(jnp.int32)
    dest2 = pad_start[e2] + meta[:, META_R2].astype(jnp.int32)
    n_slots = 2 * t + N_EXPERTS * MOE_BLOCK
    blk_start = jnp.arange(n_slots // MOE_BLOCK, dtype=jnp.int32) * MOE_BLOCK
    blk_expert = jnp.minimum(jnp.sum(blk_start[:, None] >= pad_end[None, :], axis=1), N_EXPERTS - 1).astype(jnp.int32)
    blk_valid = (blk_start < pad_end[-1]).astype(jnp.int32)
    slot_tok = _slot_map(dest1, dest2, n_slots)
    ys = _expert_ffn(h, norm_g, slot_tok, blk_expert, blk_valid, w_gate, w_up, w_down, FFN_CHUNK)
    return _combine(h, meta, ys, dest1, dest2, ROUTER_TILE)


def kernel(x, att_norm, att_w_in, att_q_latent_norm, att_w_q_up, att_kv_latent_norm, att_w_kv_up, att_q_norm, att_k_norm, att_w_out, dffn_norm, dffn_w_gate, dffn_w_up, dffn_w_down, ssm_norm, ssm_w_in, ssm_a_re, ssm_a_im, ssm_log_dt, ssm_b_re, ssm_b_im, ssm_c_re, ssm_c_im, ssm_d, ssm_w_glu, moe_norm, moe_router, moe_w_gate, moe_w_up, moe_w_down):
    b, s, d = x.shape
    x2 = x.reshape(b * s, d)
    h = _layer0(x2, b, s, att_norm[0], att_w_in[0], att_q_latent_norm[0], att_w_q_up[0], att_kv_latent_norm[0],
                att_w_kv_up[0], att_q_norm[0], att_k_norm[0], att_w_out[0], dffn_norm[0], dffn_w_gate[0],
                dffn_w_up[0], dffn_w_down[0])
    h = _s5_layer(h, b, s, ssm_norm[0], ssm_w_in[0], ssm_a_re[0], ssm_a_im[0], ssm_log_dt[0], ssm_b_re[0],
                  ssm_b_im[0], ssm_c_re[0], ssm_c_im[0], ssm_d[0], ssm_w_glu[0])
    h = _moe_layer(h, moe_norm[0], moe_router[0], moe_w_gate[0], moe_w_up[0], moe_w_down[0])
    return h.reshape(b, s, d)
```

```python
import functools
import math

import jax
import jax.numpy as jnp
from jax import lax
from jax.experimental import pallas as pl
from jax.experimental.pallas import tpu as pltpu

F32 = jnp.float32
BF16 = jnp.bfloat16

EPS = 1e-6
LANES = 128
SB_HEADS = 8
SB_HEAD_DIM = 64
SB_WIDTH = SB_HEADS * SB_HEAD_DIM
MLA_HEADS = 8
MLA_Q_RANK = 256
MLA_KV_RANK = 128
MLA_NOPE_DIM = 64
MLA_ROPE_DIM = 32
MLA_QK_DIM = MLA_NOPE_DIM + MLA_ROPE_DIM
MLA_V_DIM = 64
MLA_WIDTH = MLA_HEADS * MLA_V_DIM
ROPE_THETA = 10000.0
SSM_GROUP = 16
SSM_GROUPS = 32
SSM_STATE = 64
SSM_WIDTH = SSM_GROUP * SSM_GROUPS
SSM_NSTATE = SSM_GROUPS * SSM_STATE
EIG_RE_MAX = -1e-4
N_EXPERTS = 8
VMEM_LIMIT = 56 * 1024 * 1024

TOKEN_TILE = 512
FFN_CHUNK = 1792
SB_Q_TILE = 512
SB_K_TILE = 256
MLA_Q_TILE = 1024
ROUTER_TILE = 256


def _params(*sem):
    return pltpu.CompilerParams(dimension_semantics=sem, vmem_limit_bytes=VMEM_LIMIT)


def _rms(x, n):
    return x * lax.rsqrt(jnp.sum(x * x, axis=-1, keepdims=True) * (1.0 / n) + EPS)


def _dot(a, b):
    return jnp.dot(a, b, preferred_element_type=F32)


def _dot_nt(a, b):
    return lax.dot_general(a, b, (((1,), (1,)), ((), ())), preferred_element_type=F32)


def _const_spec(shape):
    return pl.BlockSpec(shape, lambda *_: (0,) * len(shape))


IN_SB = 3 * SB_WIDTH
IN_CQ = IN_SB
IN_CKV = IN_CQ + MLA_Q_RANK
IN_KR = IN_CKV + MLA_KV_RANK
IN_KRS = IN_KR + LANES
IN_COLS_PAD = IN_KRS + LANES
MLA_PAD = MLA_HEADS * LANES
MLA_V_ROWS = MLA_V_DIM + 16
MLA_VT_WIDTH = MLA_HEADS * MLA_V_ROWS


def _inproj_kernel(x_ref, g_ref, win_ref, qlg_ref, wq_ref, kvlg_ref, wkv_ref, ctab_ref, stab_ref,
                   qg_ref, qgs_ref, kg_ref, kgs_ref, qkv_ref, qm_ref, km_ref, vm_ref):
    x = x_ref[...]
    xn = _rms(x, x.shape[-1]) * g_ref[...]
    hh = _dot(xn.astype(BF16), win_ref[...])
    qkv_ref[:, :SB_WIDTH] = (hh[:, :SB_WIDTH] * (SB_HEAD_DIM ** -0.5)).astype(BF16)
    qkv_ref[:, SB_WIDTH:] = hh[:, SB_WIDTH:IN_SB].astype(BF16)
    cq = _rms(hh[:, IN_CQ:IN_CKV], MLA_Q_RANK) * qlg_ref[...]
    qf = _dot(cq.astype(BF16), wq_ref[...])
    ckv = _rms(hh[:, IN_CKV:IN_KR], MLA_KV_RANK) * kvlg_ref[...]
    kvf = _dot(ckv.astype(BF16), wkv_ref[...])
    v_col = lax.broadcasted_iota(jnp.int32, (1, MLA_VT_WIDTH), 1)
    v_ones = jnp.where(v_col % MLA_V_ROWS >= MLA_V_DIM, 1.0, 0.0)
    vm_ref[...] = (kvf[:, MLA_PAD:] + v_ones).T.astype(BF16)
    kr = hh[:, IN_KR:IN_KRS]
    krs = hh[:, IN_KRS:IN_COLS_PAD]
    ctab = ctab_ref[...]
    stab = stab_ref[...]
    scale = MLA_QK_DIM ** -0.5 * math.log2(math.e)
    cq_t = ctab * (qg_ref[...] * scale)
    sq_t = stab * (qgs_ref[...] * scale)
    ck_t = ctab * kg_ref[...]
    sk_t = stab * kgs_ref[...]
    for h in range(MLA_HEADS):
        lo, hi = h * LANES, (h + 1) * LANES
        qh = qf[:, lo:hi]
        qs = qf[:, MLA_PAD + lo:MLA_PAD + hi]
        q_inv = lax.rsqrt(jnp.sum(qh * qh, axis=-1, keepdims=True) * (1.0 / MLA_QK_DIM) + EPS)
        qm_ref[:, lo:hi] = ((qh * cq_t + qs * sq_t) * q_inv).astype(BF16)
        kh = kvf[:, lo:hi] + kr
        k_inv = lax.rsqrt(jnp.sum(kh * kh, axis=-1, keepdims=True) * (1.0 / MLA_QK_DIM) + EPS)
        km_ref[:, lo:hi] = ((kh * ck_t + krs * sk_t) * k_inv).astype(BF16)


def _pad_heads(w, n_heads, width):
    k = w.shape[0]
    w = w.reshape(k, n_heads, width)
    return jnp.pad(w, ((0, 0), (0, 0), (0, LANES - width))).reshape(k, n_heads * LANES)


def _swap_rope(w):
    half = MLA_ROPE_DIM // 2
    return jnp.concatenate([jnp.zeros_like(w[..., :MLA_NOPE_DIM]),
                            w[..., MLA_NOPE_DIM + half:], w[..., MLA_NOPE_DIM:MLA_NOPE_DIM + half]], axis=-1)


def _rope_lane_tables(s_len):
    inv_freq = ROPE_THETA ** (-jnp.arange(0, MLA_ROPE_DIM, 2, dtype=F32) / MLA_ROPE_DIM)
    ang = jnp.arange(s_len, dtype=F32)[:, None] * inv_freq[None, :]
    cos, sin = jnp.cos(ang), jnp.sin(ang)
    ones = jnp.ones((s_len, MLA_NOPE_DIM), F32)
    zeros_n = jnp.zeros((s_len, MLA_NOPE_DIM), F32)
    zeros_t = jnp.zeros((s_len, LANES - MLA_QK_DIM), F32)
    ctab = jnp.concatenate([ones, cos, cos, zeros_t], axis=-1)
    stab = jnp.concatenate([zeros_n, -sin, sin, zeros_t], axis=-1)
    return ctab, stab


def _in_projection(x2, s_len, norm_g, w_in, q_lat_g, w_q_up, kv_lat_g, w_kv_up, q_g, k_g, tm):
    t, d = x2.shape
    off_kr = 3 * SB_WIDTH + MLA_Q_RANK + MLA_KV_RANK
    w_rope = w_in[:, off_kr:]
    zeros_n = jnp.zeros((d, MLA_NOPE_DIM), F32)
    zeros_t = jnp.zeros((d, LANES - MLA_QK_DIM), F32)
    half = MLA_ROPE_DIM // 2
    w_in_pad = jnp.concatenate(
        [w_in[:, :off_kr], zeros_n, w_rope, zeros_t,
         zeros_n, w_rope[:, half:], w_rope[:, :half], zeros_t], axis=-1).astype(BF16)
    wq3 = w_q_up.reshape(MLA_Q_RANK, MLA_HEADS, MLA_QK_DIM)
    wq_pad = jnp.concatenate(
        [_pad_heads(w_q_up, MLA_HEADS, MLA_QK_DIM),
         _pad_heads(_swap_rope(wq3).reshape(MLA_Q_RANK, -1), MLA_HEADS, MLA_QK_DIM)], axis=-1).astype(BF16)
    wkv3 = w_kv_up.reshape(MLA_KV_RANK, MLA_HEADS, MLA_NOPE_DIM + MLA_V_DIM)
    wkv_pad = jnp.concatenate(
        [_pad_heads(wkv3[..., :MLA_NOPE_DIM].reshape(MLA_KV_RANK, -1), MLA_HEADS, MLA_NOPE_DIM),
         jnp.pad(wkv3[..., MLA_NOPE_DIM:], ((0, 0), (0, 0), (0, MLA_V_ROWS - MLA_V_DIM))).reshape(MLA_KV_RANK, -1)],
        axis=-1).astype(BF16)
    ctab, stab = _rope_lane_tables(s_len)
    pad_g = lambda g: jnp.pad(g, (0, LANES - MLA_QK_DIM)).reshape(1, LANES)
    swap_g = lambda g: jnp.pad(jnp.concatenate(
        [jnp.zeros((MLA_NOPE_DIM,), F32), g[MLA_NOPE_DIM + half:], g[MLA_NOPE_DIM:MLA_NOPE_DIM + half]]),
        (0, LANES - MLA_QK_DIM)).reshape(1, LANES)
    n_pos = s_len // tm
    row = lambda i: (i, 0)
    return pl.pallas_call(
        _inproj_kernel,
        grid=(t // tm,),
        in_specs=[
            pl.BlockSpec((tm, d), row),
            _const_spec((1, d)),
            _const_spec((d, IN_COLS_PAD)),
            _const_spec((1, MLA_Q_RANK)),
            _const_spec((MLA_Q_RANK, 2 * MLA_PAD)),
            _const_spec((1, MLA_KV_RANK)),
            _const_spec((MLA_KV_RANK, MLA_PAD + MLA_VT_WIDTH)),
            pl.BlockSpec((tm, LANES), lambda i: (i % n_pos, 0)),
            pl.BlockSpec((tm, LANES), lambda i: (i % n_pos, 0)),
            _const_spec((1, LANES)), _const_spec((1, LANES)),
            _const_spec((1, LANES)), _const_spec((1, LANES)),
        ],
        out_specs=[
            pl.BlockSpec((tm, IN_SB), row),
            pl.BlockSpec((tm, MLA_PAD), row),
            pl.BlockSpec((tm, MLA_PAD), row),
            pl.BlockSpec((MLA_VT_WIDTH, tm), lambda i: (0, i)),
        ],
        out_shape=[
            jax.ShapeDtypeStruct((t, IN_SB), BF16),
            jax.ShapeDtypeStruct((t, MLA_PAD), BF16),
            jax.ShapeDtypeStruct((t, MLA_PAD), BF16),
            jax.ShapeDtypeStruct((MLA_VT_WIDTH, t), BF16),
        ],
        compiler_params=_params("arbitrary"),
        name="in_projection",
    )(x2, norm_g.reshape(1, d), w_in_pad, q_lat_g.reshape(1, -1), wq_pad, kv_lat_g.reshape(1, -1), wkv_pad,
      ctab, stab, pad_g(q_g), swap_g(q_g), pad_g(k_g), swap_g(k_g))


SB_EXP_UNDERFLOW = 110.0


def _sb_kernel(q_ref, k_ref, v_ref, u_ref, o_ref, acc_ref, carry_ref, z_ref, *, tq, tk):
    qi = pl.program_id(2)
    lane = lax.broadcasted_iota(jnp.int32, (1, LANES), 1)
    q = q_ref[...]
    q_heads = [jnp.where((lane >= j * SB_HEAD_DIM) & (lane < (j + 1) * SB_HEAD_DIM), q, jnp.zeros_like(q))
               for j in range(2)]
    acc_ref[...] = jnp.zeros_like(acc_ref)
    carry_ref[...] = jnp.zeros_like(carry_ref)

    def logits(j, c, lo_row=0, hi_row=tq):
        start = pl.multiple_of(c * tk, tk)
        z_ref[j, lo_row:hi_row, :] = _dot_nt(q_heads[j][lo_row:hi_row], k_ref[pl.ds(start, tk), :])

    def absorb(j, c, diagonal, lo_row=0, hi_row=tq):
        start = pl.multiple_of(c * tk, tk)
        z = z_ref[j, lo_row:hi_row, :]
        log_fail = -(jnp.maximum(z, 0.0) + jnp.log(1.0 + jnp.exp(-jnp.abs(z))))
        if diagonal:
            key = start + lax.broadcasted_iota(jnp.int32, z.shape, 1)
            qry = qi * tq + lo_row + lax.broadcasted_iota(jnp.int32, z.shape, 0)
            earlier = key < qry
            log_fail = jnp.where(earlier, log_fail, 0.0)
        hi = log_fail.astype(BF16)
        lo = (log_fail - hi.astype(F32)).astype(BF16)
        log_stick = _dot(hi, u_ref[...]) + _dot(lo, u_ref[...])
        carry = carry_ref[j, lo_row:hi_row, :]
        w = jnp.exp(z + log_fail + log_stick + jnp.tile(carry, (1, tk // LANES)))
        if diagonal:
            w = jnp.where(earlier, w, 0.0)
        acc_ref[j, lo_row:hi_row, :] += _dot(w.astype(BF16), v_ref[pl.ds(start, tk), :])
        carry_ref[j, lo_row:hi_row, :] = carry + (log_stick[:, :1] + log_fail[:, :1])

    def sweep(c, diagonal, lo_row=0, next_lo_row=0, hi_row=tq):
        logits(1, c, lo_row, hi_row)
        absorb(0, c, diagonal, lo_row, hi_row)
        logits(0, jnp.maximum(c - 1, 0), next_lo_row)
        absorb(1, c, diagonal, lo_row, hi_row)

    n_sub = tq // tk
    first = qi * n_sub + n_sub - 1
    logits(0, first, (n_sub - 1) * tk)
    for sub in reversed(range(n_sub)):
        sweep(qi * n_sub + sub, True, sub * tk, max(sub - 1, 0) * tk)

    def sticks_alive(lo_row, hi_row):
        return jnp.max(carry_ref[:, lo_row:hi_row, :]) > -SB_EXP_UNDERFLOW

    def earlier_chunk(state):
        c, _, rest_alive = state

        @pl.when(rest_alive)
        def _():
            sweep(c, False)

        @pl.when(jnp.logical_not(rest_alive))
        def _():
            sweep(c, False, hi_row=tk)

        return c - 1, sticks_alive(0, tk), sticks_alive(tk, tq)

    lax.while_loop(lambda state: (state[0] >= 0) & (state[1] | state[2]), earlier_chunk,
                   (qi * n_sub - 1, sticks_alive(0, tk), sticks_alive(tk, tq)))
    o_ref[...] = jnp.where(lane < SB_HEAD_DIM, acc_ref[0], acc_ref[1]).astype(o_ref.dtype)


def _sb_attention(qkv, tq, tk):
    b, s, _ = qkv.shape
    assert tq % tk == 0 and tq > tk and s % tq == 0
    n_pairs = SB_WIDTH // LANES
    tri = (jnp.arange(tk)[:, None] > jnp.arange(tk)[None, :]).astype(BF16)
    return pl.pallas_call(
        functools.partial(_sb_kernel, tq=tq, tk=tk),
        grid=(b, n_pairs, s // tq),
        in_specs=[
            pl.BlockSpec((None, tq, LANES), lambda bi, hp, qi: (bi, qi, hp)),
            pl.BlockSpec((None, s, LANES), lambda bi, hp, qi: (bi, 0, n_pairs + hp)),
            pl.BlockSpec((None, s, LANES), lambda bi, hp, qi: (bi, 0, 2 * n_pairs + hp)),
            _const_spec((tk, tk)),
        ],
        out_specs=pl.BlockSpec((None, tq, LANES), lambda bi, hp, qi: (bi, qi, hp)),
        out_shape=jax.ShapeDtypeStruct((b, s, SB_WIDTH), BF16),
        scratch_shapes=[pltpu.VMEM((2, tq, LANES), F32), pltpu.VMEM((2, tq, LANES), F32),
                        pltpu.VMEM((2, tq, tk), F32)],
        compiler_params=_params("arbitrary", "arbitrary", "arbitrary"),
        name="stick_breaking_attention",
    )(qkv, qkv, qkv, tri)


def _mla_kernel(q_ref, k_ref, vt_ref, o_ref, acc_ref, m_ref, s_ref, *, tq):
    qi = pl.program_id(2)
    acc_ref[...] = jnp.zeros_like(acc_ref)
    m_ref[...] = jnp.full(m_ref.shape, -jnp.inf, F32)

    def scores(j, c):
        start = pl.multiple_of(c * tq, tq)
        s_ref[j] = _dot_nt(k_ref[pl.ds(start, tq), j * LANES:(j + 1) * LANES],
                           q_ref[:, j * LANES:(j + 1) * LANES])

    def update(j, c, diagonal, k0=0, k1=tq, q0=0):
        start = pl.multiple_of(c * tq + k0, LANES)
        s = s_ref[j, k0:k1, q0:]
        if diagonal:
            key = k0 + lax.broadcasted_iota(jnp.int32, s.shape, 0)
            qry = q0 + lax.broadcasted_iota(jnp.int32, s.shape, 1)
            s = jnp.where(key <= qry, s, -jnp.inf)
        m_prev = m_ref[j, :, q0:]
        m_new = jnp.maximum(m_prev, jnp.max(s, axis=0, keepdims=True))
        p = jnp.exp2(s - m_new)
        vt = vt_ref[j * MLA_V_ROWS:(j + 1) * MLA_V_ROWS, pl.ds(start, k1 - k0)]
        acc_ref[j, :, q0:] = jnp.exp2(m_prev - m_new) * acc_ref[j, :, q0:] + _dot(vt, p.astype(BF16))
        m_ref[j, :, q0:] = m_new

    def update_diagonal(j):
        half = tq // 2
        update(j, qi, True, 0, half, 0)
        update(j, qi, True, half, tq, half)


    scores(0, 0)

    def earlier_block(c, _):
        scores(1, c)
        update(0, c, False)
        scores(0, c + 1)
        update(1, c, False)
        return 0

    lax.fori_loop(0, qi, earlier_block, 0)
    scores(1, qi)
    update_diagonal(0)
    update_diagonal(1)
    out_t = jnp.concatenate([acc_ref[j, :MLA_V_DIM, :] / acc_ref[j, MLA_V_DIM:MLA_V_DIM + 1, :] for j in range(2)],
                            axis=0)
    o_ref[...] = out_t.T.astype(o_ref.dtype)


def _mla_attention(q, k, vt, tq):
    b, s, _ = q.shape
    n_pairs = MLA_WIDTH // LANES
    return pl.pallas_call(
        functools.partial(_mla_kernel, tq=tq),
        grid=(b, n_pairs, s // tq),
        in_specs=[
            pl.BlockSpec((None, tq, 2 * LANES), lambda bi, hp, qi: (bi, qi, hp)),
            pl.BlockSpec((None, s, 2 * LANES), lambda bi, hp, qi: (bi, 0, hp)),
            pl.BlockSpec((2 * MLA_V_ROWS, s), lambda bi, hp, qi: (hp, bi)),
        ],
        out_specs=pl.BlockSpec((None, tq, LANES), lambda bi, hp, qi: (bi, qi, hp)),
        out_shape=jax.ShapeDtypeStruct((b, s, MLA_WIDTH), BF16),
        scratch_shapes=[pltpu.VMEM((2, MLA_V_ROWS, tq), F32), pltpu.VMEM((2, 1, tq), F32),
                        pltpu.VMEM((2, tq, tq), F32)],
        compiler_params=_params("arbitrary", "arbitrary", "arbitrary"),
        name="latent_attention",
    )(q, k, vt)


def _swiglu(x, wg_ref, wu_ref, wd_ref, f_chunk):
    d_ff = wg_ref.shape[-1]
    acc = None
    for c in range(d_ff // f_chunk):
        cols = slice(c * f_chunk, (c + 1) * f_chunk)
        gate = _dot(x, wg_ref[:, cols])
        up = _dot(x, wu_ref[:, cols])
        act = (gate * jax.nn.sigmoid(gate) * up).astype(BF16)
        part = _dot(act, wd_ref[cols, :])
        acc = part if acc is None else acc + part
    return acc


def _outproj_ffn_kernel(x_ref, osb_ref, omla_ref, wsb_ref, wmla_ref, g_ref, wg_ref, wu_ref, wd_ref, o_ref, *, f_chunk):
    h = x_ref[...] + _dot(osb_ref[...], wsb_ref[...]) + _dot(omla_ref[...], wmla_ref[...])
    hn = (_rms(h, h.shape[-1]) * g_ref[...]).astype(BF16)
    o_ref[...] = h + _swiglu(hn, wg_ref, wu_ref, wd_ref, f_chunk)


def _outproj_dense_ffn(x2, o_sb, o_mla, w_out, norm_g, w_gate, w_up, w_down, tm, f_chunk):
    t, d = x2.shape
    d_ff = w_gate.shape[-1]
    row = lambda i: (i, 0)
    resident = lambda shape: pl.BlockSpec(shape, lambda i: (0, 0), pipeline_mode=pl.Buffered(1))
    return pl.pallas_call(
        functools.partial(_outproj_ffn_kernel, f_chunk=f_chunk),
        grid=(t // tm,),
        in_specs=[pl.BlockSpec((tm, d), row), pl.BlockSpec((tm, SB_WIDTH), row), pl.BlockSpec((tm, MLA_WIDTH), row),
                  resident((SB_WIDTH, d)), resident((MLA_WIDTH, d)), resident((1, d)),
                  resident((d, d_ff)), resident((d, d_ff)), resident((d_ff, d))],
        out_specs=pl.BlockSpec((tm, d), row),
        out_shape=jax.ShapeDtypeStruct((t, d), F32),
        compiler_params=_params("arbitrary"),
        name="out_projection_dense_swiglu",
    )(x2, o_sb, o_mla, w_out[:SB_WIDTH].astype(BF16), w_out[SB_WIDTH:].astype(BF16), norm_g.reshape(1, d),
      w_gate.astype(BF16), w_up.astype(BF16), w_down.astype(BF16))


def _layer0(x2, b, s, att_norm, att_w_in, q_lat_g, w_q_up, kv_lat_g, w_kv_up, q_g, k_g, w_out,
            dffn_norm, w_gate, w_up, w_down):
    qkv, qm, km, vm = _in_projection(x2, s, att_norm, att_w_in, q_lat_g, w_q_up, kv_lat_g, w_kv_up, q_g, k_g,
                                     TOKEN_TILE)
    o_sb = _sb_attention(qkv.reshape(b, s, -1), SB_Q_TILE, SB_K_TILE).reshape(b * s, -1)
    o_mla = _mla_attention(qm.reshape(b, s, -1), km.reshape(b, s, -1), vm, MLA_Q_TILE).reshape(b * s, -1)
    return _outproj_dense_ffn(x2, o_sb, o_mla, w_out, dffn_norm, w_gate, w_up, w_down, TOKEN_TILE, FFN_CHUNK)


SSM_SEGS = 8
SSM_SEG_LEN = 128
SSM_TILE = SSM_SEGS * SSM_SEG_LEN
SSM_CH_CHUNK = LANES
SSM_ST_CHUNK = SSM_CH_CHUNK // SSM_GROUP * SSM_STATE
SSM_SCAN_TILES = 8


def _s5_prep_kernel(are_ref, aim_ref, logdt_ref, bre_ref, bim_ref, cre_ref, cim_ref,
                    wbu_ref, wc_ref, pow_ref):
    n = SSM_NSTATE
    a_re = jnp.minimum(are_ref[...], EIG_RE_MAX)
    a_im = aim_ref[...]
    dt = jnp.exp(logdt_ref[...])
    mag = jnp.exp(a_re * dt)
    lam_re = mag * jnp.cos(a_im * dt)
    lam_im = mag * jnp.sin(a_im * dt)
    den = a_re * a_re + a_im * a_im
    num_re = lam_re - 1.0
    coef_re = (num_re * a_re + lam_im * a_im) / den
    coef_im = (lam_im * a_re - num_re * a_im) / den
    b_re = bre_ref[...]
    b_im = bim_ref[...]
    in_group = (lax.broadcasted_iota(jnp.int32, (SSM_CH_CHUNK, n), 0) // SSM_GROUP
                == lax.broadcasted_iota(jnp.int32, (SSM_CH_CHUNK, n), 1) % SSM_ST_CHUNK // SSM_STATE)
    wbu_ref[:, :n] = jnp.where(in_group, coef_re * b_re - coef_im * b_im, 0.0).astype(BF16)
    wbu_ref[:, n:] = jnp.where(in_group, coef_re * b_im + coef_im * b_re, 0.0).astype(BF16)
    in_group_t = (lax.broadcasted_iota(jnp.int32, (n, SSM_CH_CHUNK), 0) % SSM_ST_CHUNK // SSM_STATE
                  == lax.broadcasted_iota(jnp.int32, (n, SSM_CH_CHUNK), 1) // SSM_GROUP)
    wc_ref[:n, :] = jnp.where(in_group_t, cre_ref[...], 0.0).astype(BF16)
    wc_ref[n:, :] = jnp.where(in_group_t, -cim_ref[...], 0.0).astype(BF16)
    steps = jnp.where(lax.broadcasted_iota(jnp.int32, (SSM_SEGS, n), 0) == 0, 1.0, float(SSM_SEG_LEN))
    mag_k = jnp.exp(steps * (a_re * dt))
    ang_k = steps * (a_im * dt)
    pow_ref[:, :n] = mag_k * jnp.cos(ang_k)
    pow_ref[:, n:] = mag_k * jnp.sin(ang_k)


def _s5_prepare(a_re, a_im, log_dt, b_re, b_im, c_re, c_im):
    n = SSM_NSTATE
    row = lambda a: a.reshape(1, n)
    reps = SSM_CH_CHUNK // SSM_GROUP
    b_t = lambda w: jnp.tile(w.transpose(2, 0, 1).reshape(SSM_GROUP, n), (reps, 1))
    c_t = lambda w: jnp.tile(w.transpose(0, 2, 1).reshape(n, SSM_GROUP), (1, reps))
    return pl.pallas_call(
        _s5_prep_kernel,
        out_shape=[jax.ShapeDtypeStruct((SSM_CH_CHUNK, 2 * n), BF16),
                   jax.ShapeDtypeStruct((2 * n, SSM_CH_CHUNK), BF16),
                   jax.ShapeDtypeStruct((SSM_SEGS, 2 * n), F32)],
        compiler_params=pltpu.CompilerParams(vmem_limit_bytes=VMEM_LIMIT),
        name="s5_discretize",
    )(row(a_re), row(a_im), row(jnp.repeat(log_dt, SSM_STATE)), b_t(b_re), b_t(b_im), c_t(c_re), c_t(c_im))


def _s5_kernel(h_ref, g_ref, win_ref, wbu_ref, pow_ref, wc_ref, d_ref, wglu_ref, o_ref,
               ut_ref, up_ref, st_ref, carry_ref, yp_ref, yt_ref):
    n = SSM_NSTATE
    nt = n // LANES
    n_ch = SSM_WIDTH // LANES
    per = SSM_ST_CHUNK // LANES
    d_model = h_ref.shape[-1]

    @pl.when(pl.program_id(1) == 0)
    def _():
        carry_ref[...] = jnp.zeros_like(carry_ref)

    x = h_ref[...]
    xn = _rms(x, d_model) * g_ref[...]
    u = _dot(xn.astype(BF16), win_ref[...])
    for k in range(n_ch):
        ut_ref[k] = u[:, k * LANES:(k + 1) * LANES]

    def to_step_major(i, _):
        dst = pl.ds(pl.multiple_of(i * SSM_SEGS, SSM_SEGS), SSM_SEGS)
        for k in range(n_ch):
            up_ref[k, dst, :] = ut_ref[k, pl.ds(i, SSM_SEGS, stride=SSM_SEG_LEN), :]
        return 0

    lax.fori_loop(0, SSM_SEG_LEN, to_step_major, 0)

    for c in range(n_ch):
        u16 = up_ref[c].astype(BF16)
        for part in range(2):
            cols = slice(part * n + c * SSM_ST_CHUNK, part * n + (c + 1) * SSM_ST_CHUNK)
            bu = _dot(u16, wbu_ref[:, cols])
            for k in range(per):
                st_ref[part * nt + c * per + k] = bu[:, k * LANES:(k + 1) * LANES]

    for grp in range(nt // SSM_SCAN_TILES):
        tiles = range(grp * SSM_SCAN_TILES, (grp + 1) * SSM_SCAN_TILES)
        re_l = [slice(k * LANES, (k + 1) * LANES) for k in tiles]
        im_l = [slice(n + k * LANES, n + (k + 1) * LANES) for k in tiles]
        lam_re = [jnp.broadcast_to(pow_ref[0:1, c], (SSM_SEGS, LANES)) for c in re_l]
        lam_im = [jnp.broadcast_to(pow_ref[0:1, c], (SSM_SEGS, LANES)) for c in im_l]

        def scan(state, store):
            def step(i, state):
                rows = pl.ds(pl.multiple_of(i * SSM_SEGS, SSM_SEGS), SSM_SEGS)
                new_state = []
                for idx, k in enumerate(tiles):
                    s_re, s_im = state[idx]
                    n_re = lam_re[idx] * s_re - lam_im[idx] * s_im + st_ref[k, rows, :]
                    n_im = lam_re[idx] * s_im + lam_im[idx] * s_re + st_ref[nt + k, rows, :]
                    if store:
                        st_ref[k, rows, :] = n_re
                        st_ref[nt + k, rows, :] = n_im
                    new_state.append((n_re, n_im))
                return tuple(new_state)

            return lax.fori_loop(0, SSM_SEG_LEN, step, state)

        zero = jnp.zeros((SSM_SEGS, LANES), F32)
        ends = scan(tuple((zero, zero) for _ in tiles), False)
        starts = []
        for idx in range(len(tiles)):
            e_re, e_im = ends[idx]
            p_re = pow_ref[1:2, re_l[idx]]
            p_im = pow_ref[1:2, im_l[idx]]
            c_re = carry_ref[:, re_l[idx]]
            c_im = carry_ref[:, im_l[idx]]
            rows_re, rows_im = [], []
            for j in range(SSM_SEGS):
                rows_re.append(c_re)
                rows_im.append(c_im)
                c_re, c_im = (e_re[j:j + 1] + (p_re * c_re - p_im * c_im),
                              e_im[j:j + 1] + (p_re * c_im + p_im * c_re))
            carry_ref[:, re_l[idx]] = c_re
            carry_ref[:, im_l[idx]] = c_im
            starts.append((jnp.concatenate(rows_re, axis=0), jnp.concatenate(rows_im, axis=0)))
        scan(tuple(starts), True)

    for c in range(n_ch):
        h_re = jnp.concatenate([st_ref[c * per + k] for k in range(per)], axis=-1).astype(BF16)
        h_im = jnp.concatenate([st_ref[nt + c * per + k] for k in range(per)], axis=-1).astype(BF16)
        rows_re = slice(c * SSM_ST_CHUNK, (c + 1) * SSM_ST_CHUNK)
        rows_im = slice(n + c * SSM_ST_CHUNK, n + (c + 1) * SSM_ST_CHUNK)
        yp_ref[c] = _dot(h_re, wc_ref[rows_re, :]) + _dot(h_im, wc_ref[rows_im, :])

    def to_token_major(i, _):
        src = pl.ds(pl.multiple_of(i * SSM_SEGS, SSM_SEGS), SSM_SEGS)
        for k in range(n_ch):
            yt_ref[k, pl.ds(i, SSM_SEGS, stride=SSM_SEG_LEN), :] = yp_ref[k, src, :]
        return 0

    lax.fori_loop(0, SSM_SEG_LEN, to_token_major, 0)
    y = jnp.concatenate([yt_ref[k] for k in range(n_ch)], axis=-1)
    y = jax.nn.gelu(y + d_ref[...] * u)
    z = _dot(y.astype(BF16), wglu_ref[...])
    o_ref[...] = x + z[:, :d_model] * jax.nn.sigmoid(z[:, d_model:])


def _s5_layer(h, b, s, norm_g, w_in, a_re, a_im, log_dt, b_re, b_im, c_re, c_im, d_skip, w_glu):
    t, d = h.shape
    n = SSM_NSTATE
    w_bu, w_c, powers = _s5_prepare(a_re, a_im, log_dt, b_re, b_im, c_re, c_im)
    tiles = s // SSM_TILE
    resident = lambda shape: pl.BlockSpec(shape, lambda bi, ti: (0, 0), pipeline_mode=pl.Buffered(1))
    row = lambda bi, ti: (bi * tiles + ti, 0)
    ch_tiles = pltpu.VMEM((SSM_WIDTH // LANES, SSM_TILE, LANES), F32)
    return pl.pallas_call(
        _s5_kernel,
        grid=(b, tiles),
        in_specs=[pl.BlockSpec((SSM_TILE, d), row), resident((1, d)), resident((d, SSM_WIDTH)),
                  resident((SSM_CH_CHUNK, 2 * n)), resident((SSM_SEGS, 2 * n)), resident((2 * n, SSM_CH_CHUNK)),
                  resident((1, SSM_WIDTH)), resident((SSM_WIDTH, 2 * d))],
        out_specs=pl.BlockSpec((SSM_TILE, d), row),
        out_shape=jax.ShapeDtypeStruct((t, d), F32),
        scratch_shapes=[ch_tiles, ch_tiles, pltpu.VMEM((2 * n // LANES, SSM_TILE, LANES), F32),
                        pltpu.VMEM((1, 2 * n), F32), ch_tiles, ch_tiles],
        compiler_params=_params("arbitrary", "arbitrary"),
        name="s5_mixer",
    )(h, norm_g.reshape(1, d), w_in.astype(BF16), w_bu, powers, w_c, d_skip.reshape(1, -1), w_glu.astype(BF16))


MOE_BLOCK = 256
META_E1, META_E2, META_G1, META_G2, META_R1, META_R2 = range(6)


def _router_kernel(h_ref, g_ref, wr_ref, tri_ref, meta_ref, counts_ref, run_ref):
    @pl.when(pl.program_id(0) == 0)
    def _():
        run_ref[...] = jnp.zeros_like(run_ref)

    x = h_ref[...]
    xn = _rms(x, x.shape[-1]) * g_ref[...]
    logits = jnp.dot(xn, wr_ref[...], preferred_element_type=F32, precision=lax.Precision.HIGHEST)
    tm = logits.shape[0]
    e = lax.broadcasted_iota(jnp.int32, (tm, N_EXPERTS), 1)
    v1 = jnp.max(logits, axis=-1, keepdims=True)
    i1 = jnp.min(jnp.where(logits == v1, e, N_EXPERTS), axis=-1, keepdims=True)
    rest = jnp.where(e == i1, -jnp.inf, logits)
    v2 = jnp.max(rest, axis=-1, keepdims=True)
    i2 = jnp.min(jnp.where(rest == v2, e, N_EXPERTS), axis=-1, keepdims=True)
    ex = jnp.exp(v2 - v1)
    g1 = 1.0 / (1.0 + ex)
    g2 = ex / (1.0 + ex)
    oh1 = (e == i1).astype(F32)
    oh2 = (e == i2).astype(F32)
    both = oh1 + oh2
    before = _dot(tri_ref[...], both.astype(BF16)) + run_ref[...]
    r1 = jnp.sum(oh1 * before, axis=-1, keepdims=True)
    r2 = jnp.sum(oh2 * before, axis=-1, keepdims=True)
    run_ref[...] += jnp.sum(both, axis=0, keepdims=True)
    counts_ref[...] = run_ref[...]
    meta = jnp.zeros((tm, N_EXPERTS), F32)
    for lane, val in ((META_E1, i1.astype(F32)), (META_E2, i2.astype(F32)), (META_G1, g1), (META_G2, g2),
                      (META_R1, r1), (META_R2, r2)):
        meta = jnp.where(e == lane, val, meta)
    meta_ref[...] = meta


def _router(h, norm_g, w_router, tm):
    t, d = h.shape
    row = lambda i: (i, 0)
    tri = (jnp.arange(tm)[:, None] > jnp.arange(tm)[None, :]).astype(BF16)
    return pl.pallas_call(
        _router_kernel,
        grid=(t // tm,),
        in_specs=[pl.BlockSpec((tm, d), row), _const_spec((1, d)), _const_spec((d, N_EXPERTS)),
                  _const_spec((tm, tm))],
        out_specs=[pl.BlockSpec((tm, N_EXPERTS), row), _const_spec((1, N_EXPERTS))],
        out_shape=[jax.ShapeDtypeStruct((t, N_EXPERTS), F32), jax.ShapeDtypeStruct((1, N_EXPERTS), F32)],
        scratch_shapes=[pltpu.VMEM((1, N_EXPERTS), F32)],
        compiler_params=_params("arbitrary"),
        name="moe_router",
    )(h, norm_g.reshape(1, d), w_router, tri)


def _slot_map_kernel(d1_ref, d2_ref, tok_ref):
    def clear(slot, _):
        tok_ref[slot] = 0
        return 0

    lax.fori_loop(0, tok_ref.shape[0], clear, 0, unroll=16)

    def place(tok, _):
        tok_ref[d1_ref[tok]] = tok
        tok_ref[d2_ref[tok]] = tok
        return 0

    lax.fori_loop(0, d1_ref.shape[0], place, 0, unroll=8)


def _slot_map(dest1, dest2, n_slots):
    smem = pl.BlockSpec(memory_space=pltpu.SMEM)
    return pl.pallas_call(
        _slot_map_kernel,
        in_specs=[smem, smem], out_specs=smem,
        out_shape=jax.ShapeDtypeStruct((n_slots,), jnp.int32),
        name="moe_slot_map",
    )(dest1, dest2)


def _row_copy(src_ref, src_row, dst_ref, dst_row, sem):
    return pltpu.make_async_copy(src_ref.at[pl.ds(src_row, 1)], dst_ref.at[pl.ds(dst_row, 1)], sem)


def _expert_ffn_kernel(be_ref, bv_ref, tok_ref, h_ref, g_ref, wg_ref, wu_ref, wd_ref, ys_ref, buf_ref, sem, *, f_chunk):
    del be_ref
    i = pl.program_id(0)
    last = pl.num_programs(0) - 1
    cur = i % 2

    def gather(block, buf):
        base = block * MOE_BLOCK
        for r in range(MOE_BLOCK):
            _row_copy(h_ref, tok_ref[base + r], buf_ref.at[buf], r, sem.at[buf]).start(priority=r % 2)

    def wait_gather(buf):
        pltpu.make_async_copy(h_ref.at[pl.ds(0, MOE_BLOCK)], buf_ref.at[buf], sem.at[buf]).wait()

    @pl.when(i == 0)
    def _():
        gather(0, 0)

    gather(jnp.minimum(i + 1, last), 1 - cur)
    wait_gather(cur)
    valid = bv_ref[i] != 0

    @pl.when(valid)
    def _():
        x = buf_ref[cur]
        xn = (_rms(x, x.shape[-1]) * g_ref[...]).astype(BF16)
        ys_ref[...] = _swiglu(xn, wg_ref, wu_ref, wd_ref, f_chunk)

    @pl.when(jnp.logical_not(valid))
    def _():
        ys_ref[...] = jnp.zeros_like(ys_ref)

    @pl.when(i == last)
    def _():
        wait_gather(1 - cur)


def _expert_ffn(h, norm_g, slot_tok, blk_expert, blk_valid, w_gate, w_up, w_down, f_chunk):
    n_slots = slot_tok.shape[0]
    d = h.shape[-1]
    d_ff = w_gate.shape[-1]
    return pl.pallas_call(
        functools.partial(_expert_ffn_kernel, f_chunk=f_chunk),
        grid_spec=pltpu.PrefetchScalarGridSpec(
            num_scalar_prefetch=3, grid=(n_slots // MOE_BLOCK,),
            in_specs=[pl.BlockSpec(memory_space=pl.ANY),
                      pl.BlockSpec((1, d), lambda i, be, bv, st: (0, 0)),
                      pl.BlockSpec((None, d, d_ff), lambda i, be, bv, st: (be[i], 0, 0)),
                      pl.BlockSpec((None, d, d_ff), lambda i, be, bv, st: (be[i], 0, 0)),
                      pl.BlockSpec((None, d_ff, d), lambda i, be, bv, st: (be[i], 0, 0))],
            out_specs=pl.BlockSpec((MOE_BLOCK, d), lambda i, be, bv, st: (i, 0)),
            scratch_shapes=[pltpu.VMEM((2, MOE_BLOCK, d), F32), pltpu.SemaphoreType.DMA((2,))]),
        out_shape=jax.ShapeDtypeStruct((n_slots, d), F32),
        compiler_params=_params("arbitrary"),
        name="expert_swiglu",
    )(blk_expert, blk_valid, slot_tok, h, norm_g.reshape(1, d), w_gate.astype(BF16), w_up.astype(BF16),
      w_down.astype(BF16))


def _combine_kernel(d1_ref, d2_ref, h_ref, meta_ref, ys_ref, o_ref, a_ref, b_ref, sem, *, tm):
    base = pl.program_id(0) * tm

    def issue(r, _):
        tok = base + r
        _row_copy(ys_ref, d1_ref[tok], a_ref, r, sem).start(priority=0)
        _row_copy(ys_ref, d2_ref[tok], b_ref, r, sem).start(priority=1)
        return 0

    lax.fori_loop(0, tm, issue, 0, unroll=8)

    pltpu.make_async_copy(ys_ref.at[pl.ds(0, tm)], a_ref, sem).wait()
    pltpu.make_async_copy(ys_ref.at[pl.ds(0, tm)], b_ref, sem).wait()
    meta = meta_ref[...]
    g1 = meta[:, META_G1:META_G1 + 1]
    g2 = meta[:, META_G2:META_G2 + 1]
    o_ref[...] = h_ref[...] + (g1 * a_ref[...] + g2 * b_ref[...])


def _combine(h, meta, ys, dest1, dest2, tm):
    t, d = h.shape
    row = lambda i, d1, d2: (i, 0)
    return pl.pallas_call(
        functools.partial(_combine_kernel, tm=tm),
        grid_spec=pltpu.PrefetchScalarGridSpec(
            num_scalar_prefetch=2, grid=(t // tm,),
            in_specs=[pl.BlockSpec((tm, d), row), pl.BlockSpec((tm, N_EXPERTS), row),
                      pl.BlockSpec(memory_space=pl.ANY)],
            out_specs=pl.BlockSpec((tm, d), row),
            scratch_shapes=[pltpu.VMEM((tm, d), F32), pltpu.VMEM((tm, d), F32), pltpu.SemaphoreType.DMA(())]),
        out_shape=jax.ShapeDtypeStruct((t, d), F32),
        compiler_params=_params("arbitrary"),
        name="moe_combine",
    )(dest1, dest2, h, meta, ys)


def _moe_layer(h, norm_g, w_router, w_gate, w_up, w_down):
    t, d = h.shape
    meta, counts = _router(h, norm_g, w_router, ROUTER_TILE)
    counts = counts.reshape(N_EXPERTS).astype(jnp.int32)
    padded = (counts + MOE_BLOCK - 1) // MOE_BLOCK * MOE_BLOCK
    pad_end = jnp.cumsum(padded)
    pad_start = pad_end - padded
    e1 = meta[:, META_E1].astype(jnp.int32)
    e2 = meta[:, META_E2].astype(jnp.int32)
    dest1 = pad_start[e1] + meta[:, META_R1].astype(jnp.int32)
    dest2 = pad_start[e2] + meta[:, META_R2].astype(jnp.int32)
    n_slots = 2 * t + N_EXPERTS * MOE_BLOCK
    blk_start = jnp.arange(n_slots // MOE_BLOCK, dtype=jnp.int32) * MOE_BLOCK
    blk_expert = jnp.minimum(jnp.sum(blk_start[:, None] >= pad_end[None, :], axis=1), N_EXPERTS - 1).astype(jnp.int32)
    blk_valid = (blk_start < pad_end[-1]).astype(jnp.int32)
    slot_tok = _slot_map(dest1, dest2, n_slots)
    ys = _expert_ffn(h, norm_g, slot_tok, blk_expert, blk_valid, w_gate, w_up, w_down, FFN_CHUNK)
    return _combine(h, meta, ys, dest1, dest2, ROUTER_TILE)


def kernel(x, att_norm, att_w_in, att_q_latent_norm, att_w_q_up, att_kv_latent_norm, att_w_kv_up, att_q_norm, att_k_norm, att_w_out, dffn_norm, dffn_w_gate, dffn_w_up, dffn_w_down, ssm_norm, ssm_w_in, ssm_a_re, ssm_a_im, ssm_log_dt, ssm_b_re, ssm_b_im, ssm_c_re, ssm_c_im, ssm_d, ssm_w_glu, moe_norm, moe_router, moe_w_gate, moe_w_up, moe_w_down):
    b, s, d = x.shape
    x2 = x.reshape(b * s, d)
    h = _layer0(x2, b, s, att_norm[0], att_w_in[0], att_q_latent_norm[0], att_w_q_up[0], att_kv_latent_norm[0],
                att_w_kv_up[0], att_q_norm[0], att_k_norm[0], att_w_out[0], dffn_norm[0], dffn_w_gate[0],
                dffn_w_up[0], dffn_w_down[0])
    h = _s5_layer(h, b, s, ssm_norm[0], ssm_w_in[0], ssm_a_re[0], ssm_a_im[0], ssm_log_dt[0], ssm_b_re[0],
                  ssm_b_im[0], ssm_c_re[0], ssm_c_im[0], ssm_d[0], ssm_w_glu[0])
    h = _moe_layer(h, moe_norm[0], moe_router[0], moe_w_gate[0], moe_w_up[0], moe_w_down[0])
    return h.reshape(b, s, d)
```

```python
import functools
import math

import jax
import jax.numpy as jnp
from jax import lax
from jax.experimental import pallas as pl
from jax.experimental.pallas import tpu as pltpu

F32 = jnp.float32
BF16 = jnp.bfloat16

EPS = 1e-6
LANES = 128
SB_HEADS = 8
SB_HEAD_DIM = 64
SB_WIDTH = SB_HEADS * SB_HEAD_DIM
MLA_HEADS = 8
MLA_Q_RANK = 256
MLA_KV_RANK = 128
MLA_NOPE_DIM = 64
MLA_ROPE_DIM = 32
MLA_QK_DIM = MLA_NOPE_DIM + MLA_ROPE_DIM
MLA_V_DIM = 64
MLA_WIDTH = MLA_HEADS * MLA_V_DIM
ROPE_THETA = 10000.0
SSM_GROUP = 16
SSM_GROUPS = 32
SSM_STATE = 64
SSM_WIDTH = SSM_GROUP * SSM_GROUPS
SSM_NSTATE = SSM_GROUPS * SSM_STATE
EIG_RE_MAX = -1e-4
N_EXPERTS = 8
VMEM_LIMIT = 56 * 1024 * 1024

TOKEN_TILE = 512
FFN_CHUNK = 1792
SB_Q_TILE = 512
SB_K_TILE = 256
MLA_Q_TILE = 1024
ROUTER_TILE = 256


def _params(*sem):
    return pltpu.CompilerParams(dimension_semantics=sem, vmem_limit_bytes=VMEM_LIMIT)


def _rms(x, n):
    return x * lax.rsqrt(jnp.sum(x * x, axis=-1, keepdims=True) * (1.0 / n) + EPS)


def _dot(a, b):
    return jnp.dot(a, b, preferred_element_type=F32)


def _dot_nt(a, b):
    return lax.dot_general(a, b, (((1,), (1,)), ((), ())), preferred_element_type=F32)


def _const_spec(shape):
    return pl.BlockSpec(shape, lambda *_: (0,) * len(shape))


IN_SB = 3 * SB_WIDTH
IN_CQ = IN_SB
IN_CKV = IN_CQ + MLA_Q_RANK
IN_KR = IN_CKV + MLA_KV_RANK
IN_KRS = IN_KR + LANES
IN_COLS_PAD = IN_KRS + LANES
MLA_PAD = MLA_HEADS * LANES
MLA_V_ROWS = MLA_V_DIM + 16
MLA_VT_WIDTH = MLA_HEADS * MLA_V_ROWS


def _inproj_kernel(x_ref, g_ref, win_ref, qlg_ref, wq_ref, kvlg_ref, wkv_ref, ctab_ref, stab_ref,
                   qg_ref, qgs_ref, kg_ref, kgs_ref, qkv_ref, qm_ref, km_ref, vm_ref):
    x = x_ref[...]
    xn = _rms(x, x.shape[-1]) * g_ref[...]
    hh = _dot(xn.astype(BF16), win_ref[...])
    qkv_ref[:, :SB_WIDTH] = (hh[:, :SB_WIDTH] * (SB_HEAD_DIM ** -0.5)).astype(BF16)
    qkv_ref[:, SB_WIDTH:] = hh[:, SB_WIDTH:IN_SB].astype(BF16)
    cq = _rms(hh[:, IN_CQ:IN_CKV], MLA_Q_RANK) * qlg_ref[...]
    qf = _dot(cq.astype(BF16), wq_ref[...])
    ckv = _rms(hh[:, IN_CKV:IN_KR], MLA_KV_RANK) * kvlg_ref[...]
    kvf = _dot(ckv.astype(BF16), wkv_ref[...])
    v_col = lax.broadcasted_iota(jnp.int32, (1, MLA_VT_WIDTH), 1)
    v_ones = jnp.where(v_col % MLA_V_ROWS >= MLA_V_DIM, 1.0, 0.0)
    vm_ref[...] = (kvf[:, MLA_PAD:] + v_ones).T.astype(BF16)
    kr = hh[:, IN_KR:IN_KRS]
    krs = hh[:, IN_KRS:IN_COLS_PAD]
    ctab = ctab_ref[...]
    stab = stab_ref[...]
    scale = MLA_QK_DIM ** -0.5 * math.log2(math.e)
    cq_t = ctab * (qg_ref[...] * scale)
    sq_t = stab * (qgs_ref[...] * scale)
    ck_t = ctab * kg_ref[...]
    sk_t = stab * kgs_ref[...]
    for h in range(MLA_HEADS):
        lo, hi = h * LANES, (h + 1) * LANES
        qh = qf[:, lo:hi]
        qs = qf[:, MLA_PAD + lo:MLA_PAD + hi]
        q_inv = lax.rsqrt(jnp.sum(qh * qh, axis=-1, keepdims=True) * (1.0 / MLA_QK_DIM) + EPS)
        qm_ref[:, lo:hi] = ((qh * cq_t + qs * sq_t) * q_inv).astype(BF16)
        kh = kvf[:, lo:hi] + kr
        k_inv = lax.rsqrt(jnp.sum(kh * kh, axis=-1, keepdims=True) * (1.0 / MLA_QK_DIM) + EPS)
        km_ref[:, lo:hi] = ((kh * ck_t + krs * sk_t) * k_inv).astype(BF16)


def _pad_heads(w, n_heads, width):
    k = w.shape[0]
    w = w.reshape(k, n_heads, width)
    return jnp.pad(w, ((0, 0), (0, 0), (0, LANES - width))).reshape(k, n_heads * LANES)


def _swap_rope(w):
    half = MLA_ROPE_DIM // 2
    return jnp.concatenate([jnp.zeros_like(w[..., :MLA_NOPE_DIM]),
                            w[..., MLA_NOPE_DIM + half:], w[..., MLA_NOPE_DIM:MLA_NOPE_DIM + half]], axis=-1)


def _rope_lane_tables(s_len):
    inv_freq = ROPE_THETA ** (-jnp.arange(0, MLA_ROPE_DIM, 2, dtype=F32) / MLA_ROPE_DIM)
    ang = jnp.arange(s_len, dtype=F32)[:, None] * inv_freq[None, :]
    cos, sin = jnp.cos(ang), jnp.sin(ang)
    ones = jnp.ones((s_len, MLA_NOPE_DIM), F32)
    zeros_n = jnp.zeros((s_len, MLA_NOPE_DIM), F32)
    zeros_t = jnp.zeros((s_len, LANES - MLA_QK_DIM), F32)
    ctab = jnp.concatenate([ones, cos, cos, zeros_t], axis=-1)
    stab = jnp.concatenate([zeros_n, -sin, sin, zeros_t], axis=-1)
    return ctab, stab


def _in_projection(x2, s_len, norm_g, w_in, q_lat_g, w_q_up, kv_lat_g, w_kv_up, q_g, k_g, tm):
    t, d = x2.shape
    off_kr = 3 * SB_WIDTH + MLA_Q_RANK + MLA_KV_RANK
    w_rope = w_in[:, off_kr:]
    zeros_n = jnp.zeros((d, MLA_NOPE_DIM), F32)
    zeros_t = jnp.zeros((d, LANES - MLA_QK_DIM), F32)
    half = MLA_ROPE_DIM // 2
    w_in_pad = jnp.concatenate(
        [w_in[:, :off_kr], zeros_n, w_rope, zeros_t,
         zeros_n, w_rope[:, half:], w_rope[:, :half], zeros_t], axis=-1).astype(BF16)
    wq3 = w_q_up.reshape(MLA_Q_RANK, MLA_HEADS, MLA_QK_DIM)
    wq_pad = jnp.concatenate(
        [_pad_heads(w_q_up, MLA_HEADS, MLA_QK_DIM),
         _pad_heads(_swap_rope(wq3).reshape(MLA_Q_RANK, -1), MLA_HEADS, MLA_QK_DIM)], axis=-1).astype(BF16)
    wkv3 = w_kv_up.reshape(MLA_KV_RANK, MLA_HEADS, MLA_NOPE_DIM + MLA_V_DIM)
    wkv_pad = jnp.concatenate(
        [_pad_heads(wkv3[..., :MLA_NOPE_DIM].reshape(MLA_KV_RANK, -1), MLA_HEADS, MLA_NOPE_DIM),
         jnp.pad(wkv3[..., MLA_NOPE_DIM:], ((0, 0), (0, 0), (0, MLA_V_ROWS - MLA_V_DIM))).reshape(MLA_KV_RANK, -1)],
        axis=-1).astype(BF16)
    ctab, stab = _rope_lane_tables(s_len)
    pad_g = lambda g: jnp.pad(g, (0, LANES - MLA_QK_DIM)).reshape(1, LANES)
    swap_g = lambda g: jnp.pad(jnp.concatenate(
        [jnp.zeros((MLA_NOPE_DIM,), F32), g[MLA_NOPE_DIM + half:], g[MLA_NOPE_DIM:MLA_NOPE_DIM + half]]),
        (0, LANES - MLA_QK_DIM)).reshape(1, LANES)
    n_pos = s_len // tm
    row = lambda i: (i, 0)
    return pl.pallas_call(
        _inproj_kernel,
        grid=(t // tm,),
        in_specs=[
            pl.BlockSpec((tm, d), row),
            _const_spec((1, d)),
            _const_spec((d, IN_COLS_PAD)),
            _const_spec((1, MLA_Q_RANK)),
            _const_spec((MLA_Q_RANK, 2 * MLA_PAD)),
            _const_spec((1, MLA_KV_RANK)),
            _const_spec((MLA_KV_RANK, MLA_PAD + MLA_VT_WIDTH)),
            pl.BlockSpec((tm, LANES), lambda i: (i % n_pos, 0)),
            pl.BlockSpec((tm, LANES), lambda i: (i % n_pos, 0)),
            _const_spec((1, LANES)), _const_spec((1, LANES)),
            _const_spec((1, LANES)), _const_spec((1, LANES)),
        ],
        out_specs=[
            pl.BlockSpec((tm, IN_SB), row),
            pl.BlockSpec((tm, MLA_PAD), row),
            pl.BlockSpec((tm, MLA_PAD), row),
            pl.BlockSpec((MLA_VT_WIDTH, tm), lambda i: (0, i)),
        ],
        out_shape=[
            jax.ShapeDtypeStruct((t, IN_SB), BF16),
            jax.ShapeDtypeStruct((t, MLA_PAD), BF16),
            jax.ShapeDtypeStruct((t, MLA_PAD), BF16),
            jax.ShapeDtypeStruct((MLA_VT_WIDTH, t), BF16),
        ],
        compiler_params=_params("arbitrary"),
        name="in_projection",
    )(x2, norm_g.reshape(1, d), w_in_pad, q_lat_g.reshape(1, -1), wq_pad, kv_lat_g.reshape(1, -1), wkv_pad,
      ctab, stab, pad_g(q_g), swap_g(q_g), pad_g(k_g), swap_g(k_g))


SB_EXP_UNDERFLOW = 110.0


def _sb_kernel(q_ref, k_ref, v_ref, u_ref, o_ref, acc_ref, carry_ref, z_ref, *, tq, tk):
    qi = pl.program_id(2)
    lane = lax.broadcasted_iota(jnp.int32, (1, LANES), 1)
    q = q_ref[...]
    q_heads = [jnp.where((lane >= j * SB_HEAD_DIM) & (lane < (j + 1) * SB_HEAD_DIM), q, jnp.zeros_like(q))
               for j in range(2)]
    acc_ref[...] = jnp.zeros_like(acc_ref)
    carry_ref[...] = jnp.zeros_like(carry_ref)

    def logits(j, c, lo_row=0, hi_row=tq):
        start = pl.multiple_of(c * tk, tk)
        z_ref[j, lo_row:hi_row, :] = _dot_nt(q_heads[j][lo_row:hi_row], k_ref[pl.ds(start, tk), :])

    def absorb(j, c, diagonal, lo_row=0, hi_row=tq):
        start = pl.multiple_of(c * tk, tk)
        z = z_ref[j, lo_row:hi_row, :]
        log_fail = -(jnp.maximum(z, 0.0) + jnp.log(1.0 + jnp.exp(-jnp.abs(z))))
        if diagonal:
            key = start + lax.broadcasted_iota(jnp.int32, z.shape, 1)
            qry = qi * tq + lo_row + lax.broadcasted_iota(jnp.int32, z.shape, 0)
            earlier = key < qry
            log_fail = jnp.where(earlier, log_fail, 0.0)
        hi = log_fail.astype(BF16)
        lo = (log_fail - hi.astype(F32)).astype(BF16)
        log_stick = _dot(hi, u_ref[...]) + _dot(lo, u_ref[...])
        carry = carry_ref[j, lo_row:hi_row, :]
        w = jnp.exp(z + log_fail + log_stick + jnp.tile(carry, (1, tk // LANES)))
        if diagonal:
            w = jnp.where(earlier, w, 0.0)
        acc_ref[j, lo_row:hi_row, :] += _dot(w.astype(BF16), v_ref[pl.ds(start, tk), :])
        carry_ref[j, lo_row:hi_row, :] = carry + (log_stick[:, :1] + log_fail[:, :1])

    def sweep(c, diagonal, lo_row=0, next_lo_row=0, hi_row=tq):
        logits(1, c, lo_row, hi_row)
        absorb(0, c, diagonal, lo_row, hi_row)
        logits(0, jnp.maximum(c - 1, 0), next_lo_row)
        absorb(1, c, diagonal, lo_row, hi_row)

    n_sub = tq // tk
    first = qi * n_sub + n_sub - 1
    logits(0, first, (n_sub - 1) * tk)
    for sub in reversed(range(n_sub)):
        sweep(qi * n_sub + sub, True, sub * tk, max(sub - 1, 0) * tk)

    def sticks_alive(lo_row, hi_row):
        return jnp.max(carry_ref[:, lo_row:hi_row, :]) > -SB_EXP_UNDERFLOW

    def earlier_chunk(state):
        c, _, rest_alive = state

        @pl.when(rest_alive)
        def _():
            sweep(c, False)

        @pl.when(jnp.logical_not(rest_alive))
        def _():
            sweep(c, False, hi_row=tk)

        return c - 1, sticks_alive(0, tk), sticks_alive(tk, tq)

    lax.while_loop(lambda state: (state[0] >= 0) & (state[1] | state[2]), earlier_chunk,
                   (qi * n_sub - 1, sticks_alive(0, tk), sticks_alive(tk, tq)))
    o_ref[...] = jnp.where(lane < SB_HEAD_DIM, acc_ref[0], acc_ref[1]).astype(o_ref.dtype)


def _sb_attention(qkv, tq, tk):
    b, s, _ = qkv.shape
    assert tq % tk == 0 and tq > tk and s % tq == 0
    n_pairs = SB_WIDTH // LANES
    tri = (jnp.arange(tk)[:, None] > jnp.arange(tk)[None, :]).astype(BF16)
    return pl.pallas_call(
        functools.partial(_sb_kernel, tq=tq, tk=tk),
        grid=(b, n_pairs, s // tq),
        in_specs=[
            pl.BlockSpec((None, tq, LANES), lambda bi, hp, qi: (bi, qi, hp)),
            pl.BlockSpec((None, s, LANES), lambda bi, hp, qi: (bi, 0, n_pairs + hp)),
            pl.BlockSpec((None, s, LANES), lambda bi, hp, qi: (bi, 0, 2 * n_pairs + hp)),
            _const_spec((tk, tk)),
        ],
        out_specs=pl.BlockSpec((None, tq, LANES), lambda bi, hp, qi: (bi, qi, hp)),
        out_shape=jax.ShapeDtypeStruct((b, s, SB_WIDTH), BF16),
        scratch_shapes=[pltpu.VMEM((2, tq, LANES), F32), pltpu.VMEM((2, tq, LANES), F32),
                        pltpu.VMEM((2, tq, tk), F32)],
        compiler_params=_params("arbitrary", "arbitrary", "arbitrary"),
        name="stick_breaking_attention",
    )(qkv, qkv, qkv, tri)


def _mla_kernel(q_ref, k_ref, vt_ref, o_ref, acc_ref, m_ref, s_ref, *, tq):
    qi = pl.program_id(2)
    acc_ref[...] = jnp.zeros_like(acc_ref)
    m_ref[...] = jnp.full(m_ref.shape, -jnp.inf, F32)

    def scores(j, c):
        start = pl.multiple_of(c * tq, tq)
        s_ref[j] = _dot_nt(k_ref[pl.ds(start, tq), j * LANES:(j + 1) * LANES],
                           q_ref[:, j * LANES:(j + 1) * LANES])

    def update(j, c, diagonal, k0=0, k1=tq, q0=0):
        start = pl.multiple_of(c * tq + k0, LANES)
        s = s_ref[j, k0:k1, q0:]
        if diagonal:
            key = k0 + lax.broadcasted_iota(jnp.int32, s.shape, 0)
            qry = q0 + lax.broadcasted_iota(jnp.int32, s.shape, 1)
            s = jnp.where(key <= qry, s, -jnp.inf)
        m_prev = m_ref[j, :, q0:]
        m_new = jnp.maximum(m_prev, jnp.max(s, axis=0, keepdims=True))
        p = jnp.exp2(s - m_new)
        vt = vt_ref[j * MLA_V_ROWS:(j + 1) * MLA_V_ROWS, pl.ds(start, k1 - k0)]
        acc_ref[j, :, q0:] = jnp.exp2(m_prev - m_new) * acc_ref[j, :, q0:] + _dot(vt, p.astype(BF16))
        m_ref[j, :, q0:] = m_new

    def update_diagonal(j):
        half = tq // 2
        update(j, qi, True, 0, half, 0)
        update(j, qi, True, half, tq, half)


    scores(0, 0)

    def earlier_block(c, _):
        scores(1, c)
        update(0, c, False)
        scores(0, c + 1)
        update(1, c, False)
        return 0

    lax.fori_loop(0, qi, earlier_block, 0)
    scores(1, qi)
    update_diagonal(0)
    update_diagonal(1)
    out_t = jnp.concatenate([acc_ref[j, :MLA_V_DIM, :] / acc_ref[j, MLA_V_DIM:MLA_V_DIM + 1, :] for j in range(2)],
                            axis=0)
    o_ref[...] = out_t.T.astype(o_ref.dtype)


def _mla_attention(q, k, vt, tq):
    b, s, _ = q.shape
    n_pairs = MLA_WIDTH // LANES
    return pl.pallas_call(
        functools.partial(_mla_kernel, tq=tq),
        grid=(b, n_pairs, s // tq),
        in_specs=[
            pl.BlockSpec((None, tq, 2 * LANES), lambda bi, hp, qi: (bi, qi, hp)),
            pl.BlockSpec((None, s, 2 * LANES), lambda bi, hp, qi: (bi, 0, hp)),
            pl.BlockSpec((2 * MLA_V_ROWS, s), lambda bi, hp, qi: (hp, bi)),
        ],
        out_specs=pl.BlockSpec((None, tq, LANES), lambda bi, hp, qi: (bi, qi, hp)),
        out_shape=jax.ShapeDtypeStruct((b, s, MLA_WIDTH), BF16),
        scratch_shapes=[pltpu.VMEM((2, MLA_V_ROWS, tq), F32), pltpu.VMEM((2, 1, tq), F32),
                        pltpu.VMEM((2, tq, tq), F32)],
        compiler_params=_params("arbitrary", "arbitrary", "arbitrary"),
        name="latent_attention",
    )(q, k, vt)


def _swiglu(x, wg_ref, wu_ref, wd_ref, f_chunk):
    d_ff = wg_ref.shape[-1]
    acc = None
    for c in range(d_ff // f_chunk):
        cols = slice(c * f_chunk, (c + 1) * f_chunk)
        gate = _dot(x, wg_ref[:, cols])
        up = _dot(x, wu_ref[:, cols])
        act = (gate * jax.nn.sigmoid(gate) * up).astype(BF16)
        part = _dot(act, wd_ref[cols, :])
        acc = part if acc is None else acc + part
    return acc


def _outproj_ffn_kernel(x_ref, osb_ref, omla_ref, wsb_ref, wmla_ref, g_ref, wg_ref, wu_ref, wd_ref, o_ref, *, f_chunk):
    h = x_ref[...] + _dot(osb_ref[...], wsb_ref[...]) + _dot(omla_ref[...], wmla_ref[...])
    hn = (_rms(h, h.shape[-1]) * g_ref[...]).astype(BF16)
    o_ref[...] = h + _swiglu(hn, wg_ref, wu_ref, wd_ref, f_chunk)


def _outproj_dense_ffn(x2, o_sb, o_mla, w_out, norm_g, w_gate, w_up, w_down, tm, f_chunk):
    t, d = x2.shape
    d_ff = w_gate.shape[-1]
    row = lambda i: (i, 0)
    resident = lambda shape: pl.BlockSpec(shape, lambda i: (0, 0), pipeline_mode=pl.Buffered(1))
    return pl.pallas_call(
        functools.partial(_outproj_ffn_kernel, f_chunk=f_chunk),
        grid=(t // tm,),
        in_specs=[pl.BlockSpec((tm, d), row), pl.BlockSpec((tm, SB_WIDTH), row), pl.BlockSpec((tm, MLA_WIDTH), row),
                  resident((SB_WIDTH, d)), resident((MLA_WIDTH, d)), resident((1, d)),
                  resident((d, d_ff)), resident((d, d_ff)), resident((d_ff, d))],
        out_specs=pl.BlockSpec((tm, d), row),
        out_shape=jax.ShapeDtypeStruct((t, d), F32),
        compiler_params=_params("arbitrary"),
        name="out_projection_dense_swiglu",
    )(x2, o_sb, o_mla, w_out[:SB_WIDTH].astype(BF16), w_out[SB_WIDTH:].astype(BF16), norm_g.reshape(1, d),
      w_gate.astype(BF16), w_up.astype(BF16), w_down.astype(BF16))


def _layer0(x2, b, s, att_norm, att_w_in, q_lat_g, w_q_up, kv_lat_g, w_kv_up, q_g, k_g, w_out,
            dffn_norm, w_gate, w_up, w_down):
    qkv, qm, km, vm = _in_projection(x2, s, att_norm, att_w_in, q_lat_g, w_q_up, kv_lat_g, w_kv_up, q_g, k_g,
                                     TOKEN_TILE)
    o_sb = _sb_attention(qkv.reshape(b, s, -1), SB_Q_TILE, SB_K_TILE).reshape(b * s, -1)
    o_mla = _mla_attention(qm.reshape(b, s, -1), km.reshape(b, s, -1), vm, MLA_Q_TILE).reshape(b * s, -1)
    return _outproj_dense_ffn(x2, o_sb, o_mla, w_out, dffn_norm, w_gate, w_up, w_down, TOKEN_TILE, FFN_CHUNK)


SSM_SEGS = 8
SSM_SEG_LEN = 128
SSM_TILE = SSM_SEGS * SSM_SEG_LEN
SSM_CH_CHUNK = LANES
SSM_ST_CHUNK = SSM_CH_CHUNK // SSM_GROUP * SSM_STATE
SSM_SCAN_TILES = 8


def _s5_prep_kernel(are_ref, aim_ref, logdt_ref, bre_ref, bim_ref, cre_ref, cim_ref,
                    wbu_ref, wc_ref, pow_ref):
    n = SSM_NSTATE
    a_re = jnp.minimum(are_ref[...], EIG_RE_MAX)
    a_im = aim_ref[...]
    dt = jnp.exp(logdt_ref[...])
    mag = jnp.exp(a_re * dt)
    lam_re = mag * jnp.cos(a_im * dt)
    lam_im = mag * jnp.sin(a_im * dt)
    den = a_re * a_re + a_im * a_im
    num_re = lam_re - 1.0
    coef_re = (num_re * a_re + lam_im * a_im) / den
    coef_im = (lam_im * a_re - num_re * a_im) / den
    b_re = bre_ref[...]
    b_im = bim_ref[...]
    in_group = (lax.broadcasted_iota(jnp.int32, (SSM_CH_CHUNK, n), 0) // SSM_GROUP
                == lax.broadcasted_iota(jnp.int32, (SSM_CH_CHUNK, n), 1) % SSM_ST_CHUNK // SSM_STATE)
    wbu_ref[:, :n] = jnp.where(in_group, coef_re * b_re - coef_im * b_im, 0.0).astype(BF16)
    wbu_ref[:, n:] = jnp.where(in_group, coef_re * b_im + coef_im * b_re, 0.0).astype(BF16)
    in_group_t = (lax.broadcasted_iota(jnp.int32, (n, SSM_CH_CHUNK), 0) % SSM_ST_CHUNK // SSM_STATE
                  == lax.broadcasted_iota(jnp.int32, (n, SSM_CH_CHUNK), 1) // SSM_GROUP)
    wc_ref[:n, :] = jnp.where(in_group_t, cre_ref[...], 0.0).astype(BF16)
    wc_ref[n:, :] = jnp.where(in_group_t, -cim_ref[...], 0.0).astype(BF16)
    steps = jnp.where(lax.broadcasted_iota(jnp.int32, (SSM_SEGS, n), 0) == 0, 1.0, float(SSM_SEG_LEN))
    mag_k = jnp.exp(steps * (a_re * dt))
    ang_k = steps * (a_im * dt)
    pow_ref[:, :n] = mag_k * jnp.cos(ang_k)
    pow_ref[:, n:] = mag_k * jnp.sin(ang_k)


def _s5_prepare(a_re, a_im, log_dt, b_re, b_im, c_re, c_im):
    n = SSM_NSTATE
    row = lambda a: a.reshape(1, n)
    reps = SSM_CH_CHUNK // SSM_GROUP
    b_t = lambda w: jnp.tile(w.transpose(2, 0, 1).reshape(SSM_GROUP, n), (reps, 1))
    c_t = lambda w: jnp.tile(w.transpose(0, 2, 1).reshape(n, SSM_GROUP), (1, reps))
    return pl.pallas_call(
        _s5_prep_kernel,
        out_shape=[jax.ShapeDtypeStruct((SSM_CH_CHUNK, 2 * n), BF16),
                   jax.ShapeDtypeStruct((2 * n, SSM_CH_CHUNK), BF16),
                   jax.ShapeDtypeStruct((SSM_SEGS, 2 * n), F32)],
        compiler_params=pltpu.CompilerParams(vmem_limit_bytes=VMEM_LIMIT),
        name="s5_discretize",
    )(row(a_re), row(a_im), row(jnp.repeat(log_dt, SSM_STATE)), b_t(b_re), b_t(b_im), c_t(c_re), c_t(c_im))


def _s5_kernel(h_ref, g_ref, win_ref, wbu_ref, pow_ref, wc_ref, d_ref, wglu_ref, o_ref,
               ut_ref, up_ref, st_ref, carry_ref, yp_ref, yt_ref):
    n = SSM_NSTATE
    nt = n // LANES
    n_ch = SSM_WIDTH // LANES
    per = SSM_ST_CHUNK // LANES
    d_model = h_ref.shape[-1]

    @pl.when(pl.program_id(1) == 0)
    def _():
        carry_ref[...] = jnp.zeros_like(carry_ref)

    x = h_ref[...]
    xn = _rms(x, d_model) * g_ref[...]
    u = _dot(xn.astype(BF16), win_ref[...])
    for k in range(n_ch):
        ut_ref[k] = u[:, k * LANES:(k + 1) * LANES]

    def to_step_major(i, _):
        dst = pl.ds(pl.multiple_of(i * SSM_SEGS, SSM_SEGS), SSM_SEGS)
        for k in range(n_ch):
            up_ref[k, dst, :] = ut_ref[k, pl.ds(i, SSM_SEGS, stride=SSM_SEG_LEN), :]
        return 0

    lax.fori_loop(0, SSM_SEG_LEN, to_step_major, 0)

    for c in range(n_ch):
        u16 = up_ref[c].astype(BF16)
        for part in range(2):
            cols = slice(part * n + c * SSM_ST_CHUNK, part * n + (c + 1) * SSM_ST_CHUNK)
            bu = _dot(u16, wbu_ref[:, cols])
            for k in range(per):
                st_ref[part * nt + c * per + k] = bu[:, k * LANES:(k + 1) * LANES]

    for grp in range(nt // SSM_SCAN_TILES):
        tiles = range(grp * SSM_SCAN_TILES, (grp + 1) * SSM_SCAN_TILES)
        re_l = [slice(k * LANES, (k + 1) * LANES) for k in tiles]
        im_l = [slice(n + k * LANES, n + (k + 1) * LANES) for k in tiles]
        lam_re = [jnp.broadcast_to(pow_ref[0:1, c], (SSM_SEGS, LANES)) for c in re_l]
        lam_im = [jnp.broadcast_to(pow_ref[0:1, c], (SSM_SEGS, LANES)) for c in im_l]

        def scan(state, store):
            def step(i, state):
                rows = pl.ds(pl.multiple_of(i * SSM_SEGS, SSM_SEGS), SSM_SEGS)
                new_state = []
                for idx, k in enumerate(tiles):
                    s_re, s_im = state[idx]
                    n_re = lam_re[idx] * s_re - lam_im[idx] * s_im + st_ref[k, rows, :]
                    n_im = lam_re[idx] * s_im + lam_im[idx] * s_re + st_ref[nt + k, rows, :]
                    if store:
                        st_ref[k, rows, :] = n_re
                        st_ref[nt + k, rows, :] = n_im
                    new_state.append((n_re, n_im))
                return tuple(new_state)

            return lax.fori_loop(0, SSM_SEG_LEN, step, state)

        zero = jnp.zeros((SSM_SEGS, LANES), F32)
        ends = scan(tuple((zero, zero) for _ in tiles), False)
        starts = []
        for idx in range(len(tiles)):
            e_re, e_im = ends[idx]
            p_re = pow_ref[1:2, re_l[idx]]
            p_im = pow_ref[1:2, im_l[idx]]
            c_re = carry_ref[:, re_l[idx]]
            c_im = carry_ref[:, im_l[idx]]
            rows_re, rows_im = [], []
            for j in range(SSM_SEGS):
                rows_re.append(c_re)
                rows_im.append(c_im)
                c_re, c_im = (e_re[j:j + 1] + (p_re * c_re - p_im * c_im),
                              e_im[j:j + 1] + (p_re * c_im + p_im * c_re))
            carry_ref[:, re_l[idx]] = c_re
            carry_ref[:, im_l[idx]] = c_im
            starts.append((jnp.concatenate(rows_re, axis=0), jnp.concatenate(rows_im, axis=0)))
        scan(tuple(starts), True)

    for c in range(n_ch):
        h_re = jnp.concatenate([st_ref[c * per + k] for k in range(per)], axis=-1).astype(BF16)
        h_im = jnp.concatenate([st_ref[nt + c * per + k] for k in range(per)], axis=-1).astype(BF16)
        rows_re = slice(c * SSM_ST_CHUNK, (c + 1) * SSM_ST_CHUNK)
        rows_im = slice(n + c * SSM_ST_CHUNK, n + (c + 1) * SSM_ST_CHUNK)
        yp_ref[c] = _dot(h_re, wc_ref[rows_re, :]) + _dot(h_im, wc_ref[rows_im, :])

    def to_token_major(i, _):
        src = pl.ds(pl.multiple_of(i * SSM_SEGS, SSM_SEGS), SSM_SEGS)
        for k in range(n_ch):
            yt_ref[k, pl.ds(i, SSM_SEGS, stride=SSM_SEG_LEN), :] = yp_ref[k, src, :]
        return 0

    lax.fori_loop(0, SSM_SEG_LEN, to_token_major, 0)
    y = jnp.concatenate([yt_ref[k] for k in range(n_ch)], axis=-1)
    y = jax.nn.gelu(y + d_ref[...] * u)
    z = _dot(y.astype(BF16), wglu_ref[...])
    o_ref[...] = x + z[:, :d_model] * jax.nn.sigmoid(z[:, d_model:])


def _s5_layer(h, b, s, norm_g, w_in, a_re, a_im, log_dt, b_re, b_im, c_re, c_im, d_skip, w_glu):
    t, d = h.shape
    n = SSM_NSTATE
    w_bu, w_c, powers = _s5_prepare(a_re, a_im, log_dt, b_re, b_im, c_re, c_im)
    tiles = s // SSM_TILE
    resident = lambda shape: pl.BlockSpec(shape, lambda bi, ti: (0, 0), pipeline_mode=pl.Buffered(1))
    row = lambda bi, ti: (bi * tiles + ti, 0)
    ch_tiles = pltpu.VMEM((SSM_WIDTH // LANES, SSM_TILE, LANES), F32)
    return pl.pallas_call(
        _s5_kernel,
        grid=(b, tiles),
        in_specs=[pl.BlockSpec((SSM_TILE, d), row), resident((1, d)), resident((d, SSM_WIDTH)),
                  resident((SSM_CH_CHUNK, 2 * n)), resident((SSM_SEGS, 2 * n)), resident((2 * n, SSM_CH_CHUNK)),
                  resident((1, SSM_WIDTH)), resident((SSM_WIDTH, 2 * d))],
        out_specs=pl.BlockSpec((SSM_TILE, d), row),
        out_shape=jax.ShapeDtypeStruct((t, d), F32),
        scratch_shapes=[ch_tiles, ch_tiles, pltpu.VMEM((2 * n // LANES, SSM_TILE, LANES), F32),
                        pltpu.VMEM((1, 2 * n), F32), ch_tiles, ch_tiles],
        compiler_params=_params("arbitrary", "arbitrary"),
        name="s5_mixer",
    )(h, norm_g.reshape(1, d), w_in.astype(BF16), w_bu, powers, w_c, d_skip.reshape(1, -1), w_glu.astype(BF16))


MOE_BLOCK = 512
META_E1, META_E2, META_G1, META_G2, META_R1, META_R2 = range(6)


def _router_kernel(h_ref, g_ref, wr_ref, tri_ref, meta_ref, counts_ref, run_ref):
    @pl.when(pl.program_id(0) == 0)
    def _():
        run_ref[...] = jnp.zeros_like(run_ref)

    x = h_ref[...]
    xn = _rms(x, x.shape[-1]) * g_ref[...]
    logits = jnp.dot(xn, wr_ref[...], preferred_element_type=F32, precision=lax.Precision.HIGHEST)
    tm = logits.shape[0]
    e = lax.broadcasted_iota(jnp.int32, (tm, N_EXPERTS), 1)
    v1 = jnp.max(logits, axis=-1, keepdims=True)
    i1 = jnp.min(jnp.where(logits == v1, e, N_EXPERTS), axis=-1, keepdims=True)
    rest = jnp.where(e == i1, -jnp.inf, logits)
    v2 = jnp.max(rest, axis=-1, keepdims=True)
    i2 = jnp.min(jnp.where(rest == v2, e, N_EXPERTS), axis=-1, keepdims=True)
    ex = jnp.exp(v2 - v1)
    g1 = 1.0 / (1.0 + ex)
    g2 = ex / (1.0 + ex)
    oh1 = (e == i1).astype(F32)
    oh2 = (e == i2).astype(F32)
    both = oh1 + oh2
    before = _dot(tri_ref[...], both.astype(BF16)) + run_ref[...]
    r1 = jnp.sum(oh1 * before, axis=-1, keepdims=True)
    r2 = jnp.sum(oh2 * before, axis=-1, keepdims=True)
    run_ref[...] += jnp.sum(both, axis=0, keepdims=True)
    counts_ref[...] = run_ref[...]
    meta = jnp.zeros((tm, N_EXPERTS), F32)
    for lane, val in ((META_E1, i1.astype(F32)), (META_E2, i2.astype(F32)), (META_G1, g1), (META_G2, g2),
                      (META_R1, r1), (META_R2, r2)):
        meta = jnp.where(e == lane, val, meta)
    meta_ref[...] = meta


def _router(h, norm_g, w_router, tm):
    t, d = h.shape
    row = lambda i: (i, 0)
    tri = (jnp.arange(tm)[:, None] > jnp.arange(tm)[None, :]).astype(BF16)
    return pl.pallas_call(
        _router_kernel,
        grid=(t // tm,),
        in_specs=[pl.BlockSpec((tm, d), row), _const_spec((1, d)), _const_spec((d, N_EXPERTS)),
                  _const_spec((tm, tm))],
        out_specs=[pl.BlockSpec((tm, N_EXPERTS), row), _const_spec((1, N_EXPERTS))],
        out_shape=[jax.ShapeDtypeStruct((t, N_EXPERTS), F32), jax.ShapeDtypeStruct((1, N_EXPERTS), F32)],
        scratch_shapes=[pltpu.VMEM((1, N_EXPERTS), F32)],
        compiler_params=_params("arbitrary"),
        name="moe_router",
    )(h, norm_g.reshape(1, d), w_router, tri)


def _slot_map_kernel(d1_ref, d2_ref, tok_ref):
    def clear(slot, _):
        tok_ref[slot] = 0
        return 0

    lax.fori_loop(0, tok_ref.shape[0], clear, 0, unroll=16)

    def place(tok, _):
        tok_ref[d1_ref[tok]] = tok
        tok_ref[d2_ref[tok]] = tok
        return 0

    lax.fori_loop(0, d1_ref.shape[0], place, 0, unroll=8)


def _slot_map(dest1, dest2, n_slots):
    smem = pl.BlockSpec(memory_space=pltpu.SMEM)
    return pl.pallas_call(
        _slot_map_kernel,
        in_specs=[smem, smem], out_specs=smem,
        out_shape=jax.ShapeDtypeStruct((n_slots,), jnp.int32),
        name="moe_slot_map",
    )(dest1, dest2)


def _row_copy(src_ref, src_row, dst_ref, dst_row, sem):
    return pltpu.make_async_copy(src_ref.at[pl.ds(src_row, 1)], dst_ref.at[pl.ds(dst_row, 1)], sem)


def _expert_ffn_kernel(be_ref, bv_ref, tok_ref, h_ref, g_ref, wg_ref, wu_ref, wd_ref, ys_ref, buf_ref, sem, *, f_chunk):
    del be_ref
    i = pl.program_id(0)
    last = pl.num_programs(0) - 1
    cur = i % 2

    def gather(block, buf):
        base = block * MOE_BLOCK
        for r in range(MOE_BLOCK):
            _row_copy(h_ref, tok_ref[base + r], buf_ref.at[buf], r, sem.at[buf]).start(priority=r % 2)

    def wait_gather(buf):
        pltpu.make_async_copy(h_ref.at[pl.ds(0, MOE_BLOCK)], buf_ref.at[buf], sem.at[buf]).wait()

    @pl.when(i == 0)
    def _():
        gather(0, 0)

    gather(jnp.minimum(i + 1, last), 1 - cur)
    wait_gather(cur)
    valid = bv_ref[i] != 0

    @pl.when(valid)
    def _():
        x = buf_ref[cur]
        xn = (_rms(x, x.shape[-1]) * g_ref[...]).astype(BF16)
        ys_ref[...] = _swiglu(xn, wg_ref, wu_ref, wd_ref, f_chunk)

    @pl.when(jnp.logical_not(valid))
    def _():
        ys_ref[...] = jnp.zeros_like(ys_ref)

    @pl.when(i == last)
    def _():
        wait_gather(1 - cur)


def _expert_ffn(h, norm_g, slot_tok, blk_expert, blk_valid, w_gate, w_up, w_down, f_chunk):
    n_slots = slot_tok.shape[0]
    d = h.shape[-1]
    d_ff = w_gate.shape[-1]
    return pl.pallas_call(
        functools.partial(_expert_ffn_kernel, f_chunk=f_chunk),
        grid_spec=pltpu.PrefetchScalarGridSpec(
            num_scalar_prefetch=3, grid=(n_slots // MOE_BLOCK,),
            in_specs=[pl.BlockSpec(memory_space=pl.ANY),
                      pl.BlockSpec((1, d), lambda i, be, bv, st: (0, 0)),
                      pl.BlockSpec((None, d, d_ff), lambda i, be, bv, st: (be[i], 0, 0),
                                   pipeline_mode=pl.Buffered(1)),
                      pl.BlockSpec((None, d, d_ff), lambda i, be, bv, st: (be[i], 0, 0),
                                   pipeline_mode=pl.Buffered(1)),
                      pl.BlockSpec((None, d_ff, d), lambda i, be, bv, st: (be[i], 0, 0),
                                   pipeline_mode=pl.Buffered(1))],
            out_specs=pl.BlockSpec((MOE_BLOCK, d), lambda i, be, bv, st: (i, 0)),
            scratch_shapes=[pltpu.VMEM((2, MOE_BLOCK, d), F32), pltpu.SemaphoreType.DMA((2,))]),
        out_shape=jax.ShapeDtypeStruct((n_slots, d), F32),
        compiler_params=_params("arbitrary"),
        name="expert_swiglu",
    )(blk_expert, blk_valid, slot_tok, h, norm_g.reshape(1, d), w_gate.astype(BF16), w_up.astype(BF16),
      w_down.astype(BF16))


def _combine_kernel(d1_ref, d2_ref, h_ref, meta_ref, ys_ref, o_ref, a_ref, b_ref, sem, *, tm):
    base = pl.program_id(0) * tm

    def issue(r, _):
        tok = base + r
        _row_copy(ys_ref, d1_ref[tok], a_ref, r, sem).start(priority=0)
        _row_copy(ys_ref, d2_ref[tok], b_ref, r, sem).start(priority=1)
        return 0

    lax.fori_loop(0, tm, issue, 0, unroll=8)

    pltpu.make_async_copy(ys_ref.at[pl.ds(0, tm)], a_ref, sem).wait()
    pltpu.make_async_copy(ys_ref.at[pl.ds(0, tm)], b_ref, sem).wait()
    meta = meta_ref[...]
    g1 = meta[:, META_G1:META_G1 + 1]
    g2 = meta[:, META_G2:META_G2 + 1]
    o_ref[...] = h_ref[...] + (g1 * a_ref[...] + g2 * b_ref[...])


def _combine(h, meta, ys, dest1, dest2, tm):
    t, d = h.shape
    row = lambda i, d1, d2: (i, 0)
    return pl.pallas_call(
        functools.partial(_combine_kernel, tm=tm),
        grid_spec=pltpu.PrefetchScalarGridSpec(
            num_scalar_prefetch=2, grid=(t // tm,),
            in_specs=[pl.BlockSpec((tm, d), row), pl.BlockSpec((tm, N_EXPERTS), row),
                      pl.BlockSpec(memory_space=pl.ANY)],
            out_specs=pl.BlockSpec((tm, d), row),
            scratch_shapes=[pltpu.VMEM((tm, d), F32), pltpu.VMEM((tm, d), F32), pltpu.SemaphoreType.DMA(())]),
        out_shape=jax.ShapeDtypeStruct((t, d), F32),
        compiler_params=_params("arbitrary"),
        name="moe_combine",
    )(dest1, dest2, h, meta, ys)


def _moe_layer(h, norm_g, w_router, w_gate, w_up, w_down):
    t, d = h.shape
    meta, counts = _router(h, norm_g, w_router, ROUTER_TILE)
    counts = counts.reshape(N_EXPERTS).astype(jnp.int32)
    padded = (counts + MOE_BLOCK - 1) // MOE_BLOCK * MOE_BLOCK
    pad_end = jnp.cumsum(padded)
    pad_start = pad_end - padded
    e1 = meta[:, META_E1].astype(jnp.int32)
    e2 = meta[:, META_E2].astype(jnp.int32)
    dest1 = pad_start[e1] + meta[:, META_R1].astype(jnp.int32)
    dest2 = pad_start[e2] + meta[:, META_R2].astype(jnp.int32)
    n_slots = 2 * t + N_EXPERTS * MOE_BLOCK
    blk_start = jnp.arange(n_slots // MOE_BLOCK, dtype=jnp.int32) * MOE_BLOCK
    blk_expert = jnp.minimum(jnp.sum(blk_start[:, None] >= pad_end[None, :], axis=1), N_EXPERTS - 1).astype(jnp.int32)
    blk_valid = (blk_start < pad_end[-1]).astype(jnp.int32)
    slot_tok = _slot_map(dest1, dest2, n_slots)
    ys = _expert_ffn(h, norm_g, slot_tok, blk_expert, blk_valid, w_gate, w_up, w_down, FFN_CHUNK)
    return _combine(h, meta, ys, dest1, dest2, ROUTER_TILE)


def kernel(x, att_norm, att_w_in, att_q_latent_norm, att_w_q_up, att_kv_latent_norm, att_w_kv_up, att_q_norm, att_k_norm, att_w_out, dffn_norm, dffn_w_gate, dffn_w_up, dffn_w_down, ssm_norm, ssm_w_in, ssm_a_re, ssm_a_im, ssm_log_dt, ssm_b_re, ssm_b_im, ssm_c_re, ssm_c_im, ssm_d, ssm_w_glu, moe_norm, moe_router, moe_w_gate, moe_w_up, moe_w_down):
    b, s, d = x.shape
    x2 = x.reshape(b * s, d)
    h = _layer0(x2, b, s, att_norm[0], att_w_in[0], att_q_latent_norm[0], att_w_q_up[0], att_kv_latent_norm[0],
                att_w_kv_up[0], att_q_norm[0], att_k_norm[0], att_w_out[0], dffn_norm[0], dffn_w_gate[0],
                dffn_w_up[0], dffn_w_down[0])
    h = _s5_layer(h, b, s, ssm_norm[0], ssm_w_in[0], ssm_a_re[0], ssm_a_im[0], ssm_log_dt[0], ssm_b_re[0],
                  ssm_b_im[0], ssm_c_re[0], ssm_c_im[0], ssm_d[0], ssm_w_glu[0])
    h = _moe_layer(h, moe_norm[0], moe_router[0], moe_w_gate[0], moe_w_up[0], moe_w_down[0])
    return h.reshape(b, s, d)
```
